```python
import jax, jax.numpy as jnp
from jax import lax
import numpy as np

D_MODEL = 1024
BATCH = 8
SEQ = 2048
DEPTH = 2
DEC_BATCH = 128
DEC_SEQ = 1
PAST_LEN = 16384
PAGE_SIZE = 128

CONV_W = 4
D_RG = D_MODEL
RG_HEADS = 8
RG_BW = D_RG // RG_HEADS
RG_C = 8.0
D_ML = D_MODEL
ML_HEADS = 4
ML_DH = D_ML // ML_HEADS
D_RET = D_MODEL
RET_HEADS = 4
RET_DH = D_RET // RET_HEADS
N_BRANCH = 3
CHUNK = 128
ROPE_BASE = 10000.0
EPS = 1e-6
IN_SPLITS = (D_RG, D_RG, D_ML, D_ML, D_RET, D_RET, D_RET, D_RET, N_BRANCH * D_MODEL)
D_IN = 2 * D_RG + 2 * D_ML + 4 * D_RET + N_BRANCH * D_MODEL

kernel_name = 'hybrid_rglru_mlstm_retention_step'


def rmsnorm(x, g):
    xf = x.astype(jnp.float32)
    y = xf * lax.rsqrt(jnp.mean(xf * xf, axis=-1, keepdims=True) + EPS)
    return y * g.astype(jnp.float32)


def headnorm(h, g, n_heads):
    B, L, D = h.shape
    hf = h.astype(jnp.float32).reshape(B, L, n_heads, D // n_heads)
    mu = jnp.mean(hf, axis=-1, keepdims=True)
    var = jnp.mean(jnp.square(hf - mu), axis=-1, keepdims=True)
    y = ((hf - mu) * lax.rsqrt(var + EPS)).reshape(B, L, D)
    return y * g.astype(jnp.float32)


def rope(x, pos):
    half = x.shape[-1] // 2
    inv = ROPE_BASE ** (-jnp.linspace(0.0, 1.0, half, dtype=jnp.float32))
    ang = pos.astype(jnp.float32)[:, None] * inv[None, :]
    cos = jnp.cos(ang)[None, :, None, :]
    sin = jnp.sin(ang)[None, :, None, :]
    x1, x2 = x[..., :half], x[..., half:]
    return jnp.concatenate([x1 * cos - x2 * sin, x1 * sin + x2 * cos], axis=-1)


def causal_conv(x, buf, w, b):
    L = x.shape[1]
    xx = jnp.concatenate([buf.astype(x.dtype), x], axis=1)
    y = b + sum(xx[:, j:j + L] * w[j] for j in range(CONV_W))
    return y, xx[:, L:]


def _lin_comb(left, right):
    a_l, b_l = left
    a_r, b_r = right
    return a_l * a_r, a_r * b_l + b_r


def rg_lru(xc, h0, w_a, b_a, w_x, b_x, lam):
    B, L, _ = xc.shape
    xb = xc.reshape(B, L, RG_HEADS, RG_BW)
    r = jax.nn.sigmoid(jnp.einsum('blhi,hij->blhj', xb, w_a).reshape(B, L, D_RG) + b_a)
    i = jax.nn.sigmoid(jnp.einsum('blhi,hij->blhj', xb, w_x).reshape(B, L, D_RG) + b_x)
    log_a = -RG_C * r * jax.nn.softplus(-lam)
    a = jnp.exp(log_a)
    bterm = jnp.sqrt(-jnp.expm1(2.0 * log_a)) * (i * xc)
    bterm = bterm.at[:, 0].add(a[:, 0] * h0)
    _, h = lax.associative_scan(_lin_comb, (a, bterm), axis=1)
    return h, h[:, -1]


def mlstm_chunk(carry, inp):
    C0, n0, m0 = carry
    q, k, v, ig, lf = inp
    L = q.shape[2]
    F = jnp.cumsum(lf, axis=-1)
    u = ig - F
    m = F + jnp.maximum(m0[..., None], lax.cummax(u, axis=2))
    causal = jnp.tril(jnp.ones((L, L), dtype=bool))
    logD = (F - m)[..., :, None] + u[..., None, :]
    D = jnp.exp(jnp.where(causal, logD, -jnp.inf))
    inter = jnp.exp(F + m0[..., None] - m)
    s = jnp.einsum('bhtd,bhsd->bhts', q, k) * D
    num = inter[..., None] * jnp.einsum('bhtd,bhde->bhte', q, C0) + jnp.einsum('bhts,bhse->bhte', s, v)
    den = inter * jnp.einsum('bhtd,bhd->bht', q, n0) + jnp.sum(s, axis=-1)
    h = num / jnp.maximum(jnp.abs(den), jnp.exp(-m))[..., None]
    mL = m[..., -1]
    wL = jnp.exp(F[..., -1:] - mL[..., None] + u)
    decayL = jnp.exp(F[..., -1] + m0 - mL)
    C = decayL[..., None, None] * C0 + jnp.einsum('bhs,bhsd,bhse->bhde', wL, k, v)
    n = decayL[..., None] * n0 + jnp.einsum('bhs,bhsd->bhd', wL, k)
    return (C, n, mL), h


def retention_chunk(S0, inp):
    q, k, v = inp
    L = q.shape[2]
    lg = jnp.log1p(-jnp.exp2(-5.0 - jnp.arange(RET_HEADS, dtype=jnp.float32)))
    idx = jnp.arange(L, dtype=jnp.float32)
    causal = jnp.tril(jnp.ones((L, L), dtype=bool))
    rel = idx[:, None] - idx[None, :]
    D = jnp.exp(jnp.where(causal[None], rel[None] * lg[:, None, None], -jnp.inf))
    cross = jnp.exp((idx + 1.0)[None, :] * lg[:, None])
    wk = jnp.exp((L - 1.0 - idx)[None, :] * lg[:, None])
    decayL = jnp.exp(L * lg)
    s = jnp.einsum('bhtd,bhsd->bhts', q, k) * D[None]
    h = jnp.einsum('bhts,bhse->bhte', s, v) + jnp.einsum('bhtd,bhde->bhte', q, S0) * cross[None, :, :, None]
    S = decayL[None, :, None, None] * S0 + jnp.einsum('bhsd,bhse->bhde', k * wk[None, :, :, None], v)
    return S, h


def chunked_scan(step, carry, xs):
    L = xs[0].shape[2]
    c = CHUNK if L % CHUNK == 0 else L
    n = L // c
    xs = tuple(jnp.moveaxis(a.reshape(a.shape[:2] + (n, c) + a.shape[3:]), 2, 0) for a in xs)
    carry, ys = lax.scan(step, carry, xs)
    ys = jnp.moveaxis(ys, 0, 2)
    return carry, ys.reshape(ys.shape[:2] + (L,) + ys.shape[4:])


def _to_bhld(a):
    return jnp.transpose(a, (0, 2, 1, 3))


def zero_state(B):
    f = jnp.float32
    return (jnp.zeros((B, D_RG), f), jnp.zeros((B, CONV_W - 1, D_RG), f),
            jnp.zeros((B, ML_HEADS, ML_DH, ML_DH), f), jnp.zeros((B, ML_HEADS, ML_DH), f),
            jnp.zeros((B, ML_HEADS), f), jnp.zeros((B, CONV_W - 1, D_ML), f),
            jnp.zeros((B, RET_HEADS, RET_DH, RET_DH), f))


def mixer_layer(x, c, state, pos0, params):
    rg_h, rg_conv, ml_C, ml_n, ml_m, ml_conv, ret_S = tuple(s.astype(jnp.float32) for s in state)
    (w_ada, b_ada, norm_pre, norm_post, w_in, rg_conv_w, rg_conv_b, rg_w_a, rg_b_a, rg_w_x, rg_b_x,
     rg_lambda, ml_conv_w, ml_conv_b, ml_w_q, ml_w_k, ml_w_v, ml_w_if, ml_b_if, ml_skip, ml_gn,
     ret_gn, w_out) = params
    B, L, _ = x.shape
    shift, scale, gate = jnp.split(c.astype(jnp.float32) @ w_ada + b_ada, 3, axis=-1)
    u = rmsnorm(x, norm_pre) * (1.0 + scale[:, None]) + shift[:, None]
    proj = u @ w_in
    offs = []
    o = 0
    for s in IN_SPLITS[:-1]:
        o += s
        offs.append(o)
    x_rg, z_rg, x_ml, z_ml, q_r, k_r, v_r, g_r, g_br = jnp.split(proj, offs, axis=-1)

    xc_rg, rg_conv_new = causal_conv(x_rg, rg_conv, rg_conv_w, rg_conv_b)
    h_rg, rg_h_new = rg_lru(xc_rg, rg_h, rg_w_a, rg_b_a, rg_w_x, rg_b_x, rg_lambda)
    y_a = h_rg * jax.nn.silu(z_rg)

    xc_ml, ml_conv_new = causal_conv(x_ml, ml_conv, ml_conv_w, ml_conv_b)
    xc_ml = jax.nn.silu(xc_ml)
    xh = xc_ml.reshape(B, L, ML_HEADS, ML_DH)
    q = jnp.einsum('blhi,hij->blhj', xh, ml_w_q)
    k = jnp.einsum('blhi,hij->blhj', xh, ml_w_k) * (ML_DH ** -0.5)
    v = jnp.einsum('blhi,hij->blhj', x_ml.reshape(B, L, ML_HEADS, ML_DH), ml_w_v)
    gpre = jnp.concatenate([q.reshape(B, L, D_ML), k.reshape(B, L, D_ML), v.reshape(B, L, D_ML)], axis=-1) @ ml_w_if + ml_b_if
    ig = jnp.transpose(gpre[..., :ML_HEADS], (0, 2, 1))
    lf = jnp.transpose(jax.nn.log_sigmoid(gpre[..., ML_HEADS:]), (0, 2, 1))
    (ml_C_new, ml_n_new, ml_m_new), h_ml = chunked_scan(
        mlstm_chunk, (ml_C, ml_n, ml_m), (_to_bhld(q), _to_bhld(k), _to_bhld(v), ig, lf))
    h_ml = _to_bhld(h_ml).reshape(B, L, D_ML)
    y_b = (headnorm(h_ml, ml_gn, ML_HEADS) + ml_skip * xc_ml) * jax.nn.silu(z_ml)

    pos = pos0 + jnp.arange(L)
    qr = rope(q_r.reshape(B, L, RET_HEADS, RET_DH), pos)
    kr = rope(k_r.reshape(B, L, RET_HEADS, RET_DH), pos) * (RET_DH ** -0.5)
    vr = v_r.reshape(B, L, RET_HEADS, RET_DH)
    ret_S_new, h_ret = chunked_scan(retention_chunk, ret_S, (_to_bhld(qr), _to_bhld(kr), _to_bhld(vr)))
    h_ret = _to_bhld(h_ret).reshape(B, L, D_RET)
    y_c = headnorm(h_ret, ret_gn, RET_HEADS) * jax.nn.silu(g_r)

    g_a, g_b, g_c = jnp.split(jax.nn.sigmoid(g_br), 3, axis=-1)
    mix = g_a * y_a + g_b * y_b + g_c * y_c
    out = mix @ w_out
    x_new = x.astype(jnp.float32) + gate[:, None] * rmsnorm(out, norm_post)
    return x_new.astype(x.dtype), (rg_h_new, rg_conv_new, ml_C_new, ml_n_new, ml_m_new, ml_conv_new, ret_S_new)


def setup_inputs(seed: int = 0) -> dict:
    key = jax.random.key(seed)
    k = jax.random.split(key, 35)
    f = jnp.float32

    def nrm(kk, shape, s):
        return s * jax.random.normal(kk, shape, f)

    u = jax.random.uniform(k[22], (DEPTH, D_RG), f, minval=0.9, maxval=0.999)
    a = u ** (1.0 / RG_C)
    rg_lambda = jnp.log(a) - jnp.log1p(-a)
    b_ml_if = jnp.concatenate([nrm(k[29], (DEPTH, ML_HEADS), 0.02),
                               jnp.linspace(3.0, 6.0, ML_HEADS, dtype=f)[None, :] + nrm(k[30], (DEPTH, ML_HEADS), 0.02)], axis=-1)
    return {
        'x_prompt': nrm(k[0], (BATCH, SEQ, D_MODEL), 1.0),
        'x_sample': nrm(k[1], (DEC_BATCH, DEC_SEQ, D_MODEL), 1.0),
        'c_prompt': nrm(k[2], (BATCH, D_MODEL), 1.0),
        'c_sample': nrm(k[3], (DEC_BATCH, D_MODEL), 1.0),
        'state_rg_h': nrm(k[4], (DEPTH, DEC_BATCH, D_RG), 0.5),
        'state_rg_conv': nrm(k[5], (DEPTH, DEC_BATCH, CONV_W - 1, D_RG), 1.0),
        'state_ml_C': nrm(k[6], (DEPTH, DEC_BATCH, ML_HEADS, ML_DH, ML_DH), 0.5),
        'state_ml_n': nrm(k[7], (DEPTH, DEC_BATCH, ML_HEADS, ML_DH), 0.5),
        'state_ml_m': nrm(k[8], (DEPTH, DEC_BATCH, ML_HEADS), 1.0),
        'state_ml_conv': nrm(k[9], (DEPTH, DEC_BATCH, CONV_W - 1, D_ML), 1.0),
        'state_ret_S': nrm(k[10], (DEPTH, DEC_BATCH, RET_HEADS, RET_DH, RET_DH), 0.5),
        'w_ada': nrm(k[11], (DEPTH, D_MODEL, 3 * D_MODEL), 0.2 * D_MODEL ** -0.5),
        'b_ada': nrm(k[12], (DEPTH, 3 * D_MODEL), 0.02),
        'norm_pre': 1.0 + nrm(k[13], (DEPTH, D_MODEL), 0.02),
        'norm_post': 1.0 + nrm(k[14], (DEPTH, D_MODEL), 0.02),
        'w_in': nrm(k[15], (DEPTH, D_MODEL, D_IN), D_MODEL ** -0.5),
        'rg_conv_w': nrm(k[16], (DEPTH, CONV_W, D_RG), 0.5),
        'rg_conv_b': nrm(k[17], (DEPTH, D_RG), 0.02),
        'rg_w_a': nrm(k[18], (DEPTH, RG_HEADS, RG_BW, RG_BW), RG_BW ** -0.5),
        'rg_b_a': nrm(k[19], (DEPTH, D_RG), 0.02),
        'rg_w_x': nrm(k[20], (DEPTH, RG_HEADS, RG_BW, RG_BW), RG_BW ** -0.5),
        'rg_b_x': nrm(k[21], (DEPTH, D_RG), 0.02),
        'rg_lambda': rg_lambda,
        'ml_conv_w': nrm(k[23], (DEPTH, CONV_W, D_ML), 0.5),
        'ml_conv_b': nrm(k[24], (DEPTH, D_ML), 0.02),
        'ml_w_q': nrm(k[25], (DEPTH, ML_HEADS, ML_DH, ML_DH), ML_DH ** -0.5),
        'ml_w_k': nrm(k[26], (DEPTH, ML_HEADS, ML_DH, ML_DH), ML_DH ** -0.5),
        'ml_w_v': nrm(k[27], (DEPTH, ML_HEADS, ML_DH, ML_DH), ML_DH ** -0.5),
        'ml_w_if': nrm(k[28], (DEPTH, 3 * D_ML, 2 * ML_HEADS), (3 * D_ML) ** -0.5),
        'ml_b_if': b_ml_if,
        'ml_skip': 1.0 + nrm(k[31], (DEPTH, D_ML), 0.02),
        'ml_gn': 1.0 + nrm(k[32], (DEPTH, D_ML), 0.02),
        'ret_gn': 1.0 + nrm(k[33], (DEPTH, D_RET), 0.02),
        'w_out': nrm(k[34], (DEPTH, D_MODEL, D_MODEL), D_MODEL ** -0.5),
    }


def reference(x_prompt, x_sample, c_prompt, c_sample, state_rg_h, state_rg_conv, state_ml_C, state_ml_n,
              state_ml_m, state_ml_conv, state_ret_S, w_ada, b_ada, norm_pre, norm_post, w_in,
              rg_conv_w, rg_conv_b, rg_w_a, rg_b_a, rg_w_x, rg_b_x, rg_lambda, ml_conv_w, ml_conv_b,
              ml_w_q, ml_w_k, ml_w_v, ml_w_if, ml_b_if, ml_skip, ml_gn, ret_gn, w_out):
    params_all = (w_ada, b_ada, norm_pre, norm_post, w_in, rg_conv_w, rg_conv_b, rg_w_a, rg_b_a, rg_w_x,
                  rg_b_x, rg_lambda, ml_conv_w, ml_conv_b, ml_w_q, ml_w_k, ml_w_v, ml_w_if, ml_b_if,
                  ml_skip, ml_gn, ret_gn, w_out)
    xp, xs = x_prompt, x_sample
    st_p_list, st_s_list = [], []
    for l in range(DEPTH):
        p_l = tuple(w[l] for w in params_all)
        xp, st_p = mixer_layer(xp, c_prompt, zero_state(xp.shape[0]), 0, p_l)
        st_s0 = (state_rg_h[l], state_rg_conv[l], state_ml_C[l], state_ml_n[l], state_ml_m[l],
                 state_ml_conv[l], state_ret_S[l])
        xs, st_s = mixer_layer(xs, c_sample, st_s0, PAST_LEN, p_l)
        st_p_list.append(st_p)
        st_s_list.append(st_s)
    p_rg_h, p_rg_conv, p_ml_C, p_ml_n, p_ml_m, p_ml_conv, p_ret_S = [jnp.stack(a) for a in zip(*st_p_list)]
    s_rg_h, s_rg_conv, s_ml_C, s_ml_n, s_ml_m, s_ml_conv, s_ret_S = [jnp.stack(a) for a in zip(*st_s_list)]
    return (xp, xs, p_rg_h, p_rg_conv, p_ml_C, p_ml_n, p_ml_m, p_ml_conv, p_ret_S,
            s_rg_h, s_rg_conv, s_ml_C, s_ml_n, s_ml_m, s_ml_conv, s_ret_S)
```

```python
import functools
import math

import jax
import jax.numpy as jnp
from jax import lax
from jax.experimental import pallas as pl
from jax.experimental.pallas import tpu as pltpu

D_MODEL = 1024
DEPTH = 2
CONV_W = 4
RG_HEADS = 8
RG_BW = D_MODEL // RG_HEADS
RG_C = 8.0
N_HEADS = 4
DH = D_MODEL // N_HEADS
HALF = DH // 2
PAST_LEN = 16384
ROPE_BASE = 10000.0
EPS = 1e-6
D_IN = 11 * D_MODEL
O_XRG, O_ZRG, O_XML, O_ZML, O_QR, O_KR, O_VR, O_GR, O_GA, O_GB, O_GC = (
    i * D_MODEL for i in range(11))

V_NPRE, V_NPOST, V_RGCW, V_RGCB, V_RGBA, V_RGBX, V_RGLAM = 0, 1, 2, 6, 7, 8, 9
V_MLCW, V_MLCB, V_MLSKIP, V_MLGN, V_RETGN, V_ROWS = 10, 14, 15, 16, 17, 24

T_TILE = 256
VMEM_LIMIT = 60 * 1024 * 1024

_RET_LOG_GAMMA = tuple(math.log1p(-2.0 ** (-5.0 - h)) for h in range(N_HEADS))

_NT = (((1,), (1,)), ((), ()))
_TN = (((0,), (0,)), ((), ()))
_F32 = jnp.float32
_BF16 = jnp.bfloat16


def _dot(a, b):
    return jnp.dot(a, b, preferred_element_type=_F32)


def _sigmoid(x):
    return 1.0 / (1.0 + jnp.exp(-x))


def _silu(x):
    return x * _sigmoid(x)


def _softplus(x):
    return jnp.maximum(x, 0.0) + jnp.log1p(jnp.exp(-jnp.abs(x)))


def _log_sigmoid(x):
    return -_softplus(-x)


def _neg_expm1(y, exp_y):
    p = 1.0 / 720.0
    for c in (1.0 / 120.0, 1.0 / 24.0, 1.0 / 6.0, 0.5, 1.0):
        p = p * y + c
    return jnp.where(y > -0.125, -(p * y), 1.0 - exp_y)


def _rms(x):
    return x * lax.rsqrt(jnp.mean(x * x, axis=-1, keepdims=True) + EPS)


def _headnorm(h):
    mu = jnp.mean(h, axis=-1, keepdims=True)
    c = h - mu
    var = jnp.mean(c * c, axis=-1, keepdims=True)
    return c * lax.rsqrt(var + EPS)


def _lane_scan(x, op, ident):
    n = x.shape[1]
    lane = lax.broadcasted_iota(jnp.int32, x.shape, 1)
    d = 1
    while d < n:
        sh = pltpu.roll(x, d, axis=1)
        x = op(x, jnp.where(lane >= d, sh, ident))
        d *= 2
    return x


def _last_lane(x):
    n = x.shape[1]
    lane = lax.broadcasted_iota(jnp.int32, x.shape, 1)
    return jnp.sum(jnp.where(lane == n - 1, x, 0.0), axis=1, keepdims=True)


def _rg_gates(xc, vec_ref, wax_ref):
    xcb = xc.astype(_BF16)
    sp = _softplus(-vec_ref[V_RGLAM:V_RGLAM + 1, :])
    a_parts, b_parts = [], []
    for h in range(RG_HEADS):
        sl = slice(h * RG_BW, (h + 1) * RG_BW)
        g = _dot(xcb[:, sl], wax_ref[h])
        r = _sigmoid(g[:, :RG_BW] + vec_ref[V_RGBA:V_RGBA + 1, sl])
        i = _sigmoid(g[:, RG_BW:] + vec_ref[V_RGBX:V_RGBX + 1, sl])
        log_a = (-RG_C) * r * sp[:, sl]
        a = jnp.exp(log_a)
        a_parts.append(a)
        b_parts.append(jnp.sqrt(_neg_expm1(2.0 * log_a, a * a)) * (i * xc[:, sl]))
    return jnp.concatenate(a_parts, axis=1), jnp.concatenate(b_parts, axis=1)


def _rope_tables(pos, inv_row):
    ang = pos * inv_row
    return jnp.cos(ang), jnp.sin(ang)


def _rope(x, cos, sin, scale=None):
    out = []
    for h in range(N_HEADS):
        x1 = x[:, h * DH:h * DH + HALF]
        x2 = x[:, h * DH + HALF:(h + 1) * DH]
        r = jnp.concatenate([x1 * cos - x2 * sin, x1 * sin + x2 * cos], axis=1)
        out.append(r if scale is None else r * scale)
    return out


def _ada_kernel(c_ref, w_ref, b_ref, o_ref):
    c = c_ref[...]
    w = w_ref[...]
    c_hi = c.astype(_BF16)
    c_lo = (c - c_hi.astype(_F32)).astype(_BF16)
    w_hi = w.astype(_BF16)
    w_lo = (w - w_hi.astype(_F32)).astype(_BF16)
    o_ref[...] = _dot(c_hi, w_hi) + _dot(c_hi, w_lo) + _dot(c_lo, w_hi) + b_ref[...]


def _ada_call(c_all, w_ada, b_ada):
    rows = c_all.shape[0]
    return pl.pallas_call(
        _ada_kernel,
        grid=(DEPTH, 3),
        in_specs=[
            pl.BlockSpec((rows, D_MODEL), lambda l, n: (0, 0)),
            pl.BlockSpec((None, D_MODEL, D_MODEL), lambda l, n: (l, 0, n)),
            pl.BlockSpec((None, 1, D_MODEL), lambda l, n: (l, 0, n)),
        ],
        out_specs=pl.BlockSpec((None, rows, D_MODEL), lambda l, n: (l, 0, n)),
        out_shape=jax.ShapeDtypeStruct((DEPTH, rows, 3 * D_MODEL), _F32),
        name="adaln",
    )(c_all, w_ada, b_ada.reshape(DEPTH, 1, 3 * D_MODEL))


def _prompt_kernel(x_ref, mod_ref, vec_ref, bif_ref, inv_ref,
                   win_ref, wax_ref, wq_ref, wk_ref, wv_ref, wif_ref, wout_ref,
                   y_ref, o_rgh, o_rgc, o_C, o_n, o_m, o_mlc, o_S,
                   cb_rg, cb_ml, a_s, b_s, h_s, qkv_s, m_s, cos_s, sin_s,
                   dret_s, cross_s, wk_s):
    T = T_TILE
    b = pl.program_id(0)
    j = pl.program_id(1)

    @pl.when(jnp.logical_and(b == 0, j == 0))
    def _init_tables():
        row = lax.broadcasted_iota(jnp.int32, (T, T), 0)
        col = lax.broadcasted_iota(jnp.int32, (T, T), 1)
        rel = (row - col).astype(_F32)
        rowf = lax.broadcasted_iota(jnp.int32, (T, DH), 0).astype(_F32)
        for h in range(N_HEADS):
            lg = _RET_LOG_GAMMA[h]
            dret_s[h] = jnp.where(row >= col, jnp.exp(rel * lg), 0.0)
            cross_s[h] = jnp.exp((rowf + 1.0) * lg)
            wk_s[h] = jnp.exp((T - 1.0 - rowf) * lg)

    @pl.when(b == 0)
    def _init_rope():
        pos = (j * T + lax.broadcasted_iota(jnp.int32, (T, HALF), 0)).astype(_F32)
        c, s = _rope_tables(pos, inv_ref[...])
        cos_s[pl.ds(pl.multiple_of(j * T, T), T), :] = c
        sin_s[pl.ds(pl.multiple_of(j * T, T), T), :] = s

    @pl.when(j == 0)
    def _init_state():
        o_rgh[...] = jnp.zeros_like(o_rgh)
        o_C[...] = jnp.zeros_like(o_C)
        o_n[...] = jnp.zeros_like(o_n)
        o_S[...] = jnp.zeros_like(o_S)
        m_s[...] = jnp.zeros_like(m_s)
        cb_rg[0:8, :] = jnp.zeros((8, D_MODEL), _F32)
        cb_ml[0:8, :] = jnp.zeros((8, D_MODEL), _F32)

    x = x_ref[0]
    shift = mod_ref[0, :, 0:D_MODEL]
    scale = mod_ref[0, :, D_MODEL:2 * D_MODEL]
    gate = mod_ref[0, :, 2 * D_MODEL:3 * D_MODEL]
    u = _rms(x) * (vec_ref[V_NPRE:V_NPRE + 1, :] * (1.0 + scale)) + shift
    ub = u.astype(_BF16)

    def proj(off):
        return _dot(ub, win_ref[:, off:off + D_MODEL])

    def conv(xin, cb, w0, brow, o_state):
        cb[8:8 + T, :] = xin
        acc = vec_ref[brow:brow + 1, :] + vec_ref[w0 + 3:w0 + 4, :] * xin
        for k in range(CONV_W - 1):
            acc = acc + vec_ref[w0 + k:w0 + k + 1, :] * cb[5 + k:5 + k + T, :]
        o_state[0] = cb[T + 5:T + 8, :]
        cb[0:8, :] = cb[T:T + 8, :]
        return acc

    xc = conv(proj(O_XRG), cb_rg, V_RGCW, V_RGCB, o_rgc)
    a, bt = _rg_gates(xc, vec_ref, wax_ref)
    a_s[...] = a
    b_s[...] = bt

    def scan_body(g, h):
        r0 = pl.multiple_of(g * 8, 8)
        a8 = a_s[pl.ds(r0, 8), :]
        b8 = b_s[pl.ds(r0, 8), :]
        rows = []
        for k in range(8):
            h = a8[k:k + 1, :] * h + b8[k:k + 1, :]
            rows.append(h)
        h_s[pl.ds(r0, 8), :] = jnp.concatenate(rows, axis=0)
        return h

    h_last = lax.fori_loop(0, T // 8, scan_body, o_rgh[0])
    o_rgh[0] = h_last
    mix = _sigmoid(proj(O_GA)) * (h_s[...] * _silu(proj(O_ZRG)))

    x_ml = proj(O_XML)
    xc_ml = _silu(conv(x_ml, cb_ml, V_MLCW, V_MLCB, o_mlc))
    xhb = xc_ml.astype(_BF16)
    xmb = x_ml.astype(_BF16)
    for h in range(N_HEADS):
        sl = slice(h * DH, (h + 1) * DH)
        qkv_s[:, h * DH:(h + 1) * DH] = _dot(xhb[:, sl], wq_ref[h]).astype(_BF16)
        qkv_s[:, D_MODEL + h * DH:D_MODEL + (h + 1) * DH] = (
            _dot(xhb[:, sl], wk_ref[h]) * (DH ** -0.5)).astype(_BF16)
        qkv_s[:, 2 * D_MODEL + h * DH:2 * D_MODEL + (h + 1) * DH] = (
            _dot(xmb[:, sl], wv_ref[h]).astype(_BF16))
    gpre = _dot(qkv_s[...], wif_ref[...]) + bif_ref[...]
    g_rows = gpre.T[0:8, :]
    ig = g_rows[0:N_HEADS, :]
    lf = _log_sigmoid(g_rows[N_HEADS:2 * N_HEADS, :])
    m0 = m_s[0:N_HEADS, :]
    F = _lane_scan(lf, jnp.add, 0.0)
    uu = ig - F
    m = F + jnp.maximum(m0, _lane_scan(uu, jnp.maximum, -jnp.inf))
    mL = _last_lane(m)
    FL = _last_lane(F)
    rows_t = jnp.concatenate([
        F - m,
        jnp.exp(F + m0 - m),
        jnp.exp(-m),
        jnp.exp(FL - mL + uu),
        jnp.exp(FL + m0 - mL),
        jnp.zeros((128 - 5 * N_HEADS, T), _F32)], axis=0)
    cols = rows_t.T
    m_s[0:N_HEADS, :] = jnp.broadcast_to(mL, (N_HEADS, T))
    o_m[0] = m_s[:, 0:128]

    row_i = lax.broadcasted_iota(jnp.int32, (T, T), 0)
    col_i = lax.broadcasted_iota(jnp.int32, (T, T), 1)
    causal = row_i >= col_i
    hml_parts = []
    for h in range(N_HEADS):
        qh = qkv_s[:, h * DH:(h + 1) * DH]
        kh = qkv_s[:, D_MODEL + h * DH:D_MODEL + (h + 1) * DH]
        vh = qkv_s[:, 2 * D_MODEL + h * DH:2 * D_MODEL + (h + 1) * DH]
        c_rowpart = cols[:, h:h + 1]
        c_inter = cols[:, 4 + h:5 + h]
        c_floor = cols[:, 8 + h:9 + h]
        c_wl = cols[:, 12 + h:13 + h]
        c_decay = cols[:, 16 + h:17 + h]
        dmat = jnp.where(causal, jnp.exp(c_rowpart + uu[h:h + 1, :]), 0.0)
        s = lax.dot_general(qh, kh, _NT, preferred_element_type=_F32) * dmat
        C0 = o_C[0, h]
        n0 = o_n[0, h:h + 1, :]
        num = c_inter * _dot(qh, C0.astype(_BF16)) + _dot(s.astype(_BF16), vh)
        den = (c_inter * jnp.sum(qh.astype(_F32) * n0, axis=-1, keepdims=True)
               + jnp.sum(s, axis=-1, keepdims=True))
        hml_parts.append(num * (1.0 / jnp.maximum(jnp.abs(den), c_floor)))
        kw = kh.astype(_F32) * c_wl
        o_C[0, h] = c_decay * C0 + lax.dot_general(
            kw.astype(_BF16), vh, _TN, preferred_element_type=_F32)
        o_n[0, h:h + 1, :] = c_decay[0:1, :] * n0 + jnp.sum(kw, axis=0, keepdims=True)
    hn = jnp.concatenate([_headnorm(p) for p in hml_parts], axis=1)
    y_b = (hn * vec_ref[V_MLGN:V_MLGN + 1, :] + vec_ref[V_MLSKIP:V_MLSKIP + 1, :] * xc_ml)
    mix = mix + _sigmoid(proj(O_GB)) * (y_b * _silu(proj(O_ZML)))

    t0 = pl.multiple_of(j * T, T)
    cos = cos_s[pl.ds(t0, T), :]
    sin = sin_s[pl.ds(t0, T), :]
    q_heads = _rope(proj(O_QR), cos, sin)
    k_heads = _rope(proj(O_KR), cos, sin, DH ** -0.5)
    vrb = proj(O_VR).astype(_BF16)
    hret_parts = []
    for h in range(N_HEADS):
        qh = q_heads[h].astype(_BF16)
        kf = k_heads[h]
        vh = vrb[:, h * DH:(h + 1) * DH]
        s = lax.dot_general(qh, kf.astype(_BF16), _NT, preferred_element_type=_F32) * dret_s[h]
        S0 = o_S[0, h]
        hret_parts.append(_dot(s.astype(_BF16), vh) + _dot(qh, S0.astype(_BF16)) * cross_s[h])
        kw = (kf * wk_s[h]).astype(_BF16)
        o_S[0, h] = math.exp(T * _RET_LOG_GAMMA[h]) * S0 + lax.dot_general(
            kw, vh, _TN, preferred_element_type=_F32)
    hn = jnp.concatenate([_headnorm(p) for p in hret_parts], axis=1)
    y_c = hn * vec_ref[V_RETGN:V_RETGN + 1, :] * _silu(proj(O_GR))
    mix = mix + _sigmoid(proj(O_GC)) * y_c

    out = _dot(mix.astype(_BF16), wout_ref[...])
    y_ref[0] = x + gate * (_rms(out) * vec_ref[V_NPOST:V_NPOST + 1, :])


def _prompt_layer(x, mod, vec, bif, inv, win, wax, wq, wk, wv, wif, wout):
    B, L, _ = x.shape
    T = T_TILE
    const2 = lambda b, j: (0, 0)
    const3 = lambda b, j: (0, 0, 0)
    per_b3 = lambda b, j: (b, 0, 0)
    per_b4 = lambda b, j: (b, 0, 0, 0)

    def resident(shape, imap):
        return pl.BlockSpec(shape, imap, pipeline_mode=pl.Buffered(1))

    outs = pl.pallas_call(
        _prompt_kernel,
        grid=(B, L // T),
        in_specs=[
            pl.BlockSpec((1, T, D_MODEL), lambda b, j: (b, j, 0)),
            pl.BlockSpec((1, 1, 3 * D_MODEL), per_b3),
            resident((V_ROWS, D_MODEL), const2),
            resident((1, 128), const2),
            resident((1, HALF), const2),
            resident((D_MODEL, D_IN), const2),
            resident((RG_HEADS, RG_BW, 2 * RG_BW), const3),
            resident((N_HEADS, DH, DH), const3),
            resident((N_HEADS, DH, DH), const3),
            resident((N_HEADS, DH, DH), const3),
            resident((3 * D_MODEL, 128), const2),
            resident((D_MODEL, D_MODEL), const2),
        ],
        out_specs=[
            pl.BlockSpec((1, T, D_MODEL), lambda b, j: (b, j, 0)),
            pl.BlockSpec((1, 1, D_MODEL), per_b3),
            pl.BlockSpec((1, CONV_W - 1, D_MODEL), per_b3),
            pl.BlockSpec((1, N_HEADS, DH, DH), per_b4),
            pl.BlockSpec((1, N_HEADS, DH), per_b3),
            pl.BlockSpec((1, 8, 128), per_b3),
            pl.BlockSpec((1, CONV_W - 1, D_MODEL), per_b3),
            pl.BlockSpec((1, N_HEADS, DH, DH), per_b4),
        ],
        out_shape=[
            jax.ShapeDtypeStruct((B, L, D_MODEL), _F32),
            jax.ShapeDtypeStruct((B, 1, D_MODEL), _F32),
            jax.ShapeDtypeStruct((B, CONV_W - 1, D_MODEL), _F32),
            jax.ShapeDtypeStruct((B, N_HEADS, DH, DH), _F32),
            jax.ShapeDtypeStruct((B, N_HEADS, DH), _F32),
            jax.ShapeDtypeStruct((B, 8, 128), _F32),
            jax.ShapeDtypeStruct((B, CONV_W - 1, D_MODEL), _F32),
            jax.ShapeDtypeStruct((B, N_HEADS, DH, DH), _F32),
        ],
        scratch_shapes=[
            pltpu.VMEM((T + 8, D_MODEL), _F32),
            pltpu.VMEM((T + 8, D_MODEL), _F32),
            pltpu.VMEM((T, D_MODEL), _F32),
            pltpu.VMEM((T, D_MODEL), _F32),
            pltpu.VMEM((T, D_MODEL), _F32),
            pltpu.VMEM((T, 3 * D_MODEL), _BF16),
            pltpu.VMEM((8, T), _F32),
            pltpu.VMEM((L, HALF), _F32),
            pltpu.VMEM((L, HALF), _F32),
            pltpu.VMEM((N_HEADS, T, T), _F32),
            pltpu.VMEM((N_HEADS, T, DH), _F32),
            pltpu.VMEM((N_HEADS, T, DH), _F32),
        ],
        compiler_params=pltpu.CompilerParams(
            dimension_semantics=("arbitrary", "arbitrary"),
            vmem_limit_bytes=VMEM_LIMIT),
        name="prompt_layer",
    )(x, mod, vec, bif, inv, win, wax, wq, wk, wv, wif, wout)
    y, rgh, rgc, mlC, mln, mlm, mlc, retS = outs
    return y, (rgh[:, 0], rgc, mlC, mln, mlm[:, :N_HEADS, 0], mlc, retS)


def _sample_proj_kernel(x_ref, mod_ref, vec_ref, w_ref, o_ref):
    scale = mod_ref[:, D_MODEL:2 * D_MODEL]
    u = _rms(x_ref[...]) * (vec_ref[V_NPRE:V_NPRE + 1, :] * (1.0 + scale)) + mod_ref[:, 0:D_MODEL]
    o_ref[...] = _dot(u.astype(_BF16), w_ref[...])


def _sample_proj(x, mod, vec, win):
    rows = x.shape[0]
    return pl.pallas_call(
        _sample_proj_kernel,
        grid=(D_IN // D_MODEL,),
        in_specs=[
            pl.BlockSpec((rows, D_MODEL), lambda n: (0, 0)),
            pl.BlockSpec((rows, 3 * D_MODEL), lambda n: (0, 0)),
            pl.BlockSpec((V_ROWS, D_MODEL), lambda n: (0, 0)),
            pl.BlockSpec((D_MODEL, D_MODEL), lambda n: (0, n)),
        ],
        out_specs=pl.BlockSpec((rows, D_MODEL), lambda n: (0, n)),
        out_shape=jax.ShapeDtypeStruct((rows, D_IN), _F32),
        name="sample_proj",
    )(x, mod, vec, win)


S_BT = 2


def _bcast_heads(cols):
    rows = cols[0].shape[0]
    return jnp.concatenate([jnp.broadcast_to(c, (rows, DH)) for c in cols], axis=1)


def _sample_mix_kernel(proj_ref, x_ref, mod_ref, vec_ref, bif_ref, inv_ref,
                       wax_ref, wq_ref, wk_ref, wv_ref, wif_ref, wout_ref,
                       rgh_ref, rgc_ref, mlc_ref, mln_ref, mlm_ref, C_ref, S_ref,
                       y_ref, o_rgh, o_rgc, o_mlc, o_mln, o_mlm, o_C, o_S,
                       mix_s, xcml_s, inter_s, dv_s, sv_s, rden_s, svr_s, vr_s, qC_s, qS_s,
                       qT_s, kT_s, qrT_s, krT_s):
    i = pl.program_id(0)
    n_steps = pl.num_programs(0)
    R = x_ref.shape[0]

    def p(off):
        return proj_ref[:, off:off + D_MODEL]

    def conv(xin, cs_ref, o_state, w0, brow):
        cs = cs_ref[...]
        acc = vec_ref[brow:brow + 1, :] + vec_ref[w0 + 3:w0 + 4, :] * xin
        for k in range(CONV_W - 1):
            acc = acc + vec_ref[w0 + k:w0 + k + 1, :] * cs[:, k * D_MODEL:(k + 1) * D_MODEL]
        o_state[...] = jnp.concatenate([cs[:, D_MODEL:], xin], axis=1)
        return acc

    @pl.when(i == 0)
    def _dense():
        xc = conv(p(O_XRG), rgc_ref, o_rgc, V_RGCW, V_RGCB)
        a, bt = _rg_gates(xc, vec_ref, wax_ref)
        h = a * rgh_ref[...] + bt
        o_rgh[...] = h
        mix_s[...] = _sigmoid(p(O_GA)) * (h * _silu(p(O_ZRG)))

        x_ml = p(O_XML)
        xc_ml = _silu(conv(x_ml, mlc_ref, o_mlc, V_MLCW, V_MLCB))
        xcml_s[...] = xc_ml
        xhb = xc_ml.astype(_BF16)
        xmb = x_ml.astype(_BF16)
        qs, ks, vs = [], [], []
        for h_ in range(N_HEADS):
            sl = slice(h_ * DH, (h_ + 1) * DH)
            qs.append(_dot(xhb[:, sl], wq_ref[h_]))
            ks.append(_dot(xhb[:, sl], wk_ref[h_]) * (DH ** -0.5))
            vs.append(_dot(xmb[:, sl], wv_ref[h_]))
        q = jnp.concatenate(qs, axis=1)
        k = jnp.concatenate(ks, axis=1)
        v = jnp.concatenate(vs, axis=1)
        qkvb = jnp.concatenate([q, k, v], axis=1).astype(_BF16)
        ig = _dot(qkvb, wif_ref[0]) + bif_ref[0:1, :]
        lf = _log_sigmoid(_dot(qkvb, wif_ref[1]) + bif_ref[1:2, :])
        m0 = mlm_ref[...]
        m = jnp.maximum(m0 + lf, ig)
        o_mlm[...] = m
        dg = jnp.exp(ig - m)
        inter = jnp.exp(lf + m0 - m)
        floor = jnp.exp(-m)
        n0 = mln_ref[...]
        s_cols, rden_cols = [], []
        for h_ in range(N_HEADS):
            sl = slice(h_ * DH, (h_ + 1) * DH)
            s_h = jnp.sum(q[:, sl] * k[:, sl], axis=-1, keepdims=True) * dg[:, h_:h_ + 1]
            qn = jnp.sum(q[:, sl] * n0[:, sl], axis=-1, keepdims=True)
            den = inter[:, h_:h_ + 1] * qn + s_h
            s_cols.append(s_h)
            rden_cols.append(1.0 / jnp.maximum(jnp.abs(den), floor[:, h_:h_ + 1]))
        dg_b = _bcast_heads([dg[:, h_:h_ + 1] for h_ in range(N_HEADS)])
        inter_b = _bcast_heads([inter[:, h_:h_ + 1] for h_ in range(N_HEADS)])
        inter_s[...] = inter_b
        dv_s[...] = dg_b * v
        sv_s[...] = _bcast_heads(s_cols) * v
        rden_s[...] = _bcast_heads(rden_cols)
        o_mln[...] = inter_b * n0 + dg_b * k
        qT_s[...] = q.T
        kT_s[...] = k.T

        pos = jnp.full((1, HALF), float(PAST_LEN), _F32)
        cos, sin = _rope_tables(pos, inv_ref[...])
        qr = jnp.concatenate(_rope(p(O_QR), cos, sin), axis=1)
        kr = jnp.concatenate(_rope(p(O_KR), cos, sin, DH ** -0.5), axis=1)
        v_r = p(O_VR)
        s_cols = [jnp.sum(qr[:, h_ * DH:(h_ + 1) * DH] * kr[:, h_ * DH:(h_ + 1) * DH],
                          axis=-1, keepdims=True) for h_ in range(N_HEADS)]
        svr_s[...] = _bcast_heads(s_cols) * v_r
        vr_s[...] = v_r
        qrT_s[...] = qr.T
        krT_s[...] = kr.T

    lane = lax.broadcasted_iota(jnp.int32, (DH, R), 1)
    for bi in range(S_BT):
        b = i * S_BT + bi
        onehot = (lane == b).astype(_F32)

        def column(t_ref, h_):
            return jnp.sum(t_ref[h_ * DH:(h_ + 1) * DH, :] * onehot, axis=1, keepdims=True)

        for h_ in range(N_HEADS):
            sl = slice(h_ * DH, (h_ + 1) * DH)
            C0 = C_ref[bi, h_]
            qC_s[pl.ds(b, 1), sl] = jnp.sum(C0 * column(qT_s, h_), axis=0, keepdims=True)
            o_C[bi, h_] = (inter_s[pl.ds(b, 1), sl] * C0
                           + column(kT_s, h_) * dv_s[pl.ds(b, 1), sl])
            S0 = S_ref[bi, h_]
            qS_s[pl.ds(b, 1), sl] = jnp.sum(S0 * column(qrT_s, h_), axis=0, keepdims=True)
            o_S[bi, h_] = (math.exp(_RET_LOG_GAMMA[h_]) * S0
                           + column(krT_s, h_) * vr_s[pl.ds(b, 1), sl])

    @pl.when(i == n_steps - 1)
    def _epilogue():
        h_ml = (inter_s[...] * qC_s[...] + sv_s[...]) * rden_s[...]
        hn = jnp.concatenate(
            [_headnorm(h_ml[:, h_ * DH:(h_ + 1) * DH]) for h_ in range(N_HEADS)], axis=1)
        y_b = hn * vec_ref[V_MLGN:V_MLGN + 1, :] + vec_ref[V_MLSKIP:V_MLSKIP + 1, :] * xcml_s[...]
        mix = mix_s[...] + _sigmoid(p(O_GB)) * (y_b * _silu(p(O_ZML)))
        cross = jnp.concatenate(
            [jnp.full((1, DH), math.exp(_RET_LOG_GAMMA[h_]), _F32) for h_ in range(N_HEADS)], axis=1)
        h_ret = svr_s[...] + qS_s[...] * cross
        hn = jnp.concatenate(
            [_headnorm(h_ret[:, h_ * DH:(h_ + 1) * DH]) for h_ in range(N_HEADS)], axis=1)
        mix = mix + _sigmoid(p(O_GC)) * (hn * vec_ref[V_RETGN:V_RETGN + 1, :] * _silu(p(O_GR)))
        out = _dot(mix.astype(_BF16), wout_ref[...])
        gate = mod_ref[:, 2 * D_MODEL:3 * D_MODEL]
        y_ref[...] = x_ref[...] + gate * (_rms(out) * vec_ref[V_NPOST:V_NPOST + 1, :])


def _sample_mix(proj, x, mod, vec, bif2, inv, wax, wq, wk, wv, wif2, wout,
                rgh, rgc, mlc, mln, mlm, mlC, retS):
    R = x.shape[0]
    c2 = lambda i: (0, 0)
    c3 = lambda i: (0, 0, 0)
    st4 = lambda i: (i, 0, 0, 0)

    def resident(shape, imap):
        return pl.BlockSpec(shape, imap, pipeline_mode=pl.Buffered(1))

    row_blk = pl.BlockSpec((R, D_MODEL), c2)
    conv_blk = pl.BlockSpec((R, (CONV_W - 1) * D_MODEL), c2)
    lane_blk = pl.BlockSpec((R, 128), c2)
    state_blk = pl.BlockSpec((S_BT, N_HEADS, DH, DH), st4)
    f32 = lambda *s: jax.ShapeDtypeStruct(s, _F32)
    outs = pl.pallas_call(
        _sample_mix_kernel,
        grid=(R // S_BT,),
        in_specs=[
            resident((R, D_IN), c2),
            resident((R, D_MODEL), c2),
            resident((R, 3 * D_MODEL), c2),
            resident((V_ROWS, D_MODEL), c2),
            resident((2, 128), c2),
            resident((1, HALF), c2),
            resident((RG_HEADS, RG_BW, 2 * RG_BW), c3),
            resident((N_HEADS, DH, DH), c3),
            resident((N_HEADS, DH, DH), c3),
            resident((N_HEADS, DH, DH), c3),
            resident((2, 3 * D_MODEL, 128), c3),
            resident((D_MODEL, D_MODEL), c2),
            resident((R, D_MODEL), c2),
            resident((R, (CONV_W - 1) * D_MODEL), c2),
            resident((R, (CONV_W - 1) * D_MODEL), c2),
            resident((R, D_MODEL), c2),
            resident((R, 128), c2),
            state_blk,
            state_blk,
        ],
        out_specs=[row_blk, row_blk, conv_blk, conv_blk, row_blk, lane_blk, state_blk, state_blk],
        out_shape=[
            f32(R, D_MODEL), f32(R, D_MODEL), f32(R, (CONV_W - 1) * D_MODEL),
            f32(R, (CONV_W - 1) * D_MODEL), f32(R, D_MODEL), f32(R, 128),
            f32(R, N_HEADS, DH, DH), f32(R, N_HEADS, DH, DH),
        ],
        scratch_shapes=(
            [pltpu.VMEM((R, D_MODEL), _F32) for _ in range(10)]
            + [pltpu.VMEM((D_MODEL, R), _F32) for _ in range(4)]),
        compiler_params=pltpu.CompilerParams(
            dimension_semantics=("arbitrary",), vmem_limit_bytes=VMEM_LIMIT),
        name="sample_mix",
    )(proj, x, mod, vec, bif2, inv, wax, wq, wk, wv, wif2, wout,
      rgh, rgc, mlc, mln, mlm, mlC, retS)
    y, o_rgh, o_rgc, o_mlc, o_mln, o_mlm, o_C, o_S = outs
    return y, (o_rgh, o_rgc.reshape(R, CONV_W - 1, D_MODEL), o_C,
               o_mln.reshape(R, N_HEADS, DH), o_mlm[:, :N_HEADS],
               o_mlc.reshape(R, CONV_W - 1, D_MODEL), o_S)


def _pack_layer_params(l, norm_pre, norm_post, rg_conv_w, rg_conv_b, rg_b_a, rg_b_x, rg_lambda,
                       ml_conv_w, ml_conv_b, ml_skip, ml_gn, ret_gn):
    rows = [norm_pre[l][None], norm_post[l][None], rg_conv_w[l], rg_conv_b[l][None],
            rg_b_a[l][None], rg_b_x[l][None], rg_lambda[l][None], ml_conv_w[l],
            ml_conv_b[l][None], ml_skip[l][None], ml_gn[l][None], ret_gn[l][None]]
    vec = jnp.concatenate(rows, axis=0)
    return jnp.pad(vec, ((0, V_ROWS - vec.shape[0]), (0, 0)))


def kernel(x_prompt, x_sample, c_prompt, c_sample, state_rg_h, state_rg_conv, state_ml_C, state_ml_n, state_ml_m, state_ml_conv, state_ret_S, w_ada, b_ada, norm_pre, norm_post, w_in, rg_conv_w, rg_conv_b, rg_w_a, rg_b_a, rg_w_x, rg_b_x, rg_lambda, ml_conv_w, ml_conv_b, ml_w_q, ml_w_k, ml_w_v, ml_w_if, ml_b_if, ml_skip, ml_gn, ret_gn, w_out):
    B = x_prompt.shape[0]
    mod_all = _ada_call(jnp.concatenate([c_prompt, c_sample], axis=0), w_ada, b_ada)
    inv = (ROPE_BASE ** (-jnp.linspace(0.0, 1.0, HALF, dtype=_F32)))[None, :]
    xp = x_prompt
    xs = x_sample[:, 0, :]
    R = xs.shape[0]
    st_p, st_s = [], []
    lane_pad = ((0, 0), (0, 128 - N_HEADS))
    for l in range(DEPTH):
        vec = _pack_layer_params(l, norm_pre, norm_post, rg_conv_w, rg_conv_b, rg_b_a, rg_b_x,
                                 rg_lambda, ml_conv_w, ml_conv_b, ml_skip, ml_gn, ret_gn)
        bif = jnp.pad(ml_b_if[l][None, :], ((0, 0), (0, 128 - 2 * N_HEADS)))
        win = w_in[l].astype(_BF16)
        wax = jnp.concatenate([rg_w_a[l], rg_w_x[l]], axis=-1).astype(_BF16)
        wq = ml_w_q[l].astype(_BF16)
        wk = ml_w_k[l].astype(_BF16)
        wv = ml_w_v[l].astype(_BF16)
        wif = jnp.pad(ml_w_if[l], ((0, 0), (0, 128 - 2 * N_HEADS))).astype(_BF16)
        wout = w_out[l].astype(_BF16)
        mod_p = mod_all[l, :B][:, None, :]
        xp, sp = _prompt_layer(xp, mod_p, vec, bif, inv, win, wax, wq, wk, wv, wif, wout)
        st_p.append(sp)

        mod_s = mod_all[l, B:]
        wif2 = jnp.stack([jnp.pad(ml_w_if[l][:, :N_HEADS], lane_pad),
                          jnp.pad(ml_w_if[l][:, N_HEADS:], lane_pad)]).astype(_BF16)
        bif2 = jnp.stack([jnp.pad(ml_b_if[l][:N_HEADS], (0, 128 - N_HEADS)),
                          jnp.pad(ml_b_if[l][N_HEADS:], (0, 128 - N_HEADS))])
        proj = _sample_proj(xs, mod_s, vec, win)
        xs, ss = _sample_mix(
            proj, xs, mod_s, vec, bif2, inv, wax, wq, wk, wv, wif2, wout,
            state_rg_h[l], state_rg_conv[l].reshape(R, -1), state_ml_conv[l].reshape(R, -1),
            state_ml_n[l].reshape(R, -1), jnp.pad(state_ml_m[l], lane_pad),
            state_ml_C[l], state_ret_S[l])
        st_s.append(ss)
    p_states = [jnp.stack(a) for a in zip(*st_p)]
    s_states = [jnp.stack(a) for a in zip(*st_s)]
    return (xp, xs[:, None, :]) + tuple(p_states) + tuple(s_states)
```

```python
import functools
import math

import jax
import jax.numpy as jnp
from jax import lax
from jax.experimental import pallas as pl
from jax.experimental.pallas import tpu as pltpu

D_MODEL = 1024
DEPTH = 2
CONV_W = 4
RG_HEADS = 8
RG_BW = D_MODEL // RG_HEADS
RG_C = 8.0
N_HEADS = 4
DH = D_MODEL // N_HEADS
HALF = DH // 2
PAST_LEN = 16384
ROPE_BASE = 10000.0
EPS = 1e-6
D_IN = 11 * D_MODEL
O_XRG, O_ZRG, O_XML, O_ZML, O_QR, O_KR, O_VR, O_GR, O_GA, O_GB, O_GC = (
    i * D_MODEL for i in range(11))

V_NPRE, V_NPOST, V_RGCW, V_RGCB, V_RGBA, V_RGBX, V_RGLAM = 0, 1, 2, 6, 7, 8, 9
V_MLCW, V_MLCB, V_MLSKIP, V_MLGN, V_RETGN, V_ROWS = 10, 14, 15, 16, 17, 24

T_TILE = 256
VMEM_LIMIT = 60 * 1024 * 1024

_RET_LOG_GAMMA = tuple(math.log1p(-2.0 ** (-5.0 - h)) for h in range(N_HEADS))

_NT = (((1,), (1,)), ((), ()))
_TN = (((0,), (0,)), ((), ()))
_F32 = jnp.float32
_BF16 = jnp.bfloat16


def _dot(a, b):
    return jnp.dot(a, b, preferred_element_type=_F32)


def _sigmoid(x):
    return 0.5 * jnp.tanh(0.5 * x) + 0.5


def _silu(x):
    return x * _sigmoid(x)


def _softplus(x):
    return jnp.maximum(x, 0.0) + jnp.log1p(jnp.exp(-jnp.abs(x)))


def _log_sigmoid(x):
    return -_softplus(-x)


def _neg_expm1(y, exp_y):
    p = 1.0 / 720.0
    for c in (1.0 / 120.0, 1.0 / 24.0, 1.0 / 6.0, 0.5, 1.0):
        p = p * y + c
    return jnp.where(y > -0.125, -(p * y), 1.0 - exp_y)


def _rms(x):
    return x * lax.rsqrt(jnp.mean(x * x, axis=-1, keepdims=True) + EPS)


def _headnorm(h):
    mu = jnp.mean(h, axis=-1, keepdims=True)
    c = h - mu
    var = jnp.mean(c * c, axis=-1, keepdims=True)
    return c * lax.rsqrt(var + EPS)


def _lane_scan(x, op, ident):
    n = x.shape[1]
    lane = lax.broadcasted_iota(jnp.int32, x.shape, 1)
    d = 1
    while d < n:
        sh = pltpu.roll(x, d, axis=1)
        x = op(x, jnp.where(lane >= d, sh, ident))
        d *= 2
    return x


def _last_lane(x):
    n = x.shape[1]
    lane = lax.broadcasted_iota(jnp.int32, x.shape, 1)
    return jnp.sum(jnp.where(lane == n - 1, x, 0.0), axis=1, keepdims=True)


def _rg_gates(xc, vec_ref, wax_ref):
    xcb = xc.astype(_BF16)
    sp = _softplus(-vec_ref[V_RGLAM:V_RGLAM + 1, :])
    a_parts, b_parts = [], []
    for h in range(RG_HEADS):
        sl = slice(h * RG_BW, (h + 1) * RG_BW)
        g = _dot(xcb[:, sl], wax_ref[h])
        r = _sigmoid(g[:, :RG_BW] + vec_ref[V_RGBA:V_RGBA + 1, sl])
        i = _sigmoid(g[:, RG_BW:] + vec_ref[V_RGBX:V_RGBX + 1, sl])
        log_a = (-RG_C) * r * sp[:, sl]
        a = jnp.exp(log_a)
        a_parts.append(a)
        b_parts.append(jnp.sqrt(_neg_expm1(2.0 * log_a, a * a)) * (i * xc[:, sl]))
    return jnp.concatenate(a_parts, axis=1), jnp.concatenate(b_parts, axis=1)


def _rope_tables(pos, inv_row):
    ang = pos * inv_row
    return jnp.cos(ang), jnp.sin(ang)


def _rope(x, cos, sin, scale=None):
    out = []
    for h in range(N_HEADS):
        x1 = x[:, h * DH:h * DH + HALF]
        x2 = x[:, h * DH + HALF:(h + 1) * DH]
        r = jnp.concatenate([x1 * cos - x2 * sin, x1 * sin + x2 * cos], axis=1)
        out.append(r if scale is None else r * scale)
    return out


def _ada_kernel(c_ref, w_ref, b_ref, o_ref):
    c = c_ref[...]
    w = w_ref[...]
    c_hi = c.astype(_BF16)
    c_lo = (c - c_hi.astype(_F32)).astype(_BF16)
    w_hi = w.astype(_BF16)
    w_lo = (w - w_hi.astype(_F32)).astype(_BF16)
    o_ref[...] = _dot(c_hi, w_hi) + _dot(c_hi, w_lo) + _dot(c_lo, w_hi) + b_ref[...]


def _ada_call(c_all, w_ada, b_ada):
    rows = c_all.shape[0]
    return pl.pallas_call(
        _ada_kernel,
        grid=(DEPTH, 3),
        in_specs=[
            pl.BlockSpec((rows, D_MODEL), lambda l, n: (0, 0)),
            pl.BlockSpec((None, D_MODEL, D_MODEL), lambda l, n: (l, 0, n)),
            pl.BlockSpec((None, 1, D_MODEL), lambda l, n: (l, 0, n)),
        ],
        out_specs=pl.BlockSpec((None, rows, D_MODEL), lambda l, n: (l, 0, n)),
        out_shape=jax.ShapeDtypeStruct((DEPTH, rows, 3 * D_MODEL), _F32),
        name="adaln",
    )(c_all, w_ada, b_ada.reshape(DEPTH, 1, 3 * D_MODEL))


def _prompt_kernel(x_ref, mod_ref, vec_ref, bif_ref, inv_ref,
                   win_ref, wax_ref, wq_ref, wk_ref, wv_ref, wif_ref, wout_ref, *rest):
    (y_ref, o_rgh, o_rgc, o_C, o_n, o_m, o_mlc, o_S,
     cb_rg, cb_ml, a_s, b_s, h_s, qkv_s, m_s, cos_s, sin_s,
     dret_s, cross_s, wk_s) = rest[-20:]
    T = T_TILE
    b = pl.program_id(0)
    j = pl.program_id(1)

    @pl.when(jnp.logical_and(b == 0, j == 0))
    def _init_tables():
        row = lax.broadcasted_iota(jnp.int32, (T, T), 0)
        col = lax.broadcasted_iota(jnp.int32, (T, T), 1)
        rel = (row - col).astype(_F32)
        rowf = lax.broadcasted_iota(jnp.int32, (T, DH), 0).astype(_F32)
        for h in range(N_HEADS):
            lg = _RET_LOG_GAMMA[h]
            dret_s[h] = jnp.where(row >= col, jnp.exp(rel * lg), 0.0)
            cross_s[h] = jnp.exp((rowf + 1.0) * lg)
            wk_s[h] = jnp.exp((T - 1.0 - rowf) * lg)

    @pl.when(b == 0)
    def _init_rope():
        pos = (j * T + lax.broadcasted_iota(jnp.int32, (T, HALF), 0)).astype(_F32)
        c, s = _rope_tables(pos, inv_ref[...])
        cos_s[pl.ds(pl.multiple_of(j * T, T), T), :] = c
        sin_s[pl.ds(pl.multiple_of(j * T, T), T), :] = s

    @pl.when(j == 0)
    def _init_state():
        o_rgh[...] = jnp.zeros_like(o_rgh)
        o_C[...] = jnp.zeros_like(o_C)
        o_n[...] = jnp.zeros_like(o_n)
        o_S[...] = jnp.zeros_like(o_S)
        m_s[...] = jnp.zeros_like(m_s)
        cb_rg[0:8, :] = jnp.zeros((8, D_MODEL), _F32)
        cb_ml[0:8, :] = jnp.zeros((8, D_MODEL), _F32)

    x = x_ref[0]
    shift = mod_ref[0, :, 0:D_MODEL]
    scale = mod_ref[0, :, D_MODEL:2 * D_MODEL]
    gate = mod_ref[0, :, 2 * D_MODEL:3 * D_MODEL]
    u = _rms(x) * (vec_ref[V_NPRE:V_NPRE + 1, :] * (1.0 + scale)) + shift
    ub = u.astype(_BF16)

    def proj(off):
        return _dot(ub, win_ref[:, off:off + D_MODEL])

    def conv(xin, cb, w0, brow, o_state):
        cb[8:8 + T, :] = xin
        acc = vec_ref[brow:brow + 1, :] + vec_ref[w0 + 3:w0 + 4, :] * xin
        for k in range(CONV_W - 1):
            acc = acc + vec_ref[w0 + k:w0 + k + 1, :] * cb[5 + k:5 + k + T, :]
        o_state[0] = cb[T + 5:T + 8, :]
        cb[0:8, :] = cb[T:T + 8, :]
        return acc

    xc = conv(proj(O_XRG), cb_rg, V_RGCW, V_RGCB, o_rgc)
    a, bt = _rg_gates(xc, vec_ref, wax_ref)
    a_s[...] = a
    b_s[...] = bt

    def scan_body(g, h):
        r0 = pl.multiple_of(g * 8, 8)
        a8 = a_s[pl.ds(r0, 8), :]
        b8 = b_s[pl.ds(r0, 8), :]
        rows = []
        for k in range(8):
            h = a8[k:k + 1, :] * h + b8[k:k + 1, :]
            rows.append(h)
        h_s[pl.ds(r0, 8), :] = jnp.concatenate(rows, axis=0)
        return h

    h_last = lax.fori_loop(0, T // 8, scan_body, o_rgh[0])
    o_rgh[0] = h_last
    mix = _sigmoid(proj(O_GA)) * (h_s[...] * _silu(proj(O_ZRG)))

    x_ml = proj(O_XML)
    xc_ml = _silu(conv(x_ml, cb_ml, V_MLCW, V_MLCB, o_mlc))
    xhb = xc_ml.astype(_BF16)
    xmb = x_ml.astype(_BF16)
    for h in range(N_HEADS):
        sl = slice(h * DH, (h + 1) * DH)
        qkv_s[:, h * DH:(h + 1) * DH] = _dot(xhb[:, sl], wq_ref[h]).astype(_BF16)
        qkv_s[:, D_MODEL + h * DH:D_MODEL + (h + 1) * DH] = (
            _dot(xhb[:, sl], wk_ref[h]) * (DH ** -0.5)).astype(_BF16)
        qkv_s[:, 2 * D_MODEL + h * DH:2 * D_MODEL + (h + 1) * DH] = (
            _dot(xmb[:, sl], wv_ref[h]).astype(_BF16))
    gpre = _dot(qkv_s[...], wif_ref[...]) + bif_ref[...]
    g_rows = gpre.T[0:8, :]
    ig = g_rows[0:N_HEADS, :]
    lf = _log_sigmoid(g_rows[N_HEADS:2 * N_HEADS, :])
    m0 = m_s[0:N_HEADS, :]
    F = _lane_scan(lf, jnp.add, 0.0)
    uu = ig - F
    m = F + jnp.maximum(m0, _lane_scan(uu, jnp.maximum, -jnp.inf))
    mL = _last_lane(m)
    FL = _last_lane(F)
    rows_t = jnp.concatenate([
        F - m,
        jnp.exp(F + m0 - m),
        jnp.exp(-m),
        jnp.exp(FL - mL + uu),
        jnp.exp(FL + m0 - mL),
        jnp.zeros((128 - 5 * N_HEADS, T), _F32)], axis=0)
    cols = rows_t.T
    m_s[0:N_HEADS, :] = jnp.broadcast_to(mL, (N_HEADS, T))
    o_m[0] = m_s[:, 0:128]

    row_i = lax.broadcasted_iota(jnp.int32, (T, T), 0)
    col_i = lax.broadcasted_iota(jnp.int32, (T, T), 1)
    causal = row_i >= col_i
    hml_parts = []
    for h in range(N_HEADS):
        qh = qkv_s[:, h * DH:(h + 1) * DH]
        kh = qkv_s[:, D_MODEL + h * DH:D_MODEL + (h + 1) * DH]
        vh = qkv_s[:, 2 * D_MODEL + h * DH:2 * D_MODEL + (h + 1) * DH]
        c_rowpart = cols[:, h:h + 1]
        c_inter = cols[:, 4 + h:5 + h]
        c_floor = cols[:, 8 + h:9 + h]
        c_wl = cols[:, 12 + h:13 + h]
        c_decay = cols[:, 16 + h:17 + h]
        dmat = jnp.where(causal, jnp.exp(c_rowpart + uu[h:h + 1, :]), 0.0)
        s = lax.dot_general(qh, kh, _NT, preferred_element_type=_F32) * dmat
        C0 = o_C[0, h]
        n0 = o_n[0, h:h + 1, :]
        num = c_inter * _dot(qh, C0.astype(_BF16)) + _dot(s.astype(_BF16), vh)
        den = (c_inter * jnp.sum(qh.astype(_F32) * n0, axis=-1, keepdims=True)
               + jnp.sum(s, axis=-1, keepdims=True))
        hml_parts.append(num * (1.0 / jnp.maximum(jnp.abs(den), c_floor)))
        kw = kh.astype(_F32) * c_wl
        o_C[0, h] = c_decay * C0 + lax.dot_general(
            kw.astype(_BF16), vh, _TN, preferred_element_type=_F32)
        o_n[0, h:h + 1, :] = c_decay[0:1, :] * n0 + jnp.sum(kw, axis=0, keepdims=True)
    hn = jnp.concatenate([_headnorm(p) for p in hml_parts], axis=1)
    y_b = (hn * vec_ref[V_MLGN:V_MLGN + 1, :] + vec_ref[V_MLSKIP:V_MLSKIP + 1, :] * xc_ml)
    mix = mix + _sigmoid(proj(O_GB)) * (y_b * _silu(proj(O_ZML)))

    t0 = pl.multiple_of(j * T, T)
    cos = cos_s[pl.ds(t0, T), :]
    sin = sin_s[pl.ds(t0, T), :]
    q_heads = _rope(proj(O_QR), cos, sin)
    k_heads = _rope(proj(O_KR), cos, sin, DH ** -0.5)
    vrb = proj(O_VR).astype(_BF16)
    hret_parts = []
    for h in range(N_HEADS):
        qh = q_heads[h].astype(_BF16)
        kf = k_heads[h]
        vh = vrb[:, h * DH:(h + 1) * DH]
        s = lax.dot_general(qh, kf.astype(_BF16), _NT, preferred_element_type=_F32) * dret_s[h]
        S0 = o_S[0, h]
        hret_parts.append(_dot(s.astype(_BF16), vh) + _dot(qh, S0.astype(_BF16)) * cross_s[h])
        kw = (kf * wk_s[h]).astype(_BF16)
        o_S[0, h] = math.exp(T * _RET_LOG_GAMMA[h]) * S0 + lax.dot_general(
            kw, vh, _TN, preferred_element_type=_F32)
    hn = jnp.concatenate([_headnorm(p) for p in hret_parts], axis=1)
    y_c = hn * vec_ref[V_RETGN:V_RETGN + 1, :] * _silu(proj(O_GR))
    mix = mix + _sigmoid(proj(O_GC)) * y_c

    out = _dot(mix.astype(_BF16), wout_ref[...])
    y_ref[0] = x + gate * (_rms(out) * vec_ref[V_NPOST:V_NPOST + 1, :])


def _prompt_layer(layer, prev_big, x, mod, vec, bif, inv, win, wax, wq, wk, wv, wif, wout):
    B, L, _ = x.shape
    T = T_TILE
    const2 = lambda b, j: (0, 0)
    const3 = lambda b, j: (0, 0, 0)
    per_b3 = lambda b, j: (b, 0, 0)
    big_blk = pl.BlockSpec((None, 1, N_HEADS, DH, DH), lambda b, j: (layer, b, 0, 0, 0))
    big_shape = jax.ShapeDtypeStruct((DEPTH, B, N_HEADS, DH, DH), _F32)

    def resident(shape, imap):
        return pl.BlockSpec(shape, imap, pipeline_mode=pl.Buffered(1))

    n_in = 12
    extra_specs = [pl.BlockSpec(memory_space=pl.ANY)] * len(prev_big)
    aliases = {n_in: 3, n_in + 1: 7} if prev_big else {}
    outs = pl.pallas_call(
        _prompt_kernel,
        grid=(B, L // T),
        input_output_aliases=aliases,
        in_specs=[
            pl.BlockSpec((1, T, D_MODEL), lambda b, j: (b, j, 0)),
            pl.BlockSpec((1, 1, 3 * D_MODEL), per_b3),
            resident((V_ROWS, D_MODEL), const2),
            resident((1, 128), const2),
            resident((1, HALF), const2),
            resident((D_MODEL, D_IN), const2),
            resident((RG_HEADS, RG_BW, 2 * RG_BW), const3),
            resident((N_HEADS, DH, DH), const3),
            resident((N_HEADS, DH, DH), const3),
            resident((N_HEADS, DH, DH), const3),
            resident((3 * D_MODEL, 128), const2),
            resident((D_MODEL, D_MODEL), const2),
        ] + extra_specs,
        out_specs=[
            pl.BlockSpec((1, T, D_MODEL), lambda b, j: (b, j, 0)),
            pl.BlockSpec((1, 1, D_MODEL), per_b3),
            pl.BlockSpec((1, CONV_W - 1, D_MODEL), per_b3),
            big_blk,
            pl.BlockSpec((1, N_HEADS, DH), per_b3),
            pl.BlockSpec((1, 8, 128), per_b3),
            pl.BlockSpec((1, CONV_W - 1, D_MODEL), per_b3),
            big_blk,
        ],
        out_shape=[
            jax.ShapeDtypeStruct((B, L, D_MODEL), _F32),
            jax.ShapeDtypeStruct((B, 1, D_MODEL), _F32),
            jax.ShapeDtypeStruct((B, CONV_W - 1, D_MODEL), _F32),
            big_shape,
            jax.ShapeDtypeStruct((B, N_HEADS, DH), _F32),
            jax.ShapeDtypeStruct((B, 8, 128), _F32),
            jax.ShapeDtypeStruct((B, CONV_W - 1, D_MODEL), _F32),
            big_shape,
        ],
        scratch_shapes=[
            pltpu.VMEM((T + 8, D_MODEL), _F32),
            pltpu.VMEM((T + 8, D_MODEL), _F32),
            pltpu.VMEM((T, D_MODEL), _F32),
            pltpu.VMEM((T, D_MODEL), _F32),
            pltpu.VMEM((T, D_MODEL), _F32),
            pltpu.VMEM((T, 3 * D_MODEL), _BF16),
            pltpu.VMEM((8, T), _F32),
            pltpu.VMEM((L, HALF), _F32),
            pltpu.VMEM((L, HALF), _F32),
            pltpu.VMEM((N_HEADS, T, T), _F32),
            pltpu.VMEM((N_HEADS, T, DH), _F32),
            pltpu.VMEM((N_HEADS, T, DH), _F32),
        ],
        compiler_params=pltpu.CompilerParams(
            dimension_semantics=("arbitrary", "arbitrary"),
            vmem_limit_bytes=VMEM_LIMIT),
        name="prompt_layer",
    )(x, mod, vec, bif, inv, win, wax, wq, wk, wv, wif, wout, *prev_big)
    y, rgh, rgc, mlC, mln, mlm, mlc, retS = outs
    return y, (rgh[:, 0], rgc, mln, mlm[:, :N_HEADS, 0], mlc), (mlC, retS)


def _sample_proj_kernel(x_ref, mod_ref, vec_ref, w_ref, o_ref):
    scale = mod_ref[:, D_MODEL:2 * D_MODEL]
    u = _rms(x_ref[...]) * (vec_ref[V_NPRE:V_NPRE + 1, :] * (1.0 + scale)) + mod_ref[:, 0:D_MODEL]
    o_ref[...] = _dot(u.astype(_BF16), w_ref[...])


def _sample_proj(x, mod, vec, win):
    rows = x.shape[0]
    return pl.pallas_call(
        _sample_proj_kernel,
        grid=(D_IN // D_MODEL,),
        in_specs=[
            pl.BlockSpec((rows, D_MODEL), lambda n: (0, 0)),
            pl.BlockSpec((rows, 3 * D_MODEL), lambda n: (0, 0)),
            pl.BlockSpec((V_ROWS, D_MODEL), lambda n: (0, 0)),
            pl.BlockSpec((D_MODEL, D_MODEL), lambda n: (0, n)),
        ],
        out_specs=pl.BlockSpec((rows, D_MODEL), lambda n: (0, n)),
        out_shape=jax.ShapeDtypeStruct((rows, D_IN), _F32),
        name="sample_proj",
    )(x, mod, vec, win)


S_BT = 2


def _bcast_heads(cols):
    rows = cols[0].shape[0]
    return jnp.concatenate([jnp.broadcast_to(c, (rows, DH)) for c in cols], axis=1)


def _sample_mix_kernel(proj_ref, x_ref, mod_ref, vec_ref, bif_ref, inv_ref,
                       wax_ref, wq_ref, wk_ref, wv_ref, wif_ref, wout_ref,
                       rgh_ref, rgc_ref, mlc_ref, mln_ref, mlm_ref, C_ref, S_ref, *rest):
    (y_ref, o_rgh, o_rgc, o_mlc, o_mln, o_mlm, o_C, o_S,
     mix_s, xcml_s, inter_s, dv_s, sv_s, rden_s, svr_s, vr_s, qC_s, qS_s,
     qT_s, kT_s, qrT_s, krT_s) = rest[-22:]
    i = pl.program_id(0)
    n_steps = pl.num_programs(0)
    R = x_ref.shape[0]

    def p(off):
        return proj_ref[:, off:off + D_MODEL]

    def conv(xin, cs_ref, o_state, w0, brow):
        cs = cs_ref[...]
        acc = vec_ref[brow:brow + 1, :] + vec_ref[w0 + 3:w0 + 4, :] * xin
        for k in range(CONV_W - 1):
            acc = acc + vec_ref[w0 + k:w0 + k + 1, :] * cs[:, k * D_MODEL:(k + 1) * D_MODEL]
        o_state[...] = jnp.concatenate([cs[:, D_MODEL:], xin], axis=1)
        return acc

    @pl.when(i == 0)
    def _dense():
        xc = conv(p(O_XRG), rgc_ref, o_rgc, V_RGCW, V_RGCB)
        a, bt = _rg_gates(xc, vec_ref, wax_ref)
        h = a * rgh_ref[...] + bt
        o_rgh[...] = h
        mix_s[...] = _sigmoid(p(O_GA)) * (h * _silu(p(O_ZRG)))

        x_ml = p(O_XML)
        xc_ml = _silu(conv(x_ml, mlc_ref, o_mlc, V_MLCW, V_MLCB))
        xcml_s[...] = xc_ml
        xhb = xc_ml.astype(_BF16)
        xmb = x_ml.astype(_BF16)
        qs, ks, vs = [], [], []
        for h_ in range(N_HEADS):
            sl = slice(h_ * DH, (h_ + 1) * DH)
            qs.append(_dot(xhb[:, sl], wq_ref[h_]))
            ks.append(_dot(xhb[:, sl], wk_ref[h_]) * (DH ** -0.5))
            vs.append(_dot(xmb[:, sl], wv_ref[h_]))
        q = jnp.concatenate(qs, axis=1)
        k = jnp.concatenate(ks, axis=1)
        v = jnp.concatenate(vs, axis=1)
        qkvb = jnp.concatenate([q, k, v], axis=1).astype(_BF16)
        ig = _dot(qkvb, wif_ref[0]) + bif_ref[0:1, :]
        lf = _log_sigmoid(_dot(qkvb, wif_ref[1]) + bif_ref[1:2, :])
        m0 = mlm_ref[...]
        m = jnp.maximum(m0 + lf, ig)
        o_mlm[...] = m
        dg = jnp.exp(ig - m)
        inter = jnp.exp(lf + m0 - m)
        floor = jnp.exp(-m)
        n0 = mln_ref[...]
        s_cols, rden_cols = [], []
        for h_ in range(N_HEADS):
            sl = slice(h_ * DH, (h_ + 1) * DH)
            s_h = jnp.sum(q[:, sl] * k[:, sl], axis=-1, keepdims=True) * dg[:, h_:h_ + 1]
            qn = jnp.sum(q[:, sl] * n0[:, sl], axis=-1, keepdims=True)
            den = inter[:, h_:h_ + 1] * qn + s_h
            s_cols.append(s_h)
            rden_cols.append(1.0 / jnp.maximum(jnp.abs(den), floor[:, h_:h_ + 1]))
        dg_b = _bcast_heads([dg[:, h_:h_ + 1] for h_ in range(N_HEADS)])
        inter_b = _bcast_heads([inter[:, h_:h_ + 1] for h_ in range(N_HEADS)])
        inter_s[...] = inter_b
        dv_s[...] = dg_b * v
        sv_s[...] = _bcast_heads(s_cols) * v
        rden_s[...] = _bcast_heads(rden_cols)
        o_mln[...] = inter_b * n0 + dg_b * k
        qT_s[...] = q.T
        kT_s[...] = k.T

        pos = jnp.full((1, HALF), float(PAST_LEN), _F32)
        cos, sin = _rope_tables(pos, inv_ref[...])
        qr = jnp.concatenate(_rope(p(O_QR), cos, sin), axis=1)
        kr = jnp.concatenate(_rope(p(O_KR), cos, sin, DH ** -0.5), axis=1)
        v_r = p(O_VR)
        s_cols = [jnp.sum(qr[:, h_ * DH:(h_ + 1) * DH] * kr[:, h_ * DH:(h_ + 1) * DH],
                          axis=-1, keepdims=True) for h_ in range(N_HEADS)]
        svr_s[...] = _bcast_heads(s_cols) * v_r
        vr_s[...] = v_r
        qrT_s[...] = qr.T
        krT_s[...] = kr.T

    lane = lax.broadcasted_iota(jnp.int32, (DH, R), 1)
    for bi in range(S_BT):
        b = i * S_BT + bi
        onehot = (lane == b).astype(_F32)

        def column(t_ref, h_):
            return jnp.sum(t_ref[h_ * DH:(h_ + 1) * DH, :] * onehot, axis=1, keepdims=True)

        for h_ in range(N_HEADS):
            sl = slice(h_ * DH, (h_ + 1) * DH)
            C0 = C_ref[bi, h_]
            qC_s[pl.ds(b, 1), sl] = jnp.sum(C0 * column(qT_s, h_), axis=0, keepdims=True)
            o_C[bi, h_] = (inter_s[pl.ds(b, 1), sl] * C0
                           + column(kT_s, h_) * dv_s[pl.ds(b, 1), sl])
            S0 = S_ref[bi, h_]
            qS_s[pl.ds(b, 1), sl] = jnp.sum(S0 * column(qrT_s, h_), axis=0, keepdims=True)
            o_S[bi, h_] = (math.exp(_RET_LOG_GAMMA[h_]) * S0
                           + column(krT_s, h_) * vr_s[pl.ds(b, 1), sl])

    @pl.when(i == n_steps - 1)
    def _epilogue():
        h_ml = (inter_s[...] * qC_s[...] + sv_s[...]) * rden_s[...]
        hn = jnp.concatenate(
            [_headnorm(h_ml[:, h_ * DH:(h_ + 1) * DH]) for h_ in range(N_HEADS)], axis=1)
        y_b = hn * vec_ref[V_MLGN:V_MLGN + 1, :] + vec_ref[V_MLSKIP:V_MLSKIP + 1, :] * xcml_s[...]
        mix = mix_s[...] + _sigmoid(p(O_GB)) * (y_b * _silu(p(O_ZML)))
        cross = jnp.concatenate(
            [jnp.full((1, DH), math.exp(_RET_LOG_GAMMA[h_]), _F32) for h_ in range(N_HEADS)], axis=1)
        h_ret = svr_s[...] + qS_s[...] * cross
        hn = jnp.concatenate(
            [_headnorm(h_ret[:, h_ * DH:(h_ + 1) * DH]) for h_ in range(N_HEADS)], axis=1)
        mix = mix + _sigmoid(p(O_GC)) * (hn * vec_ref[V_RETGN:V_RETGN + 1, :] * _silu(p(O_GR)))
        out = _dot(mix.astype(_BF16), wout_ref[...])
        gate = mod_ref[:, 2 * D_MODEL:3 * D_MODEL]
        y_ref[...] = x_ref[...] + gate * (_rms(out) * vec_ref[V_NPOST:V_NPOST + 1, :])


def _sample_mix(layer, prev_big, proj, x, mod, vec, bif2, inv, wax, wq, wk, wv, wif2, wout,
                rgh, rgc, mlc, mln, mlm, mlC_all, retS_all):
    R = x.shape[0]
    c2 = lambda i: (0, 0)
    c3 = lambda i: (0, 0, 0)
    st5 = lambda i: (layer, i, 0, 0, 0)

    def resident(shape, imap):
        return pl.BlockSpec(shape, imap, pipeline_mode=pl.Buffered(1))

    row_blk = pl.BlockSpec((R, D_MODEL), c2)
    conv_blk = pl.BlockSpec((R, (CONV_W - 1) * D_MODEL), c2)
    lane_blk = pl.BlockSpec((R, 128), c2)
    state_blk = pl.BlockSpec((None, S_BT, N_HEADS, DH, DH), st5)
    f32 = lambda *s: jax.ShapeDtypeStruct(s, _F32)
    n_in = 19
    outs = pl.pallas_call(
        _sample_mix_kernel,
        grid=(R // S_BT,),
        input_output_aliases={n_in: 6, n_in + 1: 7} if prev_big else {},
        in_specs=[
            resident((R, D_IN), c2),
            resident((R, D_MODEL), c2),
            resident((R, 3 * D_MODEL), c2),
            resident((V_ROWS, D_MODEL), c2),
            resident((2, 128), c2),
            resident((1, HALF), c2),
            resident((RG_HEADS, RG_BW, 2 * RG_BW), c3),
            resident((N_HEADS, DH, DH), c3),
            resident((N_HEADS, DH, DH), c3),
            resident((N_HEADS, DH, DH), c3),
            resident((2, 3 * D_MODEL, 128), c3),
            resident((D_MODEL, D_MODEL), c2),
            resident((R, D_MODEL), c2),
            resident((R, (CONV_W - 1) * D_MODEL), c2),
            resident((R, (CONV_W - 1) * D_MODEL), c2),
            resident((R, D_MODEL), c2),
            resident((R, 128), c2),
            state_blk,
            state_blk,
        ] + [pl.BlockSpec(memory_space=pl.ANY)] * len(prev_big),
        out_specs=[row_blk, row_blk, conv_blk, conv_blk, row_blk, lane_blk, state_blk, state_blk],
        out_shape=[
            f32(R, D_MODEL), f32(R, D_MODEL), f32(R, (CONV_W - 1) * D_MODEL),
            f32(R, (CONV_W - 1) * D_MODEL), f32(R, D_MODEL), f32(R, 128),
            f32(DEPTH, R, N_HEADS, DH, DH), f32(DEPTH, R, N_HEADS, DH, DH),
        ],
        scratch_shapes=(
            [pltpu.VMEM((R, D_MODEL), _F32) for _ in range(10)]
            + [pltpu.VMEM((D_MODEL, R), _F32) for _ in range(4)]),
        compiler_params=pltpu.CompilerParams(
            dimension_semantics=("arbitrary",), vmem_limit_bytes=VMEM_LIMIT),
        name="sample_mix",
    )(proj, x, mod, vec, bif2, inv, wax, wq, wk, wv, wif2, wout,
      rgh, rgc, mlc, mln, mlm, mlC_all, retS_all, *prev_big)
    y, o_rgh, o_rgc, o_mlc, o_mln, o_mlm, o_C, o_S = outs
    return y, (o_rgh, o_rgc.reshape(R, CONV_W - 1, D_MODEL),
               o_mln.reshape(R, N_HEADS, DH), o_mlm[:, :N_HEADS],
               o_mlc.reshape(R, CONV_W - 1, D_MODEL)), (o_C, o_S)


def _pack_layer_params(l, norm_pre, norm_post, rg_conv_w, rg_conv_b, rg_b_a, rg_b_x, rg_lambda,
                       ml_conv_w, ml_conv_b, ml_skip, ml_gn, ret_gn):
    rows = [norm_pre[l][None], norm_post[l][None], rg_conv_w[l], rg_conv_b[l][None],
            rg_b_a[l][None], rg_b_x[l][None], rg_lambda[l][None], ml_conv_w[l],
            ml_conv_b[l][None], ml_skip[l][None], ml_gn[l][None], ret_gn[l][None]]
    vec = jnp.concatenate(rows, axis=0)
    return jnp.pad(vec, ((0, V_ROWS - vec.shape[0]), (0, 0)))


def kernel(x_prompt, x_sample, c_prompt, c_sample, state_rg_h, state_rg_conv, state_ml_C, state_ml_n, state_ml_m, state_ml_conv, state_ret_S, w_ada, b_ada, norm_pre, norm_post, w_in, rg_conv_w, rg_conv_b, rg_w_a, rg_b_a, rg_w_x, rg_b_x, rg_lambda, ml_conv_w, ml_conv_b, ml_w_q, ml_w_k, ml_w_v, ml_w_if, ml_b_if, ml_skip, ml_gn, ret_gn, w_out):
    B = x_prompt.shape[0]
    mod_all = _ada_call(jnp.concatenate([c_prompt, c_sample], axis=0), w_ada, b_ada)
    inv = (ROPE_BASE ** (-jnp.linspace(0.0, 1.0, HALF, dtype=_F32)))[None, :]
    xp = x_prompt
    xs = x_sample[:, 0, :]
    R = xs.shape[0]
    st_p, st_s = [], []
    big_p, big_s = (), ()
    lane_pad = ((0, 0), (0, 128 - N_HEADS))
    for l in range(DEPTH):
        vec = _pack_layer_params(l, norm_pre, norm_post, rg_conv_w, rg_conv_b, rg_b_a, rg_b_x,
                                 rg_lambda, ml_conv_w, ml_conv_b, ml_skip, ml_gn, ret_gn)
        bif = jnp.pad(ml_b_if[l][None, :], ((0, 0), (0, 128 - 2 * N_HEADS)))
        win = w_in[l].astype(_BF16)
        wax = jnp.concatenate([rg_w_a[l], rg_w_x[l]], axis=-1).astype(_BF16)
        wq = ml_w_q[l].astype(_BF16)
        wk = ml_w_k[l].astype(_BF16)
        wv = ml_w_v[l].astype(_BF16)
        wif = jnp.pad(ml_w_if[l], ((0, 0), (0, 128 - 2 * N_HEADS))).astype(_BF16)
        wout = w_out[l].astype(_BF16)
        mod_p = mod_all[l, :B][:, None, :]
        xp, sp, big_p = _prompt_layer(l, big_p, xp, mod_p, vec, bif, inv, win, wax, wq, wk, wv, wif, wout)
        st_p.append(sp)

        mod_s = mod_all[l, B:]
        wif2 = jnp.stack([jnp.pad(ml_w_if[l][:, :N_HEADS], lane_pad),
                          jnp.pad(ml_w_if[l][:, N_HEADS:], lane_pad)]).astype(_BF16)
        bif2 = jnp.stack([jnp.pad(ml_b_if[l][:N_HEADS], (0, 128 - N_HEADS)),
                          jnp.pad(ml_b_if[l][N_HEADS:], (0, 128 - N_HEADS))])
        proj = _sample_proj(xs, mod_s, vec, win)
        xs, ss, big_s = _sample_mix(
            l, big_s, proj, xs, mod_s, vec, bif2, inv, wax, wq, wk, wv, wif2, wout,
            state_rg_h[l], state_rg_conv[l].reshape(R, -1), state_ml_conv[l].reshape(R, -1),
            state_ml_n[l].reshape(R, -1), jnp.pad(state_ml_m[l], lane_pad),
            state_ml_C, state_ret_S)
        st_s.append(ss)
    p_rgh, p_rgc, p_n, p_m, p_mlc = [jnp.stack(a) for a in zip(*st_p)]
    s_rgh, s_rgc, s_n, s_m, s_mlc = [jnp.stack(a) for a in zip(*st_s)]
    return (xp, xs[:, None, :], p_rgh, p_rgc, big_p[0], p_n, p_m, p_mlc, big_p[1],
            s_rgh, s_rgc, big_s[0], s_n, s_m, s_mlc, big_s[1])
```

```python
import math

import jax
import jax.numpy as jnp
from jax import lax
from jax.experimental import pallas as pl
from jax.experimental.pallas import tpu as pltpu

D_MODEL = 1024
DEPTH = 2
CONV_W = 4
RG_HEADS = 8
RG_BW = D_MODEL // RG_HEADS
RG_C = 8.0
N_HEADS = 4
DH = D_MODEL // N_HEADS
HALF = DH // 2
PAST_LEN = 16384
ROPE_BASE = 10000.0
EPS = 1e-6
N_PROJ = 11
D_IN = N_PROJ * D_MODEL
P_XRG, P_ZRG, P_XML, P_ZML, P_QR, P_KR, P_VR, P_GR, P_GA, P_GB, P_GC = range(N_PROJ)

V_NPRE, V_NPOST, V_RGCW, V_RGCB, V_RGBA, V_RGBX, V_RGLAM = 0, 1, 2, 6, 7, 8, 9
V_MLCW, V_MLCB, V_MLSKIP, V_MLGN, V_RETGN, V_ROWS = 10, 14, 15, 16, 17, 24

SUBLANES = 8
LANES = 128
MXU_N = 256
T_TILE = 256
SEG = T_TILE // SUBLANES
N_GROUPS = T_TILE // SUBLANES
VMEM_LIMIT = 60 * 1024 * 1024

_RET_LOG_GAMMA = tuple(math.log1p(-2.0 ** (-5.0 - h)) for h in range(N_HEADS))

_NT = (((1,), (1,)), ((), ()))
_TN = (((0,), (0,)), ((), ()))
_F32 = jnp.float32
_BF16 = jnp.bfloat16


def _dot(a, b):
    return jnp.dot(a, b, preferred_element_type=_F32)


def _dot_chunks(a, w_ref, first, n):
    return jnp.concatenate([_dot(a, w_ref[first + c]) for c in range(n)], axis=1)


def _sigmoid(x):
    return 0.5 * jnp.tanh(0.5 * x) + 0.5


def _silu(x):
    return x * _sigmoid(x)


def _softplus(x):
    return jnp.maximum(x, 0.0) + jnp.log1p(jnp.exp(-jnp.abs(x)))


def _log_sigmoid(x):
    return -_softplus(-x)


def _neg_expm1(y, exp_y):
    p = 1.0 / 720.0
    for c in (1.0 / 120.0, 1.0 / 24.0, 1.0 / 6.0, 0.5, 1.0):
        p = p * y + c
    return jnp.where(y > -0.125, -(p * y), 1.0 - exp_y)


def _rms(x):
    return x * lax.rsqrt(jnp.mean(x * x, axis=-1, keepdims=True) + EPS)


def _headnorm(h):
    mu = jnp.mean(h, axis=-1, keepdims=True)
    c = h - mu
    var = jnp.mean(c * c, axis=-1, keepdims=True)
    return c * lax.rsqrt(var + EPS)


def _rg_gates(xc, vec_ref, wax_ref):
    xcb = xc.astype(_BF16)
    sp = _softplus(-vec_ref[V_RGLAM:V_RGLAM + 1, :])
    a_parts, b_parts = [], []
    for h in range(RG_HEADS):
        sl = slice(h * RG_BW, (h + 1) * RG_BW)
        g = _dot(xcb[:, sl], wax_ref[h])
        r = _sigmoid(g[:, :RG_BW] + vec_ref[V_RGBA:V_RGBA + 1, sl])
        i = _sigmoid(g[:, RG_BW:] + vec_ref[V_RGBX:V_RGBX + 1, sl])
        log_a = (-RG_C) * r * sp[:, sl]
        a = jnp.exp(log_a)
        a_parts.append(a)
        b_parts.append(jnp.sqrt(_neg_expm1(2.0 * log_a, a * a)) * (i * xc[:, sl]))
    return jnp.concatenate(a_parts, axis=1), jnp.concatenate(b_parts, axis=1)


def _rope_tables(pos, inv_row):
    ang = pos * inv_row
    return jnp.cos(ang), jnp.sin(ang)


def _rope(x, cos, sin, scale=None):
    out = []
    for h in range(N_HEADS):
        x1 = x[:, h * DH:h * DH + HALF]
        x2 = x[:, h * DH + HALF:(h + 1) * DH]
        r = jnp.concatenate([x1 * cos - x2 * sin, x1 * sin + x2 * cos], axis=1)
        out.append(r if scale is None else r * scale)
    return out


def _token_of_row(i):
    return (i & (SUBLANES - 1)) * SEG + (i >> 3)


def _split3_bf16(x):
    p0 = x.astype(_BF16)
    r = x - p0.astype(_F32)
    p1 = r.astype(_BF16)
    p2 = (r - p1.astype(_F32)).astype(_BF16)
    return p0, p1, p2


def _ada_kernel(c_ref, w_ref, b_ref, o_ref):
    c = c_ref[...]
    w = w_ref[...]
    c_hi = c.astype(_BF16)
    c_lo = (c - c_hi.astype(_F32)).astype(_BF16)
    w_hi = w.astype(_BF16)
    w_lo = (w - w_hi.astype(_F32)).astype(_BF16)
    o_ref[...] = _dot(c_hi, w_hi) + _dot(c_hi, w_lo) + _dot(c_lo, w_hi) + b_ref[...]


def _ada_call(c_all, w_ada, b_ada):
    rows = c_all.shape[0]
    return pl.pallas_call(
        _ada_kernel,
        grid=(DEPTH, 3),
        in_specs=[
            pl.BlockSpec((rows, D_MODEL), lambda l, n: (0, 0)),
            pl.BlockSpec((None, D_MODEL, D_MODEL), lambda l, n: (l, 0, n)),
            pl.BlockSpec((None, 1, D_MODEL), lambda l, n: (l, 0, n)),
        ],
        out_specs=pl.BlockSpec((None, rows, D_MODEL), lambda l, n: (l, 0, n)),
        out_shape=jax.ShapeDtypeStruct((DEPTH, rows, 3 * D_MODEL), _F32),
        name="adaln",
    )(c_all, w_ada, b_ada.reshape(DEPTH, 1, 3 * D_MODEL))


def _shifted_conv(xin, carry_ref, vec_ref, w0, brow, o_state):
    T = T_TILE
    tail = (CONV_W - 1) * SUBLANES
    prev = carry_ref[...]
    cur = xin[T - tail:T, :]
    sub = lax.broadcasted_iota(jnp.int32, (SUBLANES, D_MODEL), 0)
    lead = []
    for q in range(CONV_W - 1):
        sl = slice(q * SUBLANES, (q + 1) * SUBLANES)
        lead.append(jnp.where(sub == 0, pltpu.roll(prev[sl], 1, axis=0), pltpu.roll(cur[sl], 1, axis=0)))
    ext = jnp.concatenate(lead + [xin], axis=0)
    acc = vec_ref[brow:brow + 1, :] + vec_ref[w0 + 3:w0 + 4, :] * xin
    for d in range(1, CONV_W):
        acc = acc + vec_ref[w0 + 3 - d:w0 + 4 - d, :] * ext[tail - SUBLANES * d:tail - SUBLANES * d + T, :]
    carry_ref[...] = cur
    o_state[0] = jnp.concatenate(
        [cur[q * SUBLANES + SUBLANES - 1:(q + 1) * SUBLANES, :] for q in range(CONV_W - 1)], axis=0)
    return acc


def _segment_scan(a, bt, h0):
    hs, cums = [], []
    h = cum = None
    for r in range(N_GROUPS):
        sl = slice(r * SUBLANES, (r + 1) * SUBLANES)
        if r == 0:
            h, cum = bt[sl], a[sl]
        else:
            h, cum = a[sl] * h + bt[sl], a[sl] * cum
        hs.append(h)
        cums.append(cum)
    carry = h0
    carries = [carry]
    for s in range(SUBLANES):
        carry = cum[s:s + 1, :] * carry + h[s:s + 1, :]
        carries.append(carry)
    carry_in = jnp.concatenate(carries[:SUBLANES], axis=0)
    out = jnp.concatenate([hs[r] + cums[r] * carry_in for r in range(N_GROUPS)], axis=0)
    return out, carries[SUBLANES]


def _prompt_kernel(x_ref, mod_ref, vec_ref, bif_ref, inv_ref,
                   win_ref, wax_ref, wq_ref, wk_ref, wv_ref, wif_ref, wout_ref, *rest):
    (y_ref, o_rgh, o_rgc, o_C, o_n, o_m, o_mlc, o_S,
     cc_rg, cc_ml, qkv_s, m_s, cos_s, sin_s,
     mask_s, tri_s, dret_s, cross_s, wk_s) = rest[-19:]
    T = T_TILE
    b = pl.program_id(0)
    j = pl.program_id(1)

    @pl.when(jnp.logical_and(b == 0, j == 0))
    def _init_tables():
        tr = _token_of_row(lax.broadcasted_iota(jnp.int32, (T, T), 0))
        tc = _token_of_row(lax.broadcasted_iota(jnp.int32, (T, T), 1))
        causal = tr >= tc
        mask_s[...] = jnp.where(causal, 0.0, -jnp.inf)
        tri_s[...] = jnp.where(causal, 1.0, 0.0).astype(_BF16)
        rel = (tr - tc).astype(_F32)
        tokf = _token_of_row(lax.broadcasted_iota(jnp.int32, (T, DH), 0)).astype(_F32)
        for h in range(N_HEADS):
            lg = _RET_LOG_GAMMA[h]
            dret_s[h] = jnp.where(causal, jnp.exp(rel * lg), 0.0)
            cross_s[h] = jnp.exp((tokf + 1.0) * lg)
            wk_s[h] = jnp.exp((T - 1.0 - tokf) * lg)

    @pl.when(b == 0)
    def _init_rope():
        tok = _token_of_row(lax.broadcasted_iota(jnp.int32, (T, HALF), 0))
        c, s = _rope_tables((j * T + tok).astype(_F32), inv_ref[...])
        cos_s[pl.ds(pl.multiple_of(j * T, T), T), :] = c
        sin_s[pl.ds(pl.multiple_of(j * T, T), T), :] = s

    @pl.when(j == 0)
    def _init_state():
        o_rgh[...] = jnp.zeros_like(o_rgh)
        o_C[...] = jnp.zeros_like(o_C)
        o_n[...] = jnp.zeros_like(o_n)
        o_S[...] = jnp.zeros_like(o_S)
        m_s[...] = jnp.zeros_like(m_s)
        cc_rg[...] = jnp.zeros_like(cc_rg)
        cc_ml[...] = jnp.zeros_like(cc_ml)

    x = x_ref[0]
    shift = mod_ref[0, :, 0:D_MODEL]
    scale = mod_ref[0, :, D_MODEL:2 * D_MODEL]
    gate = mod_ref[0, :, 2 * D_MODEL:3 * D_MODEL]
    u = _rms(x) * (vec_ref[V_NPRE:V_NPRE + 1, :] * (1.0 + scale)) + shift
    ub = u.astype(_BF16)
    n_chunk = D_MODEL // MXU_N

    def proj(blk):
        return _dot_chunks(ub, win_ref, blk * n_chunk, n_chunk)

    x_ml = proj(P_XML)
    xc_ml = _silu(_shifted_conv(x_ml, cc_ml, vec_ref, V_MLCW, V_MLCB, o_mlc))
    xhb = xc_ml.astype(_BF16)
    xmb = x_ml.astype(_BF16)
    for h in range(N_HEADS):
        sl = slice(h * DH, (h + 1) * DH)
        qkv_s[:, h * DH:(h + 1) * DH] = _dot(xhb[:, sl], wq_ref[h]).astype(_BF16)
        qkv_s[:, D_MODEL + h * DH:D_MODEL + (h + 1) * DH] = (
            _dot(xhb[:, sl], wk_ref[h]) * (DH ** -0.5)).astype(_BF16)
        qkv_s[:, 2 * D_MODEL + h * DH:2 * D_MODEL + (h + 1) * DH] = (
            _dot(xmb[:, sl], wv_ref[h]).astype(_BF16))
    ig = _dot(qkv_s[...], wif_ref[0]) + bif_ref[0:1, :]
    lf = _log_sigmoid(_dot(qkv_s[...], wif_ref[1]) + bif_ref[1:2, :])
    tri = tri_s[...]
    F = sum(_dot(tri, part) for part in _split3_bf16(lf))
    uu = ig - F
    uu_rows = uu.T
    mask = mask_s[...]
    lane = lax.broadcasted_iota(jnp.int32, (T, LANES), 1)
    cmax = jnp.zeros((T, LANES), _F32)
    for h in range(N_HEADS):
        cm_h = jnp.max(uu_rows[h:h + 1, :] + mask, axis=1, keepdims=True)
        cmax = jnp.where(lane == h, cm_h, cmax)
    m0 = m_s[0:1, :]
    mrun = jnp.maximum(m0, cmax)
    m = F + mrun
    mL = m[T - 1:T, :]
    FL = F[T - 1:T, :]
    c_row = -mrun
    c_inter = jnp.exp(m0 - mrun)
    c_floor = jnp.exp(-m)
    c_wl = jnp.exp(FL - mL + uu)
    c_decay = jnp.broadcast_to(jnp.exp(FL + m0 - mL), (T, LANES))
    m_s[...] = jnp.broadcast_to(mL, m_s.shape)
    o_m[0] = m_s[...]

    xc = _shifted_conv(proj(P_XRG), cc_rg, vec_ref, V_RGCW, V_RGCB, o_rgc)
    a, bt = _rg_gates(xc, vec_ref, wax_ref)
    h_rg, h_last = _segment_scan(a, bt, o_rgh[0])
    o_rgh[0] = h_last
    mix = _sigmoid(proj(P_GA)) * (h_rg * _silu(proj(P_ZRG)))

    hml_parts = []
    for h in range(N_HEADS):
        qh = qkv_s[:, h * DH:(h + 1) * DH]
        kh = qkv_s[:, D_MODEL + h * DH:D_MODEL + (h + 1) * DH]
        vh = qkv_s[:, 2 * D_MODEL + h * DH:2 * D_MODEL + (h + 1) * DH]
        col = lambda c: c[:, h:h + 1]
        dmat = jnp.exp(col(c_row) + (uu_rows[h:h + 1, :] + mask))
        s = lax.dot_general(qh, kh, _NT, preferred_element_type=_F32) * dmat
        C0 = o_C[0, h]
        n0 = o_n[0, h:h + 1, :]
        num = col(c_inter) * _dot(qh, C0.astype(_BF16)) + _dot(s.astype(_BF16), vh)
        den = (col(c_inter) * jnp.sum(qh.astype(_F32) * n0, axis=-1, keepdims=True)
               + jnp.sum(s, axis=-1, keepdims=True))
        hml_parts.append(num * (1.0 / jnp.maximum(jnp.abs(den), col(c_floor))))
        kw = kh.astype(_F32) * col(c_wl)
        o_C[0, h] = col(c_decay) * C0 + lax.dot_general(
            kw.astype(_BF16), vh, _TN, preferred_element_type=_F32)
        o_n[0, h:h + 1, :] = col(c_decay)[0:1, :] * n0 + jnp.sum(kw, axis=0, keepdims=True)
    hn = jnp.concatenate([_headnorm(p) for p in hml_parts], axis=1)
    y_b = (hn * vec_ref[V_MLGN:V_MLGN + 1, :] + vec_ref[V_MLSKIP:V_MLSKIP + 1, :] * xc_ml)
    mix = mix + _sigmoid(proj(P_GB)) * (y_b * _silu(proj(P_ZML)))

    t0 = pl.multiple_of(j * T, T)
    cos = cos_s[pl.ds(t0, T), :]
    sin = sin_s[pl.ds(t0, T), :]
    q_heads = _rope(proj(P_QR), cos, sin)
    k_heads = _rope(proj(P_KR), cos, sin, DH ** -0.5)
    vrb = proj(P_VR).astype(_BF16)
    hret_parts = []
    for h in range(N_HEADS):
        qh = q_heads[h].astype(_BF16)
        kf = k_heads[h]
        vh = vrb[:, h * DH:(h + 1) * DH]
        s = lax.dot_general(qh, kf.astype(_BF16), _NT, preferred_element_type=_F32) * dret_s[h]
        S0 = o_S[0, h]
        hret_parts.append(_dot(s.astype(_BF16), vh) + _dot(qh, S0.astype(_BF16)) * cross_s[h])
        kw = (kf * wk_s[h]).astype(_BF16)
        o_S[0, h] = math.exp(T * _RET_LOG_GAMMA[h]) * S0 + lax.dot_general(
            kw, vh, _TN, preferred_element_type=_F32)
    hn = jnp.concatenate([_headnorm(p) for p in hret_parts], axis=1)
    y_c = hn * vec_ref[V_RETGN:V_RETGN + 1, :] * _silu(proj(P_GR))
    mix = mix + _sigmoid(proj(P_GC)) * y_c

    out = _dot_chunks(mix.astype(_BF16), wout_ref, 0, n_chunk)
    y_ref[0] = x + gate * (_rms(out) * vec_ref[V_NPOST:V_NPOST + 1, :])


def _prompt_layer(layer, prev_big, x, mod, vec, bif2, inv, win, wax, wq, wk, wv, wif2, wout):
    B, L, _ = x.shape
    T = T_TILE
    const2 = lambda b, j: (0, 0)
    const3 = lambda b, j: (0, 0, 0)
    per_b3 = lambda b, j: (b, 0, 0)
    big_blk = pl.BlockSpec((None, 1, N_HEADS, DH, DH), lambda b, j: (layer, b, 0, 0, 0))
    big_shape = jax.ShapeDtypeStruct((DEPTH, B, N_HEADS, DH, DH), _F32)

    def resident(shape, imap):
        return pl.BlockSpec(shape, imap, pipeline_mode=pl.Buffered(1))

    n_in = 12
    extra_specs = [pl.BlockSpec(memory_space=pl.ANY)] * len(prev_big)
    aliases = {n_in: 3, n_in + 1: 7} if prev_big else {}
    tail = (CONV_W - 1) * SUBLANES
    outs = pl.pallas_call(
        _prompt_kernel,
        grid=(B, L // T),
        input_output_aliases=aliases,
        in_specs=[
            pl.BlockSpec((1, T, D_MODEL), lambda b, j: (b, j, 0)),
            pl.BlockSpec((1, 1, 3 * D_MODEL), per_b3),
            resident((V_ROWS, D_MODEL), const2),
            resident((2, LANES), const2),
            resident((1, HALF), const2),
            resident((D_IN // MXU_N, D_MODEL, MXU_N), const3),
            resident((RG_HEADS, RG_BW, 2 * RG_BW), const3),
            resident((N_HEADS, DH, DH), const3),
            resident((N_HEADS, DH, DH), const3),
            resident((N_HEADS, DH, DH), const3),
            resident((2, 3 * D_MODEL, LANES), const3),
            resident((D_MODEL // MXU_N, D_MODEL, MXU_N), const3),
        ] + extra_specs,
        out_specs=[
            pl.BlockSpec((1, T, D_MODEL), lambda b, j: (b, j, 0)),
            pl.BlockSpec((1, 1, D_MODEL), per_b3),
            pl.BlockSpec((1, CONV_W - 1, D_MODEL), per_b3),
            big_blk,
            pl.BlockSpec((1, N_HEADS, DH), per_b3),
            pl.BlockSpec((1, SUBLANES, LANES), per_b3),
            pl.BlockSpec((1, CONV_W - 1, D_MODEL), per_b3),
            big_blk,
        ],
        out_shape=[
            jax.ShapeDtypeStruct((B, L, D_MODEL), _F32),
            jax.ShapeDtypeStruct((B, 1, D_MODEL), _F32),
            jax.ShapeDtypeStruct((B, CONV_W - 1, D_MODEL), _F32),
            big_shape,
            jax.ShapeDtypeStruct((B, N_HEADS, DH), _F32),
            jax.ShapeDtypeStruct((B, SUBLANES, LANES), _F32),
            jax.ShapeDtypeStruct((B, CONV_W - 1, D_MODEL), _F32),
            big_shape,
        ],
        scratch_shapes=[
            pltpu.VMEM((tail, D_MODEL), _F32),
            pltpu.VMEM((tail, D_MODEL), _F32),
            pltpu.VMEM((T, 3 * D_MODEL), _BF16),
            pltpu.VMEM((SUBLANES, LANES), _F32),
            pltpu.VMEM((L, HALF), _F32),
            pltpu.VMEM((L, HALF), _F32),
            pltpu.VMEM((T, T), _F32),
            pltpu.VMEM((T, T), _BF16),
            pltpu.VMEM((N_HEADS, T, T), _F32),
            pltpu.VMEM((N_HEADS, T, DH), _F32),
            pltpu.VMEM((N_HEADS, T, DH), _F32),
        ],
        compiler_params=pltpu.CompilerParams(
            dimension_semantics=("arbitrary", "arbitrary"),
            vmem_limit_bytes=VMEM_LIMIT),
        name="prompt_layer",
    )(x, mod, vec, bif2, inv, win, wax, wq, wk, wv, wif2, wout, *prev_big)
    y, rgh, rgc, mlC, mln, mlm, mlc, retS = outs
    return y, (rgh[:, 0], rgc, mln, mlm[:, 0, :N_HEADS], mlc), (mlC, retS)


def _interleave_tiles(x):
    B, L, D = x.shape
    return x.reshape(B, L // T_TILE, SUBLANES, SEG, D).transpose(0, 1, 3, 2, 4).reshape(B, L, D)


def _deinterleave_tiles(x):
    B, L, D = x.shape
    return x.reshape(B, L // T_TILE, SEG, SUBLANES, D).transpose(0, 1, 3, 2, 4).reshape(B, L, D)


def _sample_proj_kernel(x_ref, mod_ref, vec_ref, w_ref, o_ref):
    scale = mod_ref[:, D_MODEL:2 * D_MODEL]
    u = _rms(x_ref[...]) * (vec_ref[V_NPRE:V_NPRE + 1, :] * (1.0 + scale)) + mod_ref[:, 0:D_MODEL]
    o_ref[...] = _dot_chunks(u.astype(_BF16), w_ref, 0, D_MODEL // MXU_N)


def _sample_proj(x, mod, vec, win):
    rows = x.shape[0]
    n_chunk = D_MODEL // MXU_N
    return pl.pallas_call(
        _sample_proj_kernel,
        grid=(N_PROJ,),
        in_specs=[
            pl.BlockSpec((rows, D_MODEL), lambda n: (0, 0)),
            pl.BlockSpec((rows, 3 * D_MODEL), lambda n: (0, 0)),
            pl.BlockSpec((V_ROWS, D_MODEL), lambda n: (0, 0)),
            pl.BlockSpec((n_chunk, D_MODEL, MXU_N), lambda n: (n, 0, 0)),
        ],
        out_specs=pl.BlockSpec((rows, D_MODEL), lambda n: (0, n)),
        out_shape=jax.ShapeDtypeStruct((rows, D_IN), _F32),
        name="sample_proj",
    )(x, mod, vec, win)


S_BT = 2


def _bcast_heads(cols):
    rows = cols[0].shape[0]
    return jnp.concatenate([jnp.broadcast_to(c, (rows, DH)) for c in cols], axis=1)


def _sample_mix_kernel(proj_ref, x_ref, mod_ref, vec_ref, bif_ref, inv_ref,
                       wax_ref, wq_ref, wk_ref, wv_ref, wif_ref, wout_ref,
                       rgh_ref, rgc_ref, mlc_ref, mln_ref, mlm_ref, C_ref, S_ref, *rest):
    (y_ref, o_rgh, o_rgc, o_mlc, o_mln, o_mlm, o_C, o_S,
     mix_s, xcml_s, inter_s, dv_s, sv_s, rden_s, svr_s, vr_s, qC_s, qS_s,
     qT_s, kT_s, qrT_s, krT_s) = rest[-22:]
    i = pl.program_id(0)
    n_steps = pl.num_programs(0)
    R = x_ref.shape[0]

    def p(blk):
        return proj_ref[:, blk * D_MODEL:(blk + 1) * D_MODEL]

    def conv(xin, cs_ref, o_state, w0, brow):
        cs = cs_ref[...]
        acc = vec_ref[brow:brow + 1, :] + vec_ref[w0 + 3:w0 + 4, :] * xin
        for k in range(CONV_W - 1):
            acc = acc + vec_ref[w0 + k:w0 + k + 1, :] * cs[:, k * D_MODEL:(k + 1) * D_MODEL]
        o_state[...] = jnp.concatenate([cs[:, D_MODEL:], xin], axis=1)
        return acc

    @pl.when(i == 0)
    def _dense():
        xc = conv(p(P_XRG), rgc_ref, o_rgc, V_RGCW, V_RGCB)
        a, bt = _rg_gates(xc, vec_ref, wax_ref)
        h = a * rgh_ref[...] + bt
        o_rgh[...] = h
        mix_s[...] = _sigmoid(p(P_GA)) * (h * _silu(p(P_ZRG)))

        x_ml = p(P_XML)
        xc_ml = _silu(conv(x_ml, mlc_ref, o_mlc, V_MLCW, V_MLCB))
        xcml_s[...] = xc_ml
        xhb = xc_ml.astype(_BF16)
        xmb = x_ml.astype(_BF16)
        qs, ks, vs = [], [], []
        for h_ in range(N_HEADS):
            sl = slice(h_ * DH, (h_ + 1) * DH)
            qs.append(_dot(xhb[:, sl], wq_ref[h_]))
            ks.append(_dot(xhb[:, sl], wk_ref[h_]) * (DH ** -0.5))
            vs.append(_dot(xmb[:, sl], wv_ref[h_]))
        q = jnp.concatenate(qs, axis=1)
        k = jnp.concatenate(ks, axis=1)
        v = jnp.concatenate(vs, axis=1)
        qkvb = jnp.concatenate([q, k, v], axis=1).astype(_BF16)
        ig = _dot(qkvb, wif_ref[0]) + bif_ref[0:1, :]
        lf = _log_sigmoid(_dot(qkvb, wif_ref[1]) + bif_ref[1:2, :])
        m0 = mlm_ref[...]
        m = jnp.maximum(m0 + lf, ig)
        o_mlm[...] = m
        dg = jnp.exp(ig - m)
        inter = jnp.exp(lf + m0 - m)
        floor = jnp.exp(-m)
        n0 = mln_ref[...]
        s_cols, rden_cols = [], []
        for h_ in range(N_HEADS):
            sl = slice(h_ * DH, (h_ + 1) * DH)
            s_h = jnp.sum(q[:, sl] * k[:, sl], axis=-1, keepdims=True) * dg[:, h_:h_ + 1]
            qn = jnp.sum(q[:, sl] * n0[:, sl], axis=-1, keepdims=True)
            den = inter[:, h_:h_ + 1] * qn + s_h
            s_cols.append(s_h)
            rden_cols.append(1.0 / jnp.maximum(jnp.abs(den), floor[:, h_:h_ + 1]))
        dg_b = _bcast_heads([dg[:, h_:h_ + 1] for h_ in range(N_HEADS)])
        inter_b = _bcast_heads([inter[:, h_:h_ + 1] for h_ in range(N_HEADS)])
        inter_s[...] = inter_b
        dv_s[...] = dg_b * v
        sv_s[...] = _bcast_heads(s_cols) * v
        rden_s[...] = _bcast_heads(rden_cols)
        o_mln[...] = inter_b * n0 + dg_b * k
        qT_s[...] = q.T
        kT_s[...] = k.T

        pos = jnp.full((1, HALF), float(PAST_LEN), _F32)
        cos, sin = _rope_tables(pos, inv_ref[...])
        qr = jnp.concatenate(_rope(p(P_QR), cos, sin), axis=1)
        kr = jnp.concatenate(_rope(p(P_KR), cos, sin, DH ** -0.5), axis=1)
        v_r = p(P_VR)
        s_cols = [jnp.sum(qr[:, h_ * DH:(h_ + 1) * DH] * kr[:, h_ * DH:(h_ + 1) * DH],
                          axis=-1, keepdims=True) for h_ in range(N_HEADS)]
        svr_s[...] = _bcast_heads(s_cols) * v_r
        vr_s[...] = v_r
        qrT_s[...] = qr.T
        krT_s[...] = kr.T

    lane = lax.broadcasted_iota(jnp.int32, (DH, R), 1)
    for bi in range(S_BT):
        b = i * S_BT + bi
        onehot = (lane == b).astype(_F32)

        def column(t_ref, h_):
            return jnp.sum(t_ref[h_ * DH:(h_ + 1) * DH, :] * onehot, axis=1, keepdims=True)

        for h_ in range(N_HEADS):
            sl = slice(h_ * DH, (h_ + 1) * DH)
            C0 = C_ref[bi, h_]
            qC_s[pl.ds(b, 1), sl] = jnp.sum(C0 * column(qT_s, h_), axis=0, keepdims=True)
            o_C[bi, h_] = (inter_s[pl.ds(b, 1), sl] * C0
                           + column(kT_s, h_) * dv_s[pl.ds(b, 1), sl])
            S0 = S_ref[bi, h_]
            qS_s[pl.ds(b, 1), sl] = jnp.sum(S0 * column(qrT_s, h_), axis=0, keepdims=True)
            o_S[bi, h_] = (math.exp(_RET_LOG_GAMMA[h_]) * S0
                           + column(krT_s, h_) * vr_s[pl.ds(b, 1), sl])

    @pl.when(i == n_steps - 1)
    def _epilogue():
        h_ml = (inter_s[...] * qC_s[...] + sv_s[...]) * rden_s[...]
        hn = jnp.concatenate(
            [_headnorm(h_ml[:, h_ * DH:(h_ + 1) * DH]) for h_ in range(N_HEADS)], axis=1)
        y_b = hn * vec_ref[V_MLGN:V_MLGN + 1, :] + vec_ref[V_MLSKIP:V_MLSKIP + 1, :] * xcml_s[...]
        mix = mix_s[...] + _sigmoid(p(P_GB)) * (y_b * _silu(p(P_ZML)))
        cross = jnp.concatenate(
            [jnp.full((1, DH), math.exp(_RET_LOG_GAMMA[h_]), _F32) for h_ in range(N_HEADS)], axis=1)
        h_ret = svr_s[...] + qS_s[...] * cross
        hn = jnp.concatenate(
            [_headnorm(h_ret[:, h_ * DH:(h_ + 1) * DH]) for h_ in range(N_HEADS)], axis=1)
        mix = mix + _sigmoid(p(P_GC)) * (hn * vec_ref[V_RETGN:V_RETGN + 1, :] * _silu(p(P_GR)))
        out = _dot_chunks(mix.astype(_BF16), wout_ref, 0, D_MODEL // MXU_N)
        gate = mod_ref[:, 2 * D_MODEL:3 * D_MODEL]
        y_ref[...] = x_ref[...] + gate * (_rms(out) * vec_ref[V_NPOST:V_NPOST + 1, :])


def _sample_mix(layer, prev_big, proj, x, mod, vec, bif2, inv, wax, wq, wk, wv, wif2, wout,
                rgh, rgc, mlc, mln, mlm, mlC_all, retS_all):
    R = x.shape[0]
    c2 = lambda i: (0, 0)
    c3 = lambda i: (0, 0, 0)
    st5 = lambda i: (layer, i, 0, 0, 0)

    def resident(shape, imap):
        return pl.BlockSpec(shape, imap, pipeline_mode=pl.Buffered(1))

    row_blk = pl.BlockSpec((R, D_MODEL), c2)
    conv_blk = pl.BlockSpec((R, (CONV_W - 1) * D_MODEL), c2)
    lane_blk = pl.BlockSpec((R, LANES), c2)
    state_blk = pl.BlockSpec((None, S_BT, N_HEADS, DH, DH), st5)
    f32 = lambda *s: jax.ShapeDtypeStruct(s, _F32)
    n_in = 19
    outs = pl.pallas_call(
        _sample_mix_kernel,
        grid=(R // S_BT,),
        input_output_aliases={n_in: 6, n_in + 1: 7} if prev_big else {},
        in_specs=[
            resident((R, D_IN), c2),
            resident((R, D_MODEL), c2),
            resident((R, 3 * D_MODEL), c2),
            resident((V_ROWS, D_MODEL), c2),
            resident((2, LANES), c2),
            resident((1, HALF), c2),
            resident((RG_HEADS, RG_BW, 2 * RG_BW), c3),
            resident((N_HEADS, DH, DH), c3),
            resident((N_HEADS, DH, DH), c3),
            resident((N_HEADS, DH, DH), c3),
            resident((2, 3 * D_MODEL, LANES), c3),
            resident((D_MODEL // MXU_N, D_MODEL, MXU_N), c3),
            resident((R, D_MODEL), c2),
            resident((R, (CONV_W - 1) * D_MODEL), c2),
            resident((R, (CONV_W - 1) * D_MODEL), c2),
            resident((R, D_MODEL), c2),
            resident((R, LANES), c2),
            state_blk,
            state_blk,
        ] + [pl.BlockSpec(memory_space=pl.ANY)] * len(prev_big),
        out_specs=[row_blk, row_blk, conv_blk, conv_blk, row_blk, lane_blk, state_blk, state_blk],
        out_shape=[
            f32(R, D_MODEL), f32(R, D_MODEL), f32(R, (CONV_W - 1) * D_MODEL),
            f32(R, (CONV_W - 1) * D_MODEL), f32(R, D_MODEL), f32(R, LANES),
            f32(DEPTH, R, N_HEADS, DH, DH), f32(DEPTH, R, N_HEADS, DH, DH),
        ],
        scratch_shapes=(
            [pltpu.VMEM((R, D_MODEL), _F32) for _ in range(10)]
            + [pltpu.VMEM((D_MODEL, R), _F32) for _ in range(4)]),
        compiler_params=pltpu.CompilerParams(
            dimension_semantics=("arbitrary",), vmem_limit_bytes=VMEM_LIMIT),
        name="sample_mix",
    )(proj, x, mod, vec, bif2, inv, wax, wq, wk, wv, wif2, wout,
      rgh, rgc, mlc, mln, mlm, mlC_all, retS_all, *prev_big)
    y, o_rgh, o_rgc, o_mlc, o_mln, o_mlm, o_C, o_S = outs
    return y, (o_rgh, o_rgc.reshape(R, CONV_W - 1, D_MODEL),
               o_mln.reshape(R, N_HEADS, DH), o_mlm[:, :N_HEADS],
               o_mlc.reshape(R, CONV_W - 1, D_MODEL)), (o_C, o_S)


def _pack_layer_params(l, norm_pre, norm_post, rg_conv_w, rg_conv_b, rg_b_a, rg_b_x, rg_lambda,
                       ml_conv_w, ml_conv_b, ml_skip, ml_gn, ret_gn):
    rows = [norm_pre[l][None], norm_post[l][None], rg_conv_w[l], rg_conv_b[l][None],
            rg_b_a[l][None], rg_b_x[l][None], rg_lambda[l][None], ml_conv_w[l],
            ml_conv_b[l][None], ml_skip[l][None], ml_gn[l][None], ret_gn[l][None]]
    vec = jnp.concatenate(rows, axis=0)
    return jnp.pad(vec, ((0, V_ROWS - vec.shape[0]), (0, 0)))


def _column_chunks(w):
    K, N = w.shape
    return w.astype(_BF16).reshape(K, N // MXU_N, MXU_N).transpose(1, 0, 2)


def kernel(x_prompt, x_sample, c_prompt, c_sample, state_rg_h, state_rg_conv, state_ml_C, state_ml_n, state_ml_m, state_ml_conv, state_ret_S, w_ada, b_ada, norm_pre, norm_post, w_in, rg_conv_w, rg_conv_b, rg_w_a, rg_b_a, rg_w_x, rg_b_x, rg_lambda, ml_conv_w, ml_conv_b, ml_w_q, ml_w_k, ml_w_v, ml_w_if, ml_b_if, ml_skip, ml_gn, ret_gn, w_out):
    B = x_prompt.shape[0]
    mod_all = _ada_call(jnp.concatenate([c_prompt, c_sample], axis=0), w_ada, b_ada)
    inv = (ROPE_BASE ** (-jnp.linspace(0.0, 1.0, HALF, dtype=_F32)))[None, :]
    xp = _interleave_tiles(x_prompt)
    xs = x_sample[:, 0, :]
    R = xs.shape[0]
    st_p, st_s = [], []
    big_p, big_s = (), ()
    lane_pad = ((0, 0), (0, LANES - N_HEADS))
    for l in range(DEPTH):
        vec = _pack_layer_params(l, norm_pre, norm_post, rg_conv_w, rg_conv_b, rg_b_a, rg_b_x,
                                 rg_lambda, ml_conv_w, ml_conv_b, ml_skip, ml_gn, ret_gn)
        win = _column_chunks(w_in[l])
        wout = _column_chunks(w_out[l])
        wax = jnp.concatenate([rg_w_a[l], rg_w_x[l]], axis=-1).astype(_BF16)
        wq = ml_w_q[l].astype(_BF16)
        wk = ml_w_k[l].astype(_BF16)
        wv = ml_w_v[l].astype(_BF16)
        wif2 = jnp.stack([jnp.pad(ml_w_if[l][:, :N_HEADS], lane_pad),
                          jnp.pad(ml_w_if[l][:, N_HEADS:], lane_pad)]).astype(_BF16)
        bif2 = jnp.stack([jnp.pad(ml_b_if[l][:N_HEADS], (0, LANES - N_HEADS)),
                          jnp.pad(ml_b_if[l][N_HEADS:], (0, LANES - N_HEADS))])
        mod_p = mod_all[l, :B][:, None, :]
        xp, sp, big_p = _prompt_layer(l, big_p, xp, mod_p, vec, bif2, inv, win, wax, wq, wk, wv, wif2, wout)
        st_p.append(sp)

        mod_s = mod_all[l, B:]
        proj = _sample_proj(xs, mod_s, vec, win)
        xs, ss, big_s = _sample_mix(
            l, big_s, proj, xs, mod_s, vec, bif2, inv, wax, wq, wk, wv, wif2, wout,
            state_rg_h[l], state_rg_conv[l].reshape(R, -1), state_ml_conv[l].reshape(R, -1),
            state_ml_n[l].reshape(R, -1), jnp.pad(state_ml_m[l], lane_pad),
            state_ml_C, state_ret_S)
        st_s.append(ss)
    p_rgh, p_rgc, p_n, p_m, p_mlc = [jnp.stack(a) for a in zip(*st_p)]
    s_rgh, s_rgc, s_n, s_m, s_mlc = [jnp.stack(a) for a in zip(*st_s)]
    return (_deinterleave_tiles(xp), xs[:, None, :], p_rgh, p_rgc, big_p[0], p_n, p_m, p_mlc, big_p[1],
            s_rgh, s_rgc, big_s[0], s_n, s_m, s_mlc, big_s[1])
```

```python
import math

import jax
import jax.numpy as jnp
from jax import lax
from jax.experimental import pallas as pl
from jax.experimental.pallas import tpu as pltpu

D_MODEL = 1024
DEPTH = 2
CONV_W = 4
RG_HEADS = 8
RG_BW = D_MODEL // RG_HEADS
RG_C = 8.0
N_HEADS = 4
DH = D_MODEL // N_HEADS
HALF = DH // 2
PAST_LEN = 16384
ROPE_BASE = 10000.0
EPS = 1e-6
N_PROJ = 11
D_IN = N_PROJ * D_MODEL
P_XRG, P_ZRG, P_XML, P_ZML, P_QR, P_KR, P_VR, P_GR, P_GA, P_GB, P_GC = range(N_PROJ)

V_NPRE, V_NPOST, V_RGCW, V_RGCB, V_RGBA, V_RGBX, V_RGLAM = 0, 1, 2, 6, 7, 8, 9
V_MLCW, V_MLCB, V_MLSKIP, V_MLGN, V_RETGN, V_ROWS = 10, 14, 15, 16, 17, 24

SUBLANES = 8
LANES = 128
MXU_N = 256
T_TILE = 256
SEG = T_TILE // SUBLANES
N_GROUPS = T_TILE // SUBLANES
VMEM_LIMIT = 60 * 1024 * 1024
PREFETCH_B1 = (1, 2, 2, 2)
PREFETCH_GATES = 3
PREFETCH_A = 5
PREFETCH_B2 = 2
PREFETCH_C = 2

_RET_LOG_GAMMA = tuple(math.log1p(-2.0 ** (-5.0 - h)) for h in range(N_HEADS))

_NT = (((1,), (1,)), ((), ()))
_TN = (((0,), (0,)), ((), ()))
_F32 = jnp.float32
_BF16 = jnp.bfloat16


def _dot(a, b):
    return jnp.dot(a, b, preferred_element_type=_F32)


def _dot_chunks(a, w_ref, first, n):
    return jnp.concatenate([_dot(a, w_ref[first + c]) for c in range(n)], axis=1)


def _sigmoid(x):
    return 0.5 * jnp.tanh(0.5 * x) + 0.5


def _silu(x):
    hx = 0.5 * x
    return hx * jnp.tanh(hx) + hx


def _softplus(x):
    return jnp.maximum(x, 0.0) + jnp.log1p(jnp.exp(-jnp.abs(x)))


def _log_sigmoid(x):
    return -_softplus(-x)


def _neg_expm1_2x(x):
    t = jnp.tanh(x)
    return (-2.0 * t) / (1.0 - t)


def _rms(x):
    return x * lax.rsqrt(jnp.mean(x * x, axis=-1, keepdims=True) + EPS)


def _headnorm(h):
    mu = jnp.mean(h, axis=-1, keepdims=True)
    c = h - mu
    var = jnp.mean(c * c, axis=-1, keepdims=True)
    return c * lax.rsqrt(var + EPS)


def _rg_gates(xc, vec_ref, wax_ref, lane0=0):
    xcb = xc.astype(_BF16)
    a_parts, b_parts = [], []
    for hh in range(xc.shape[1] // RG_BW):
        loc = slice(hh * RG_BW, (hh + 1) * RG_BW)
        sl = slice(lane0 + hh * RG_BW, lane0 + (hh + 1) * RG_BW)
        g = _dot(xcb[:, loc], wax_ref[sl.start // RG_BW])
        r = _sigmoid(g[:, :RG_BW] + vec_ref[V_RGBA:V_RGBA + 1, sl])
        i = _sigmoid(g[:, RG_BW:] + vec_ref[V_RGBX:V_RGBX + 1, sl])
        log_a = (-RG_C) * r * _softplus(-vec_ref[V_RGLAM:V_RGLAM + 1, sl])
        a_parts.append(jnp.exp(log_a))
        b_parts.append(jnp.sqrt(_neg_expm1_2x(log_a)) * (i * xc[:, loc]))
    return jnp.concatenate(a_parts, axis=1), jnp.concatenate(b_parts, axis=1)


def _rope_tables(pos, inv_row):
    ang = pos * inv_row
    return jnp.cos(ang), jnp.sin(ang)


def _rope(x, cos, sin, scale=None):
    out = []
    for h in range(N_HEADS):
        x1 = x[:, h * DH:h * DH + HALF]
        x2 = x[:, h * DH + HALF:(h + 1) * DH]
        r = jnp.concatenate([x1 * cos - x2 * sin, x1 * sin + x2 * cos], axis=1)
        out.append(r if scale is None else r * scale)
    return out


def _token_of_row(i):
    return (i & (SUBLANES - 1)) * SEG + (i >> 3)


def _split3_bf16(x):
    p0 = x.astype(_BF16)
    r = x - p0.astype(_F32)
    p1 = r.astype(_BF16)
    p2 = (r - p1.astype(_F32)).astype(_BF16)
    return p0, p1, p2


def _ada_kernel(c_ref, w_ref, b_ref, o_ref):
    c = c_ref[...]
    w = w_ref[...]
    c_hi = c.astype(_BF16)
    c_lo = (c - c_hi.astype(_F32)).astype(_BF16)
    w_hi = w.astype(_BF16)
    w_lo = (w - w_hi.astype(_F32)).astype(_BF16)
    o_ref[...] = _dot(c_hi, w_hi) + _dot(c_hi, w_lo) + _dot(c_lo, w_hi) + b_ref[...]


def _ada_call(c_all, w_ada, b_ada):
    rows = c_all.shape[0]
    return pl.pallas_call(
        _ada_kernel,
        grid=(DEPTH, 3),
        in_specs=[
            pl.BlockSpec((rows, D_MODEL), lambda l, n: (0, 0)),
            pl.BlockSpec((None, D_MODEL, D_MODEL), lambda l, n: (l, 0, n)),
            pl.BlockSpec((None, 1, D_MODEL), lambda l, n: (l, 0, n)),
        ],
        out_specs=pl.BlockSpec((None, rows, D_MODEL), lambda l, n: (l, 0, n)),
        out_shape=jax.ShapeDtypeStruct((DEPTH, rows, 3 * D_MODEL), _F32),
        name="adaln",
    )(c_all, w_ada, b_ada.reshape(DEPTH, 1, 3 * D_MODEL))


def _shifted_conv(xin, cs, carry_ref, vec_ref, w0, brow, o_state):
    T = T_TILE
    tail = (CONV_W - 1) * SUBLANES
    prev = carry_ref[:, cs]
    cur = xin[T - tail:T, :]
    sub = lax.broadcasted_iota(jnp.int32, (SUBLANES, xin.shape[1]), 0)
    lead = []
    for q in range(CONV_W - 1):
        sl = slice(q * SUBLANES, (q + 1) * SUBLANES)
        lead.append(jnp.where(sub == 0, pltpu.roll(prev[sl], 1, axis=0), pltpu.roll(cur[sl], 1, axis=0)))
    ext = jnp.concatenate(lead + [xin], axis=0)
    acc = vec_ref[brow:brow + 1, cs] + vec_ref[w0 + 3:w0 + 4, cs] * xin
    for d in range(1, CONV_W):
        acc = acc + vec_ref[w0 + 3 - d:w0 + 4 - d, cs] * ext[tail - SUBLANES * d:tail - SUBLANES * d + T, :]
    carry_ref[:, cs] = cur
    o_state[0, :, cs] = jnp.concatenate(
        [cur[q * SUBLANES + SUBLANES - 1:(q + 1) * SUBLANES, :] for q in range(CONV_W - 1)], axis=0)
    return acc


def _segment_scan(a, bt, h0):
    hs, cums = [], []
    h = cum = None
    for r in range(N_GROUPS):
        sl = slice(r * SUBLANES, (r + 1) * SUBLANES)
        if r == 0:
            h, cum = bt[sl], a[sl]
        else:
            h, cum = a[sl] * h + bt[sl], a[sl] * cum
        hs.append(h)
        cums.append(cum)
    carry = h0
    carries = [carry]
    for s in range(SUBLANES):
        carry = cum[s:s + 1, :] * carry + h[s:s + 1, :]
        carries.append(carry)
    carry_in = jnp.concatenate(carries[:SUBLANES], axis=0)
    out = jnp.concatenate([hs[r] + cums[r] * carry_in for r in range(N_GROUPS)], axis=0)
    return out, carries[SUBLANES]


def _prompt_kernel(x_ref, mod_ref, vec_ref, bif_ref, inv_ref,
                   win_ref, wax_ref, wq_ref, wk_ref, wv_ref, wif_ref, wout_ref, *rest):
    (y_ref, o_rgh, o_rgc, o_C, o_n, o_m, o_mlc, o_S,
     cc_rg, cc_ml, qkv_s, xcml_s, mix_s, m_s, cos_s, sin_s,
     mask_s, tri_s, dret_s, cross_s, wk_s) = rest[-21:]
    T = T_TILE
    b = pl.program_id(0)
    j = pl.program_id(1)

    @pl.when(jnp.logical_and(b == 0, j == 0))
    def _init_tables():
        tr = _token_of_row(lax.broadcasted_iota(jnp.int32, (T, T), 0))
        tc = _token_of_row(lax.broadcasted_iota(jnp.int32, (T, T), 1))
        causal = tr >= tc
        mask_s[...] = jnp.where(causal, 0.0, -jnp.inf)
        tri_s[...] = jnp.where(causal, 1.0, 0.0).astype(_BF16)
        rel = (tr - tc).astype(_F32)
        tokf = _token_of_row(lax.broadcasted_iota(jnp.int32, (T, DH), 0)).astype(_F32)
        for h in range(N_HEADS):
            lg = _RET_LOG_GAMMA[h]
            dret_s[h] = jnp.where(causal, jnp.exp(rel * lg), 0.0)
            cross_s[h] = jnp.exp((tokf + 1.0) * lg)
            wk_s[h] = jnp.exp((T - 1.0 - tokf) * lg)

    @pl.when(b == 0)
    def _init_rope():
        tok = _token_of_row(lax.broadcasted_iota(jnp.int32, (T, HALF), 0))
        c, s = _rope_tables((j * T + tok).astype(_F32), inv_ref[...])
        cos_s[pl.ds(pl.multiple_of(j * T, T), T), :] = c
        sin_s[pl.ds(pl.multiple_of(j * T, T), T), :] = s

    @pl.when(j == 0)
    def _init_state():
        o_rgh[...] = jnp.zeros_like(o_rgh)
        o_C[...] = jnp.zeros_like(o_C)
        o_n[...] = jnp.zeros_like(o_n)
        o_S[...] = jnp.zeros_like(o_S)
        m_s[...] = jnp.zeros_like(m_s)
        cc_rg[...] = jnp.zeros_like(cc_rg)
        cc_ml[...] = jnp.zeros_like(cc_ml)

    x = x_ref[0]
    shift = mod_ref[0, :, 0:D_MODEL]
    scale = mod_ref[0, :, D_MODEL:2 * D_MODEL]
    gate = mod_ref[0, :, 2 * D_MODEL:3 * D_MODEL]
    u = _rms(x) * (vec_ref[V_NPRE:V_NPRE + 1, :] * (1.0 + scale)) + shift
    ub = u.astype(_BF16)
    n_chunk = D_MODEL // MXU_N

    pending = [(P_XML, c) for c in range(n_chunk)]
    pending += [(blk, c) for c in range(n_chunk) for blk in (P_XRG, P_ZRG, P_GA)]
    pending += [(blk, c) for c in range(N_HEADS) for blk in (P_ZML, P_GB)]
    pending += [(blk, c) for c in range(N_HEADS) for blk in (P_QR, P_KR, P_VR, P_GR, P_GC)]
    ready = {}

    def prefetch(n):
        for _ in range(min(n, len(pending))):
            blk, c = pending.pop(0)
            ready[(blk, c)] = _dot(ub, win_ref[blk * n_chunk + c])

    def proj(blk, c):
        while (blk, c) not in ready:
            prefetch(1)
        return ready.pop((blk, c))

    prefetch(2)
    for h in range(N_HEADS):
        cs = slice(h * DH, (h + 1) * DH)
        prefetch(PREFETCH_B1[h])
        x_ml = proj(P_XML, h)
        xc_ml = _silu(_shifted_conv(x_ml, cs, cc_ml, vec_ref, V_MLCW, V_MLCB, o_mlc))
        xcml_s[:, cs] = xc_ml
        xhb = xc_ml.astype(_BF16)
        q = _dot(xhb, wq_ref[h]).astype(_BF16)
        k = (_dot(xhb, wk_ref[h]) * (DH ** -0.5)).astype(_BF16)
        v = _dot(x_ml.astype(_BF16), wv_ref[h]).astype(_BF16)
        qkv_s[:, cs] = q
        qkv_s[:, D_MODEL + h * DH:D_MODEL + (h + 1) * DH] = k
        qkv_s[:, 2 * D_MODEL + h * DH:2 * D_MODEL + (h + 1) * DH] = v
    prefetch(PREFETCH_GATES)
    gpre = _dot(qkv_s[...], wif_ref[...]) + bif_ref[...]
    ig = gpre[:, :LANES]
    lf = _log_sigmoid(gpre[:, LANES:])
    tri = tri_s[...]
    F = sum(_dot(tri, part) for part in _split3_bf16(lf))
    uu = ig - F
    uu_rows = uu.T
    mask = mask_s[...]
    lane = lax.broadcasted_iota(jnp.int32, (T, LANES), 1)
    cmax = jnp.zeros((T, LANES), _F32)
    for h in range(N_HEADS):
        cm_h = jnp.max(uu_rows[h:h + 1, :] + mask, axis=1, keepdims=True)
        cmax = jnp.where(lane == h, cm_h, cmax)
    m0 = m_s[0:1, :]
    mrun = jnp.maximum(m0, cmax)
    m = F + mrun
    mL = m[T - 1:T, :]
    FL = F[T - 1:T, :]
    c_row = -mrun
    c_inter = jnp.exp(m0 - mrun)
    c_floor = jnp.exp(-m)
    c_wl = jnp.exp(FL - mL + uu)
    c_decay = jnp.broadcast_to(jnp.exp(FL + m0 - mL), (T, LANES))
    m_s[...] = jnp.broadcast_to(mL, m_s.shape)
    o_m[0] = m_s[...]

    for c in range(n_chunk):
        cs = slice(c * MXU_N, (c + 1) * MXU_N)
        x_rg = proj(P_XRG, c)
        prefetch(1)
        xc = _shifted_conv(x_rg, cs, cc_rg, vec_ref, V_RGCW, V_RGCB, o_rgc)
        prefetch(1)
        a, bt = _rg_gates(xc, vec_ref, wax_ref, c * MXU_N)
        prefetch(1)
        h_rg, h_last = _segment_scan(a, bt, o_rgh[0, :, cs])
        o_rgh[0, :, cs] = h_last
        prefetch(PREFETCH_A - 3)
        mix_s[:, cs] = _sigmoid(proj(P_GA, c)) * (h_rg * _silu(proj(P_ZRG, c)))

    for h in range(N_HEADS):
        cs = slice(h * DH, (h + 1) * DH)
        prefetch(1)
        qh = qkv_s[:, h * DH:(h + 1) * DH]
        kh = qkv_s[:, D_MODEL + h * DH:D_MODEL + (h + 1) * DH]
        vh = qkv_s[:, 2 * D_MODEL + h * DH:2 * D_MODEL + (h + 1) * DH]
        col = lambda c: c[:, h:h + 1]
        dmat = jnp.exp(col(c_row) + (uu_rows[h:h + 1, :] + mask))
        s = lax.dot_general(qh, kh, _NT, preferred_element_type=_F32) * dmat
        C0 = o_C[0, h]
        n0 = o_n[0, h:h + 1, :]
        num = col(c_inter) * _dot(qh, C0.astype(_BF16)) + _dot(s.astype(_BF16), vh)
        den = (col(c_inter) * jnp.sum(qh.astype(_F32) * n0, axis=-1, keepdims=True)
               + jnp.sum(s, axis=-1, keepdims=True))
        h_ml = num * (1.0 / jnp.maximum(jnp.abs(den), col(c_floor)))
        prefetch(PREFETCH_B2 - 1)
        kw = kh.astype(_F32) * col(c_wl)
        o_C[0, h] = col(c_decay) * C0 + lax.dot_general(
            kw.astype(_BF16), vh, _TN, preferred_element_type=_F32)
        o_n[0, h:h + 1, :] = col(c_decay)[0:1, :] * n0 + jnp.sum(kw, axis=0, keepdims=True)
        y_b = (_headnorm(h_ml) * vec_ref[V_MLGN:V_MLGN + 1, cs]
               + vec_ref[V_MLSKIP:V_MLSKIP + 1, cs] * xcml_s[:, cs])
        mix_s[:, cs] += _sigmoid(proj(P_GB, h)) * (y_b * _silu(proj(P_ZML, h)))

    t0 = pl.multiple_of(j * T, T)
    cos = cos_s[pl.ds(t0, T), :]
    sin = sin_s[pl.ds(t0, T), :]

    def rope(xh):
        x1, x2 = xh[:, :HALF], xh[:, HALF:]
        return jnp.concatenate([x1 * cos - x2 * sin, x1 * sin + x2 * cos], axis=1)

    for h in range(N_HEADS):
        cs = slice(h * DH, (h + 1) * DH)
        qh = rope(proj(P_QR, h)).astype(_BF16)
        prefetch(1)
        kf = rope(proj(P_KR, h)) * (DH ** -0.5)
        prefetch(PREFETCH_C - 1)
        vh = proj(P_VR, h).astype(_BF16)
        s = lax.dot_general(qh, kf.astype(_BF16), _NT, preferred_element_type=_F32) * dret_s[h]
        S0 = o_S[0, h]
        h_ret = _dot(s.astype(_BF16), vh) + _dot(qh, S0.astype(_BF16)) * cross_s[h]
        kw = (kf * wk_s[h]).astype(_BF16)
        o_S[0, h] = math.exp(T * _RET_LOG_GAMMA[h]) * S0 + lax.dot_general(
            kw, vh, _TN, preferred_element_type=_F32)
        y_c = _headnorm(h_ret) * vec_ref[V_RETGN:V_RETGN + 1, cs] * _silu(proj(P_GR, h))
        mix_s[:, cs] += _sigmoid(proj(P_GC, h)) * y_c

    out = _dot_chunks(mix_s[...].astype(_BF16), wout_ref, 0, n_chunk)
    y_ref[0] = x + gate * (_rms(out) * vec_ref[V_NPOST:V_NPOST + 1, :])


def _prompt_layer(layer, prev_big, mod_row0, x, mod, vec, bif2, inv, win, wax, wq, wk, wv, wif2, wout):
    B, L, _ = x.shape
    T = T_TILE
    per_b3 = lambda b, j: (b, 0, 0)
    big_blk = pl.BlockSpec((None, 1, N_HEADS, DH, DH), lambda b, j: (layer, b, 0, 0, 0))
    big_shape = jax.ShapeDtypeStruct((DEPTH, B, N_HEADS, DH, DH), _F32)

    def resident(*shape):
        return pl.BlockSpec((None,) + shape, lambda b, j: (layer,) + (0,) * len(shape),
                            pipeline_mode=pl.Buffered(1))

    n_in = 12
    extra_specs = [pl.BlockSpec(memory_space=pl.ANY)] * len(prev_big)
    aliases = {n_in: 3, n_in + 1: 7} if prev_big else {}
    tail = (CONV_W - 1) * SUBLANES
    outs = pl.pallas_call(
        _prompt_kernel,
        grid=(B, L // T),
        input_output_aliases=aliases,
        in_specs=[
            pl.BlockSpec((1, T, D_MODEL), lambda b, j: (b, j, 0)),
            pl.BlockSpec((None, 1, 1, 3 * D_MODEL), lambda b, j: (layer, mod_row0 + b, 0, 0)),
            resident(V_ROWS, D_MODEL),
            resident(1, 2 * LANES),
            pl.BlockSpec((1, HALF), lambda b, j: (0, 0), pipeline_mode=pl.Buffered(1)),
            resident(D_IN // MXU_N, D_MODEL, MXU_N),
            resident(RG_HEADS, RG_BW, 2 * RG_BW),
            resident(N_HEADS, DH, DH),
            resident(N_HEADS, DH, DH),
            resident(N_HEADS, DH, DH),
            resident(3 * D_MODEL, 2 * LANES),
            resident(D_MODEL // MXU_N, D_MODEL, MXU_N),
        ] + extra_specs,
        out_specs=[
            pl.BlockSpec((1, T, D_MODEL), lambda b, j: (b, j, 0)),
            pl.BlockSpec((1, 1, D_MODEL), per_b3),
            pl.BlockSpec((1, CONV_W - 1, D_MODEL), per_b3),
            big_blk,
            pl.BlockSpec((1, N_HEADS, DH), per_b3),
            pl.BlockSpec((1, SUBLANES, LANES), per_b3),
            pl.BlockSpec((1, CONV_W - 1, D_MODEL), per_b3),
            big_blk,
        ],
        out_shape=[
            jax.ShapeDtypeStruct((B, L, D_MODEL), _F32),
            jax.ShapeDtypeStruct((B, 1, D_MODEL), _F32),
            jax.ShapeDtypeStruct((B, CONV_W - 1, D_MODEL), _F32),
            big_shape,
            jax.ShapeDtypeStruct((B, N_HEADS, DH), _F32),
            jax.ShapeDtypeStruct((B, SUBLANES, LANES), _F32),
            jax.ShapeDtypeStruct((B, CONV_W - 1, D_MODEL), _F32),
            big_shape,
        ],
        scratch_shapes=[
            pltpu.VMEM((tail, D_MODEL), _F32),
            pltpu.VMEM((tail, D_MODEL), _F32),
            pltpu.VMEM((T, 3 * D_MODEL), _BF16),
            pltpu.VMEM((T, D_MODEL), _F32),
            pltpu.VMEM((T, D_MODEL), _F32),
            pltpu.VMEM((SUBLANES, LANES), _F32),
            pltpu.VMEM((L, HALF), _F32),
            pltpu.VMEM((L, HALF), _F32),
            pltpu.VMEM((T, T), _F32),
            pltpu.VMEM((T, T), _BF16),
            pltpu.VMEM((N_HEADS, T, T), _F32),
            pltpu.VMEM((N_HEADS, T, DH), _F32),
            pltpu.VMEM((N_HEADS, T, DH), _F32),
        ],
        compiler_params=pltpu.CompilerParams(
            dimension_semantics=("arbitrary", "arbitrary"),
            vmem_limit_bytes=VMEM_LIMIT),
        name="prompt_layer",
    )(x, mod, vec, bif2, inv, win, wax, wq, wk, wv, wif2, wout, *prev_big)
    y, rgh, rgc, mlC, mln, mlm, mlc, retS = outs
    return y, (rgh[:, 0], rgc, mln, mlm[:, 0, :N_HEADS], mlc), (mlC, retS)


def _interleave_tiles(x):
    B, L, D = x.shape
    return x.reshape(B, L // T_TILE, SUBLANES, SEG, D).transpose(0, 1, 3, 2, 4).reshape(B, L, D)


def _deinterleave_tiles(x):
    B, L, D = x.shape
    return x.reshape(B, L // T_TILE, SEG, SUBLANES, D).transpose(0, 1, 3, 2, 4).reshape(B, L, D)


def _sample_proj_kernel(x_ref, mod_ref, vec_ref, w_ref, o_ref):
    scale = mod_ref[:, D_MODEL:2 * D_MODEL]
    u = _rms(x_ref[...]) * (vec_ref[V_NPRE:V_NPRE + 1, :] * (1.0 + scale)) + mod_ref[:, 0:D_MODEL]
    o_ref[...] = _dot_chunks(u.astype(_BF16), w_ref, 0, D_MODEL // MXU_N)


def _sample_proj(layer, x, mod, vec, win):
    rows = x.shape[0]
    n_chunk = D_MODEL // MXU_N
    return pl.pallas_call(
        _sample_proj_kernel,
        grid=(N_PROJ,),
        in_specs=[
            pl.BlockSpec((rows, D_MODEL), lambda n: (0, 0)),
            pl.BlockSpec((None, rows, 3 * D_MODEL), lambda n: (layer, 0, 0)),
            pl.BlockSpec((None, V_ROWS, D_MODEL), lambda n: (layer, 0, 0)),
            pl.BlockSpec((None, n_chunk, D_MODEL, MXU_N), lambda n: (layer, n, 0, 0)),
        ],
        out_specs=pl.BlockSpec((rows, D_MODEL), lambda n: (0, n)),
        out_shape=jax.ShapeDtypeStruct((rows, D_IN), _F32),
        name="sample_proj",
    )(x, mod, vec, win)


S_BT = 2


def _bcast_heads(cols):
    rows = cols[0].shape[0]
    return jnp.concatenate([jnp.broadcast_to(c, (rows, DH)) for c in cols], axis=1)


def _sample_mix_kernel(proj_ref, x_ref, mod_ref, vec_ref, bif_ref, inv_ref,
                       wax_ref, wq_ref, wk_ref, wv_ref, wif_ref, wout_ref,
                       rgh_ref, rgc_ref, mlc_ref, mln_ref, mlm_ref, C_ref, S_ref, *rest):
    (y_ref, o_rgh, o_rgc, o_mlc, o_mln, o_mlm, o_C, o_S,
     mix_s, xcml_s, inter_s, dv_s, sv_s, rden_s, svr_s, vr_s, qC_s, qS_s,
     qT_s, kT_s, qrT_s, krT_s) = rest[-22:]
    i = pl.program_id(0)
    n_steps = pl.num_programs(0)
    R = x_ref.shape[0]

    def p(blk):
        return proj_ref[:, blk * D_MODEL:(blk + 1) * D_MODEL]

    def conv(xin, cs_ref, o_state, w0, brow):
        cs = cs_ref[...]
        acc = vec_ref[brow:brow + 1, :] + vec_ref[w0 + 3:w0 + 4, :] * xin
        for k in range(CONV_W - 1):
            acc = acc + vec_ref[w0 + k:w0 + k + 1, :] * cs[:, k * D_MODEL:(k + 1) * D_MODEL]
        o_state[...] = jnp.concatenate([cs[:, D_MODEL:], xin], axis=1)
        return acc

    @pl.when(i == 0)
    def _dense():
        xc = conv(p(P_XRG), rgc_ref, o_rgc, V_RGCW, V_RGCB)
        a, bt = _rg_gates(xc, vec_ref, wax_ref)
        h = a * rgh_ref[...] + bt
        o_rgh[...] = h
        mix_s[...] = _sigmoid(p(P_GA)) * (h * _silu(p(P_ZRG)))

        x_ml = p(P_XML)
        xc_ml = _silu(conv(x_ml, mlc_ref, o_mlc, V_MLCW, V_MLCB))
        xcml_s[...] = xc_ml
        xhb = xc_ml.astype(_BF16)
        xmb = x_ml.astype(_BF16)
        qs, ks, vs = [], [], []
        for h_ in range(N_HEADS):
            sl = slice(h_ * DH, (h_ + 1) * DH)
            qs.append(_dot(xhb[:, sl], wq_ref[h_]))
            ks.append(_dot(xhb[:, sl], wk_ref[h_]) * (DH ** -0.5))
            vs.append(_dot(xmb[:, sl], wv_ref[h_]))
        q = jnp.concatenate(qs, axis=1)
        k = jnp.concatenate(ks, axis=1)
        v = jnp.concatenate(vs, axis=1)
        qkvb = jnp.concatenate([q, k, v], axis=1).astype(_BF16)
        gpre = _dot(qkvb, wif_ref[...]) + bif_ref[...]
        ig = gpre[:, :LANES]
        lf = _log_sigmoid(gpre[:, LANES:])
        m0 = mlm_ref[...]
        m = jnp.maximum(m0 + lf, ig)
        o_mlm[...] = m
        dg = jnp.exp(ig - m)
        inter = jnp.exp(lf + m0 - m)
        floor = jnp.exp(-m)
        n0 = mln_ref[...]
        s_cols, rden_cols = [], []
        for h_ in range(N_HEADS):
            sl = slice(h_ * DH, (h_ + 1) * DH)
            s_h = jnp.sum(q[:, sl] * k[:, sl], axis=-1, keepdims=True) * dg[:, h_:h_ + 1]
            qn = jnp.sum(q[:, sl] * n0[:, sl], axis=-1, keepdims=True)
            den = inter[:, h_:h_ + 1] * qn + s_h
            s_cols.append(s_h)
            rden_cols.append(1.0 / jnp.maximum(jnp.abs(den), floor[:, h_:h_ + 1]))
        dg_b = _bcast_heads([dg[:, h_:h_ + 1] for h_ in range(N_HEADS)])
        inter_b = _bcast_heads([inter[:, h_:h_ + 1] for h_ in range(N_HEADS)])
        inter_s[...] = inter_b
        dv_s[...] = dg_b * v
        sv_s[...] = _bcast_heads(s_cols) * v
        rden_s[...] = _bcast_heads(rden_cols)
        o_mln[...] = inter_b * n0 + dg_b * k
        qT_s[...] = q.T
        kT_s[...] = k.T

        pos = jnp.full((1, HALF), float(PAST_LEN), _F32)
        cos, sin = _rope_tables(pos, inv_ref[...])
        qr = jnp.concatenate(_rope(p(P_QR), cos, sin), axis=1)
        kr = jnp.concatenate(_rope(p(P_KR), cos, sin, DH ** -0.5), axis=1)
        v_r = p(P_VR)
        s_cols = [jnp.sum(qr[:, h_ * DH:(h_ + 1) * DH] * kr[:, h_ * DH:(h_ + 1) * DH],
                          axis=-1, keepdims=True) for h_ in range(N_HEADS)]
        svr_s[...] = _bcast_heads(s_cols) * v_r
        vr_s[...] = v_r
        qrT_s[...] = qr.T
        krT_s[...] = kr.T

    lane = lax.broadcasted_iota(jnp.int32, (DH, R), 1)
    for bi in range(S_BT):
        b = i * S_BT + bi
        onehot = (lane == b).astype(_F32)

        def column(t_ref, h_):
            return jnp.sum(t_ref[h_ * DH:(h_ + 1) * DH, :] * onehot, axis=1, keepdims=True)

        for h_ in range(N_HEADS):
            sl = slice(h_ * DH, (h_ + 1) * DH)
            C0 = C_ref[bi, h_]
            qC_s[pl.ds(b, 1), sl] = jnp.sum(C0 * column(qT_s, h_), axis=0, keepdims=True)
            o_C[bi, h_] = (inter_s[pl.ds(b, 1), sl] * C0
                           + column(kT_s, h_) * dv_s[pl.ds(b, 1), sl])
            S0 = S_ref[bi, h_]
            qS_s[pl.ds(b, 1), sl] = jnp.sum(S0 * column(qrT_s, h_), axis=0, keepdims=True)
            o_S[bi, h_] = (math.exp(_RET_LOG_GAMMA[h_]) * S0
                           + column(krT_s, h_) * vr_s[pl.ds(b, 1), sl])

    @pl.when(i == n_steps - 1)
    def _epilogue():
        h_ml = (inter_s[...] * qC_s[...] + sv_s[...]) * rden_s[...]
        hn = jnp.concatenate(
            [_headnorm(h_ml[:, h_ * DH:(h_ + 1) * DH]) for h_ in range(N_HEADS)], axis=1)
        y_b = hn * vec_ref[V_MLGN:V_MLGN + 1, :] + vec_ref[V_MLSKIP:V_MLSKIP + 1, :] * xcml_s[...]
        mix = mix_s[...] + _sigmoid(p(P_GB)) * (y_b * _silu(p(P_ZML)))
        cross = jnp.concatenate(
            [jnp.full((1, DH), math.exp(_RET_LOG_GAMMA[h_]), _F32) for h_ in range(N_HEADS)], axis=1)
        h_ret = svr_s[...] + qS_s[...] * cross
        hn = jnp.concatenate(
            [_headnorm(h_ret[:, h_ * DH:(h_ + 1) * DH]) for h_ in range(N_HEADS)], axis=1)
        mix = mix + _sigmoid(p(P_GC)) * (hn * vec_ref[V_RETGN:V_RETGN + 1, :] * _silu(p(P_GR)))
        out = _dot_chunks(mix.astype(_BF16), wout_ref, 0, D_MODEL // MXU_N)
        gate = mod_ref[:, 2 * D_MODEL:3 * D_MODEL]
        y_ref[...] = x_ref[...] + gate * (_rms(out) * vec_ref[V_NPOST:V_NPOST + 1, :])


def _sample_mix(layer, prev_big, proj, x, mod, vec, bif2, inv, wax, wq, wk, wv, wif2, wout,
                rgh, rgc, mlc, mln, mlm, mlC_all, retS_all):
    R = x.shape[0]
    c2 = lambda i: (0, 0)
    st5 = lambda i: (layer, i, 0, 0, 0)

    def resident(*shape):
        return pl.BlockSpec((None,) + shape, lambda i: (layer,) + (0,) * len(shape),
                            pipeline_mode=pl.Buffered(1))

    def resident2(*shape):
        return pl.BlockSpec(shape, c2, pipeline_mode=pl.Buffered(1))

    row_blk = pl.BlockSpec((R, D_MODEL), c2)
    conv_blk = pl.BlockSpec((R, (CONV_W - 1) * D_MODEL), c2)
    lane_blk = pl.BlockSpec((R, LANES), c2)
    state_blk = pl.BlockSpec((None, S_BT, N_HEADS, DH, DH), st5)
    f32 = lambda *s: jax.ShapeDtypeStruct(s, _F32)
    n_in = 19
    outs = pl.pallas_call(
        _sample_mix_kernel,
        grid=(R // S_BT,),
        input_output_aliases={n_in: 6, n_in + 1: 7} if prev_big else {},
        in_specs=[
            resident2(R, D_IN),
            resident2(R, D_MODEL),
            resident(R, 3 * D_MODEL),
            resident(V_ROWS, D_MODEL),
            resident(1, 2 * LANES),
            resident2(1, HALF),
            resident(RG_HEADS, RG_BW, 2 * RG_BW),
            resident(N_HEADS, DH, DH),
            resident(N_HEADS, DH, DH),
            resident(N_HEADS, DH, DH),
            resident(3 * D_MODEL, 2 * LANES),
            resident(D_MODEL // MXU_N, D_MODEL, MXU_N),
            resident(R, D_MODEL),
            resident(R, (CONV_W - 1) * D_MODEL),
            resident(R, (CONV_W - 1) * D_MODEL),
            resident(R, D_MODEL),
            resident(R, LANES),
            state_blk,
            state_blk,
        ] + [pl.BlockSpec(memory_space=pl.ANY)] * len(prev_big),
        out_specs=[row_blk, row_blk, conv_blk, conv_blk, row_blk, lane_blk, state_blk, state_blk],
        out_shape=[
            f32(R, D_MODEL), f32(R, D_MODEL), f32(R, (CONV_W - 1) * D_MODEL),
            f32(R, (CONV_W - 1) * D_MODEL), f32(R, D_MODEL), f32(R, LANES),
            f32(DEPTH, R, N_HEADS, DH, DH), f32(DEPTH, R, N_HEADS, DH, DH),
        ],
        scratch_shapes=(
            [pltpu.VMEM((R, D_MODEL), _F32) for _ in range(10)]
            + [pltpu.VMEM((D_MODEL, R), _F32) for _ in range(4)]),
        compiler_params=pltpu.CompilerParams(
            dimension_semantics=("arbitrary",), vmem_limit_bytes=VMEM_LIMIT),
        name="sample_mix",
    )(proj, x, mod, vec, bif2, inv, wax, wq, wk, wv, wif2, wout,
      rgh, rgc, mlc, mln, mlm, mlC_all, retS_all, *prev_big)
    y, o_rgh, o_rgc, o_mlc, o_mln, o_mlm, o_C, o_S = outs
    return y, (o_rgh, o_rgc.reshape(R, CONV_W - 1, D_MODEL),
               o_mln.reshape(R, N_HEADS, DH), o_mlm[:, :N_HEADS],
               o_mlc.reshape(R, CONV_W - 1, D_MODEL)), (o_C, o_S)


def _pack_params(norm_pre, norm_post, rg_conv_w, rg_conv_b, rg_b_a, rg_b_x, rg_lambda,
                 ml_conv_w, ml_conv_b, ml_skip, ml_gn, ret_gn):
    row = lambda p: p[:, None, :]
    rows = [row(norm_pre), row(norm_post), rg_conv_w, row(rg_conv_b), row(rg_b_a), row(rg_b_x),
            row(rg_lambda), ml_conv_w, row(ml_conv_b), row(ml_skip), row(ml_gn), row(ret_gn)]
    vec = jnp.concatenate(rows, axis=1)
    return jnp.pad(vec, ((0, 0), (0, V_ROWS - vec.shape[1]), (0, 0)))


def _column_chunks(w):
    depth, K, N = w.shape
    return w.astype(_BF16).reshape(depth, K, N // MXU_N, MXU_N).transpose(0, 2, 1, 3)


def kernel(x_prompt, x_sample, c_prompt, c_sample, state_rg_h, state_rg_conv, state_ml_C, state_ml_n, state_ml_m, state_ml_conv, state_ret_S, w_ada, b_ada, norm_pre, norm_post, w_in, rg_conv_w, rg_conv_b, rg_w_a, rg_b_a, rg_w_x, rg_b_x, rg_lambda, ml_conv_w, ml_conv_b, ml_w_q, ml_w_k, ml_w_v, ml_w_if, ml_b_if, ml_skip, ml_gn, ret_gn, w_out):
    xs = x_sample[:, 0, :]
    R = xs.shape[0]
    mod_all = _ada_call(jnp.concatenate([c_sample, c_prompt], axis=0), w_ada, b_ada)
    mod_rows = mod_all.reshape(DEPTH, mod_all.shape[1], 1, 3 * D_MODEL)
    inv = (ROPE_BASE ** (-jnp.linspace(0.0, 1.0, HALF, dtype=_F32)))[None, :]
    vec = _pack_params(norm_pre, norm_post, rg_conv_w, rg_conv_b, rg_b_a, rg_b_x,
                       rg_lambda, ml_conv_w, ml_conv_b, ml_skip, ml_gn, ret_gn)
    win = _column_chunks(w_in)
    wout = _column_chunks(w_out)
    wax = jnp.concatenate([rg_w_a, rg_w_x], axis=-1).astype(_BF16)
    wq = ml_w_q.astype(_BF16)
    wk = ml_w_k.astype(_BF16)
    wv = ml_w_v.astype(_BF16)
    lane_pad = ((0, 0), (0, 0), (0, LANES - N_HEADS))
    wif2 = jnp.concatenate([jnp.pad(ml_w_if[..., :N_HEADS], lane_pad),
                            jnp.pad(ml_w_if[..., N_HEADS:], lane_pad)], axis=-1).astype(_BF16)
    b_if = ml_b_if[:, None, :]
    bif2 = jnp.concatenate([jnp.pad(b_if[..., :N_HEADS], lane_pad),
                            jnp.pad(b_if[..., N_HEADS:], lane_pad)], axis=-1)
    s_rgc_in = state_rg_conv.reshape(DEPTH, R, -1)
    s_mlc_in = state_ml_conv.reshape(DEPTH, R, -1)
    s_mln_in = state_ml_n.reshape(DEPTH, R, -1)
    s_mlm_in = jnp.pad(state_ml_m, lane_pad)

    xp = _interleave_tiles(x_prompt)
    st_p, st_s = [], []
    big_p, big_s = (), ()
    for l in range(DEPTH):
        xp, sp, big_p = _prompt_layer(l, big_p, R, xp, mod_rows, vec, bif2, inv, win, wax, wq, wk, wv, wif2, wout)
        st_p.append(sp)
        proj = _sample_proj(l, xs, mod_all, vec, win)
        xs, ss, big_s = _sample_mix(
            l, big_s, proj, xs, mod_all, vec, bif2, inv, wax, wq, wk, wv, wif2, wout,
            state_rg_h, s_rgc_in, s_mlc_in, s_mln_in, s_mlm_in, state_ml_C, state_ret_S)
        st_s.append(ss)
    p_rgh, p_rgc, p_n, p_m, p_mlc = [jnp.stack(a) for a in zip(*st_p)]
    s_rgh, s_rgc, s_n, s_m, s_mlc = [jnp.stack(a) for a in zip(*st_s)]
    return (_deinterleave_tiles(xp), xs[:, None, :], p_rgh, p_rgc, big_p[0], p_n, p_m, p_mlc, big_p[1],
            s_rgh, s_rgc, big_s[0], s_n, s_m, s_mlc, big_s[1])
```

```python
import math

import jax
import jax.numpy as jnp
from jax import lax
from jax.experimental import pallas as pl
from jax.experimental.pallas import tpu as pltpu

D_MODEL = 1024
DEPTH = 2
CONV_W = 4
RG_HEADS = 8
RG_BW = D_MODEL // RG_HEADS
RG_C = 8.0
N_HEADS = 4
DH = D_MODEL // N_HEADS
HALF = DH // 2
PAST_LEN = 16384
ROPE_BASE = 10000.0
EPS = 1e-6
N_PROJ = 11
D_IN = N_PROJ * D_MODEL
P_XRG, P_ZRG, P_XML, P_ZML, P_QR, P_KR, P_VR, P_GR, P_GA, P_GB, P_GC = range(N_PROJ)

V_NPRE, V_NPOST, V_RGCW, V_RGCB, V_RGBA, V_RGBX, V_RGLAM = 0, 1, 2, 6, 7, 8, 9
V_MLCW, V_MLCB, V_MLSKIP, V_MLGN, V_RETGN, V_ROWS = 10, 14, 15, 16, 17, 24

SUBLANES = 8
LANES = 128
MXU_N = 256
W_CHUNK = 2 * MXU_N
T_TILE = 256
SEG = T_TILE // SUBLANES
N_GROUPS = T_TILE // SUBLANES
VMEM_LIMIT = 60 * 1024 * 1024
PREFETCH_START = 1
PREFETCH_B1 = (1, 1, 1, 1)
PREFETCH_GATES = 1
PREFETCH_A = 2
PREFETCH_B2 = 1
PREFETCH_C = 1

_RET_LOG_GAMMA = tuple(math.log1p(-2.0 ** (-5.0 - h)) for h in range(N_HEADS))

_NT = (((1,), (1,)), ((), ()))
_TN = (((0,), (0,)), ((), ()))
_F32 = jnp.float32
_BF16 = jnp.bfloat16


def _dot(a, b):
    return jnp.dot(a, b, preferred_element_type=_F32)


def _dot_chunks(a, w_ref, first, n):
    return jnp.concatenate([_dot(a, w_ref[first + c]) for c in range(n)], axis=1)


def _sigmoid_of_twice(hx):
    return 0.5 * jnp.tanh(hx) + 0.5


def _silu_of_twice(hx):
    return hx * jnp.tanh(hx) + hx


def _silu(x):
    return _silu_of_twice(0.5 * x)


def _softplus(x):
    return jnp.maximum(x, 0.0) + jnp.log1p(jnp.exp(-jnp.abs(x)))


def _log_sigmoid(x):
    return -_softplus(-x)


def _neg_expm1_2x(x):
    t = jnp.tanh(x)
    return (-2.0 * t) / (1.0 - t)


def _rms(x):
    return x * lax.rsqrt(jnp.mean(x * x, axis=-1, keepdims=True) + EPS)


def _headnorm(h):
    mu = jnp.mean(h, axis=-1, keepdims=True)
    c = h - mu
    var = jnp.mean(c * c, axis=-1, keepdims=True)
    return c * lax.rsqrt(var + EPS)


def _rg_gates(xc, vec_ref, wax_ref, lane0=0):
    xcb = xc.astype(_BF16)
    a_parts, b_parts = [], []
    for hh in range(xc.shape[1] // RG_BW):
        loc = slice(hh * RG_BW, (hh + 1) * RG_BW)
        sl = slice(lane0 + hh * RG_BW, lane0 + (hh + 1) * RG_BW)
        g = _dot(xcb[:, loc], wax_ref[sl.start // RG_BW])
        r = _sigmoid_of_twice(g[:, :RG_BW] + vec_ref[V_RGBA:V_RGBA + 1, sl])
        i = _sigmoid_of_twice(g[:, RG_BW:] + vec_ref[V_RGBX:V_RGBX + 1, sl])
        log_a = (-RG_C) * r * _softplus(-vec_ref[V_RGLAM:V_RGLAM + 1, sl])
        a_parts.append(jnp.exp(log_a))
        b_parts.append(jnp.sqrt(_neg_expm1_2x(log_a)) * (i * xc[:, loc]))
    return jnp.concatenate(a_parts, axis=1), jnp.concatenate(b_parts, axis=1)


def _rope_tables(pos, inv_row):
    ang = pos * inv_row
    return jnp.cos(ang), jnp.sin(ang)


def _rope(x, cos, sin, scale=None):
    out = []
    for h in range(N_HEADS):
        x1 = x[:, h * DH:h * DH + HALF]
        x2 = x[:, h * DH + HALF:(h + 1) * DH]
        r = jnp.concatenate([x1 * cos - x2 * sin, x1 * sin + x2 * cos], axis=1)
        out.append(r if scale is None else r * scale)
    return out


def _token_of_row(i):
    return (i & (SUBLANES - 1)) * SEG + (i >> 3)


def _split3_bf16(x):
    p0 = x.astype(_BF16)
    r = x - p0.astype(_F32)
    p1 = r.astype(_BF16)
    p2 = (r - p1.astype(_F32)).astype(_BF16)
    return p0, p1, p2


def _ada_kernel(c_ref, w_ref, b_ref, o_ref):
    c = c_ref[...]
    w = w_ref[...]
    c_hi = c.astype(_BF16)
    c_lo = (c - c_hi.astype(_F32)).astype(_BF16)
    w_hi = w.astype(_BF16)
    w_lo = (w - w_hi.astype(_F32)).astype(_BF16)
    o_ref[...] = _dot(c_hi, w_hi) + _dot(c_hi, w_lo) + _dot(c_lo, w_hi) + b_ref[...]


def _ada_call(c_all, w_ada, b_ada):
    rows = c_all.shape[0]
    return pl.pallas_call(
        _ada_kernel,
        grid=(DEPTH, 3),
        in_specs=[
            pl.BlockSpec((rows, D_MODEL), lambda l, n: (0, 0)),
            pl.BlockSpec((None, D_MODEL, D_MODEL), lambda l, n: (l, 0, n)),
            pl.BlockSpec((None, 1, D_MODEL), lambda l, n: (l, 0, n)),
        ],
        out_specs=pl.BlockSpec((None, rows, D_MODEL), lambda l, n: (l, 0, n)),
        out_shape=jax.ShapeDtypeStruct((DEPTH, rows, 3 * D_MODEL), _F32),
        name="adaln",
    )(c_all, w_ada, b_ada.reshape(DEPTH, 1, 3 * D_MODEL))


def _shifted_conv(xin, cs, carry_ref, vec_ref, w0, brow, o_state):
    T = T_TILE
    tail = (CONV_W - 1) * SUBLANES
    prev = carry_ref[:, cs]
    cur = xin[T - tail:T, :]
    sub = lax.broadcasted_iota(jnp.int32, (SUBLANES, xin.shape[1]), 0)
    lead = []
    for q in range(CONV_W - 1):
        sl = slice(q * SUBLANES, (q + 1) * SUBLANES)
        lead.append(jnp.where(sub == 0, pltpu.roll(prev[sl], 1, axis=0), pltpu.roll(cur[sl], 1, axis=0)))
    ext = jnp.concatenate(lead + [xin], axis=0)
    acc = vec_ref[brow:brow + 1, cs] + vec_ref[w0 + 3:w0 + 4, cs] * xin
    for d in range(1, CONV_W):
        acc = acc + vec_ref[w0 + 3 - d:w0 + 4 - d, cs] * ext[tail - SUBLANES * d:tail - SUBLANES * d + T, :]
    carry_ref[:, cs] = cur
    o_state[0, :, cs] = jnp.concatenate(
        [cur[q * SUBLANES + SUBLANES - 1:(q + 1) * SUBLANES, :] for q in range(CONV_W - 1)], axis=0)
    return acc


def _segment_scan(a, bt, h0):
    hs, cums = [], []
    h = cum = None
    for r in range(N_GROUPS):
        sl = slice(r * SUBLANES, (r + 1) * SUBLANES)
        if r == 0:
            h, cum = bt[sl], a[sl]
        else:
            h, cum = a[sl] * h + bt[sl], a[sl] * cum
        hs.append(h)
        cums.append(cum)
    carry = h0
    carries = [carry]
    for s in range(SUBLANES):
        carry = cum[s:s + 1, :] * carry + h[s:s + 1, :]
        carries.append(carry)
    carry_in = jnp.concatenate(carries[:SUBLANES], axis=0)
    out = jnp.concatenate([hs[r] + cums[r] * carry_in for r in range(N_GROUPS)], axis=0)
    return out, carries[SUBLANES]


def _prompt_kernel(x_ref, mod_ref, vec_ref, bif_ref, inv_ref,
                   win_ref, wax_ref, wq_ref, wk_ref, wv_ref, wif_ref, wout_ref, *rest):
    (y_ref, o_rgh, o_rgc, o_C, o_n, o_m, o_mlc, o_S,
     cc_rg, cc_ml, qkv_s, xcml_s, mix_s, m_s, cos_s, sin_s,
     mask_s, tri_s, dret_s, cross_s, wk_s) = rest[-21:]
    T = T_TILE
    b = pl.program_id(0)
    j = pl.program_id(1)

    @pl.when(jnp.logical_and(b == 0, j == 0))
    def _init_tables():
        tr = _token_of_row(lax.broadcasted_iota(jnp.int32, (T, T), 0))
        tc = _token_of_row(lax.broadcasted_iota(jnp.int32, (T, T), 1))
        causal = tr >= tc
        mask_s[...] = jnp.where(causal, 0.0, -jnp.inf)
        tri_s[...] = jnp.where(causal, 1.0, 0.0).astype(_BF16)
        rel = (tr - tc).astype(_F32)
        tokf = _token_of_row(lax.broadcasted_iota(jnp.int32, (T, DH), 0)).astype(_F32)
        for h in range(N_HEADS):
            lg = _RET_LOG_GAMMA[h]
            dret_s[h] = jnp.where(causal, jnp.exp(rel * lg), 0.0)
            cross_s[h] = jnp.exp((tokf + 1.0) * lg)
            wk_s[h] = jnp.exp((T - 1.0 - tokf) * lg)

    @pl.when(b == 0)
    def _init_rope():
        tok = _token_of_row(lax.broadcasted_iota(jnp.int32, (T, HALF), 0))
        c, s = _rope_tables((j * T + tok).astype(_F32), inv_ref[...])
        cos_s[pl.ds(pl.multiple_of(j * T, T), T), :] = c
        sin_s[pl.ds(pl.multiple_of(j * T, T), T), :] = s

    @pl.when(j == 0)
    def _init_state():
        o_rgh[...] = jnp.zeros_like(o_rgh)
        o_C[...] = jnp.zeros_like(o_C)
        o_n[...] = jnp.zeros_like(o_n)
        o_S[...] = jnp.zeros_like(o_S)
        m_s[...] = jnp.zeros_like(m_s)
        cc_rg[...] = jnp.zeros_like(cc_rg)
        cc_ml[...] = jnp.zeros_like(cc_ml)

    x = x_ref[0]
    shift = mod_ref[0, :, 0:D_MODEL]
    scale = mod_ref[0, :, D_MODEL:2 * D_MODEL]
    gate = mod_ref[0, :, 2 * D_MODEL:3 * D_MODEL]
    u = _rms(x) * (vec_ref[V_NPRE:V_NPRE + 1, :] * (1.0 + scale)) + shift
    ub = u.astype(_BF16)
    n_chunk = D_MODEL // MXU_N
    n_wide = D_MODEL // W_CHUNK

    pending = [(P_XML, w) for w in range(n_wide)]
    pending += [(blk, w) for w in range(n_wide) for blk in (P_XRG, P_ZRG, P_GA)]
    pending += [(blk, w) for w in range(n_wide) for blk in (P_ZML, P_GB)]
    pending += [(blk, w) for w in range(n_wide) for blk in (P_QR, P_KR, P_VR, P_GR, P_GC)]
    ready = {}

    def prefetch(n):
        for _ in range(min(n, len(pending))):
            blk, w = pending.pop(0)
            ready[(blk, w)] = _dot(ub, win_ref[blk * n_wide + w])

    def proj(blk, c):
        per_wide = W_CHUNK // MXU_N
        key = (blk, c // per_wide)
        while key not in ready:
            prefetch(1)
        lo = (c % per_wide) * MXU_N
        return ready[key][:, lo:lo + MXU_N]

    prefetch(PREFETCH_START)
    for h in range(N_HEADS):
        cs = slice(h * DH, (h + 1) * DH)
        prefetch(PREFETCH_B1[h])
        x_ml = proj(P_XML, h)
        xc_ml = _silu(_shifted_conv(x_ml, cs, cc_ml, vec_ref, V_MLCW, V_MLCB, o_mlc))
        xcml_s[:, cs] = xc_ml
        xhb = xc_ml.astype(_BF16)
        q = _dot(xhb, wq_ref[h]).astype(_BF16)
        k = (_dot(xhb, wk_ref[h]) * (DH ** -0.5)).astype(_BF16)
        v = _dot(x_ml.astype(_BF16), wv_ref[h]).astype(_BF16)
        qkv_s[:, cs] = q
        qkv_s[:, D_MODEL + h * DH:D_MODEL + (h + 1) * DH] = k
        qkv_s[:, 2 * D_MODEL + h * DH:2 * D_MODEL + (h + 1) * DH] = v
    mask = mask_s[...]

    def gates_sums():
        half = 3 * D_MODEL // 2
        gpre = (_dot(qkv_s[:, :half], wif_ref[:half, :]) + _dot(qkv_s[:, half:], wif_ref[half:, :])
                + bif_ref[...])
        lf = _log_sigmoid(gpre[:, LANES:])
        F = sum(_dot(tri_s[...], part) for part in _split3_bf16(lf))
        uu = gpre[:, :LANES] - F
        return F, uu, uu.T

    def gates_running_max(uu_rows):
        lane = lax.broadcasted_iota(jnp.int32, (T, LANES), 1)
        cmax = jnp.zeros((T, LANES), _F32)
        for h in range(N_HEADS):
            cm_h = jnp.max(uu_rows[h:h + 1, :] + mask, axis=1, keepdims=True)
            cmax = jnp.where(lane == h, cm_h, cmax)
        return cmax

    def gates_columns(F, uu, cmax):
        m0 = m_s[0:1, :]
        mrun = jnp.maximum(m0, cmax)
        m = F + mrun
        mL = m[T - 1:T, :]
        FL = F[T - 1:T, :]
        m_s[...] = jnp.broadcast_to(mL, m_s.shape)
        o_m[0] = m_s[...]
        return (-mrun,
                jnp.exp(m0 - mrun),
                jnp.exp(-m),
                jnp.exp(FL - mL + uu),
                jnp.broadcast_to(jnp.exp(FL + m0 - mL), (T, LANES)))

    prefetch(PREFETCH_GATES)
    F, uu, uu_rows = gates_sums()
    for c in range(n_chunk):
        cs = slice(c * MXU_N, (c + 1) * MXU_N)
        x_rg = proj(P_XRG, c)
        prefetch(1)
        xc = _shifted_conv(x_rg, cs, cc_rg, vec_ref, V_RGCW, V_RGCB, o_rgc)
        prefetch(1)
        a, bt = _rg_gates(xc, vec_ref, wax_ref, c * MXU_N)
        prefetch(PREFETCH_A - 2)
        h_rg, h_last = _segment_scan(a, bt, o_rgh[0, :, cs])
        o_rgh[0, :, cs] = h_last
        mix_s[:, cs] = _sigmoid_of_twice(proj(P_GA, c)) * (h_rg * _silu_of_twice(proj(P_ZRG, c)))
        if c == 0:
            cmax = gates_running_max(uu_rows)
        if c == 1:
            c_row, c_inter, c_floor, c_wl, c_decay = gates_columns(F, uu, cmax)

    for h in range(N_HEADS):
        cs = slice(h * DH, (h + 1) * DH)
        prefetch(1)
        qh = qkv_s[:, h * DH:(h + 1) * DH]
        kh = qkv_s[:, D_MODEL + h * DH:D_MODEL + (h + 1) * DH]
        vh = qkv_s[:, 2 * D_MODEL + h * DH:2 * D_MODEL + (h + 1) * DH]
        col = lambda c: c[:, h:h + 1]
        dmat = jnp.exp(col(c_row) + (uu_rows[h:h + 1, :] + mask))
        s = lax.dot_general(qh, kh, _NT, preferred_element_type=_F32) * dmat
        C0 = o_C[0, h]
        n0 = o_n[0, h:h + 1, :]
        num = col(c_inter) * _dot(qh, C0.astype(_BF16)) + _dot(s.astype(_BF16), vh)
        den = (col(c_inter) * jnp.sum(qh.astype(_F32) * n0, axis=-1, keepdims=True)
               + jnp.sum(s, axis=-1, keepdims=True))
        h_ml = num * (1.0 / jnp.maximum(jnp.abs(den), col(c_floor)))
        kw = kh.astype(_F32) * col(c_wl)
        o_C[0, h] = col(c_decay) * C0 + lax.dot_general(
            kw.astype(_BF16), vh, _TN, preferred_element_type=_F32)
        o_n[0, h:h + 1, :] = col(c_decay)[0:1, :] * n0 + jnp.sum(kw, axis=0, keepdims=True)
        y_b = (_headnorm(h_ml) * vec_ref[V_MLGN:V_MLGN + 1, cs]
               + vec_ref[V_MLSKIP:V_MLSKIP + 1, cs] * xcml_s[:, cs])
        mix_s[:, cs] += _sigmoid_of_twice(proj(P_GB, h)) * (y_b * _silu_of_twice(proj(P_ZML, h)))

    t0 = pl.multiple_of(j * T, T)
    cos = cos_s[pl.ds(t0, T), :]
    sin = sin_s[pl.ds(t0, T), :]

    def rope(xh):
        x1, x2 = xh[:, :HALF], xh[:, HALF:]
        return jnp.concatenate([x1 * cos - x2 * sin, x1 * sin + x2 * cos], axis=1)

    for h in range(N_HEADS):
        cs = slice(h * DH, (h + 1) * DH)
        prefetch(PREFETCH_C)
        qh = rope(proj(P_QR, h)).astype(_BF16)
        kf = rope(proj(P_KR, h)) * (DH ** -0.5)
        vh = proj(P_VR, h).astype(_BF16)
        s = lax.dot_general(qh, kf.astype(_BF16), _NT, preferred_element_type=_F32) * dret_s[h]
        S0 = o_S[0, h]
        h_ret = _dot(s.astype(_BF16), vh) + _dot(qh, S0.astype(_BF16)) * cross_s[h]
        kw = (kf * wk_s[h]).astype(_BF16)
        o_S[0, h] = math.exp(T * _RET_LOG_GAMMA[h]) * S0 + lax.dot_general(
            kw, vh, _TN, preferred_element_type=_F32)
        y_c = _headnorm(h_ret) * vec_ref[V_RETGN:V_RETGN + 1, cs] * _silu_of_twice(proj(P_GR, h))
        mix_s[:, cs] += _sigmoid_of_twice(proj(P_GC, h)) * y_c

    out = _dot_chunks(mix_s[...].astype(_BF16), wout_ref, 0, n_wide)
    y_ref[0] = x + gate * (_rms(out) * vec_ref[V_NPOST:V_NPOST + 1, :])


def _prompt_layer(layer, prev_big, mod_row0, x, mod, vec, bif2, inv, win, wax, wq, wk, wv, wif2, wout):
    B, L, _ = x.shape
    T = T_TILE
    per_b3 = lambda b, j: (b, 0, 0)
    big_blk = pl.BlockSpec((None, 1, N_HEADS, DH, DH), lambda b, j: (layer, b, 0, 0, 0))
    big_shape = jax.ShapeDtypeStruct((DEPTH, B, N_HEADS, DH, DH), _F32)

    def resident(*shape):
        return pl.BlockSpec((None,) + shape, lambda b, j: (layer,) + (0,) * len(shape),
                            pipeline_mode=pl.Buffered(1))

    n_in = 12
    extra_specs = [pl.BlockSpec(memory_space=pl.ANY)] * len(prev_big)
    aliases = {n_in: 3, n_in + 1: 7} if prev_big else {}
    tail = (CONV_W - 1) * SUBLANES
    outs = pl.pallas_call(
        _prompt_kernel,
        grid=(B, L // T),
        input_output_aliases=aliases,
        in_specs=[
            pl.BlockSpec((1, T, D_MODEL), lambda b, j: (b, j, 0)),
            pl.BlockSpec((None, 1, 1, 3 * D_MODEL), lambda b, j: (layer, mod_row0 + b, 0, 0)),
            resident(V_ROWS, D_MODEL),
            resident(1, 2 * LANES),
            pl.BlockSpec((1, HALF), lambda b, j: (0, 0), pipeline_mode=pl.Buffered(1)),
            resident(D_IN // W_CHUNK, D_MODEL, W_CHUNK),
            resident(RG_HEADS, RG_BW, 2 * RG_BW),
            resident(N_HEADS, DH, DH),
            resident(N_HEADS, DH, DH),
            resident(N_HEADS, DH, DH),
            resident(3 * D_MODEL, 2 * LANES),
            resident(D_MODEL // W_CHUNK, D_MODEL, W_CHUNK),
        ] + extra_specs,
        out_specs=[
            pl.BlockSpec((1, T, D_MODEL), lambda b, j: (b, j, 0)),
            pl.BlockSpec((1, 1, D_MODEL), per_b3),
            pl.BlockSpec((1, CONV_W - 1, D_MODEL), per_b3),
            big_blk,
            pl.BlockSpec((1, N_HEADS, DH), per_b3),
            pl.BlockSpec((1, SUBLANES, LANES), per_b3),
            pl.BlockSpec((1, CONV_W - 1, D_MODEL), per_b3),
            big_blk,
        ],
        out_shape=[
            jax.ShapeDtypeStruct((B, L, D_MODEL), _F32),
            jax.ShapeDtypeStruct((B, 1, D_MODEL), _F32),
            jax.ShapeDtypeStruct((B, CONV_W - 1, D_MODEL), _F32),
            big_shape,
            jax.ShapeDtypeStruct((B, N_HEADS, DH), _F32),
            jax.ShapeDtypeStruct((B, SUBLANES, LANES), _F32),
            jax.ShapeDtypeStruct((B, CONV_W - 1, D_MODEL), _F32),
            big_shape,
        ],
        scratch_shapes=[
            pltpu.VMEM((tail, D_MODEL), _F32),
            pltpu.VMEM((tail, D_MODEL), _F32),
            pltpu.VMEM((T, 3 * D_MODEL), _BF16),
            pltpu.VMEM((T, D_MODEL), _F32),
            pltpu.VMEM((T, D_MODEL), _F32),
            pltpu.VMEM((SUBLANES, LANES), _F32),
            pltpu.VMEM((L, HALF), _F32),
            pltpu.VMEM((L, HALF), _F32),
            pltpu.VMEM((T, T), _F32),
            pltpu.VMEM((T, T), _BF16),
            pltpu.VMEM((N_HEADS, T, T), _F32),
            pltpu.VMEM((N_HEADS, T, DH), _F32),
            pltpu.VMEM((N_HEADS, T, DH), _F32),
        ],
        compiler_params=pltpu.CompilerParams(
            dimension_semantics=("arbitrary", "arbitrary"),
            vmem_limit_bytes=VMEM_LIMIT),
        name="prompt_layer",
    )(x, mod, vec, bif2, inv, win, wax, wq, wk, wv, wif2, wout, *prev_big)
    y, rgh, rgc, mlC, mln, mlm, mlc, retS = outs
    return y, (rgh[:, 0], rgc, mln, mlm[:, 0, :N_HEADS], mlc), (mlC, retS)


def _interleave_tiles(x):
    B, L, D = x.shape
    return x.reshape(B, L // T_TILE, SUBLANES, SEG, D).transpose(0, 1, 3, 2, 4).reshape(B, L, D)


def _deinterleave_tiles(x):
    B, L, D = x.shape
    return x.reshape(B, L // T_TILE, SEG, SUBLANES, D).transpose(0, 1, 3, 2, 4).reshape(B, L, D)


def _sample_proj_kernel(x_ref, mod_ref, vec_ref, w_ref, o_ref):
    scale = mod_ref[:, D_MODEL:2 * D_MODEL]
    u = _rms(x_ref[...]) * (vec_ref[V_NPRE:V_NPRE + 1, :] * (1.0 + scale)) + mod_ref[:, 0:D_MODEL]
    o_ref[...] = _dot_chunks(u.astype(_BF16), w_ref, 0, w_ref.shape[0])


def _sample_proj(layer, x, mod, vec, win):
    rows = x.shape[0]
    n_steps = D_MODEL // W_CHUNK
    return pl.pallas_call(
        _sample_proj_kernel,
        grid=(n_steps,),
        in_specs=[
            pl.BlockSpec((rows, D_MODEL), lambda n: (0, 0)),
            pl.BlockSpec((None, rows, 3 * D_MODEL), lambda n: (layer, 0, 0)),
            pl.BlockSpec((None, V_ROWS, D_MODEL), lambda n: (layer, 0, 0)),
            pl.BlockSpec((None, N_PROJ, D_MODEL, W_CHUNK), lambda n: (layer, n, 0, 0)),
        ],
        out_specs=pl.BlockSpec((rows, N_PROJ * W_CHUNK), lambda n: (0, n)),
        out_shape=jax.ShapeDtypeStruct((rows, D_IN), _F32),
        compiler_params=pltpu.CompilerParams(vmem_limit_bytes=VMEM_LIMIT),
        name="sample_proj",
    )(x, mod, vec, win)


S_BT = 2


def _bcast_heads(cols):
    rows = cols[0].shape[0]
    return jnp.concatenate([jnp.broadcast_to(c, (rows, DH)) for c in cols], axis=1)


def _sample_mix_kernel(proj_ref, x_ref, mod_ref, vec_ref, bif_ref, inv_ref,
                       wax_ref, wq_ref, wk_ref, wv_ref, wif_ref, wout_ref,
                       rgh_ref, rgc_ref, mlc_ref, mln_ref, mlm_ref, C_ref, S_ref, *rest):
    (y_ref, o_rgh, o_rgc, o_mlc, o_mln, o_mlm, o_C, o_S,
     mix_s, xcml_s, inter_s, dv_s, sv_s, rden_s, svr_s, vr_s, qC_s, qS_s,
     qT_s, kT_s, qrT_s, krT_s) = rest[-22:]
    i = pl.program_id(0)
    n_steps = pl.num_programs(0)
    R = x_ref.shape[0]

    def p(blk):
        return proj_ref[:, blk * D_MODEL:(blk + 1) * D_MODEL]

    def conv(xin, cs_ref, o_state, w0, brow):
        cs = cs_ref[...]
        acc = vec_ref[brow:brow + 1, :] + vec_ref[w0 + 3:w0 + 4, :] * xin
        for k in range(CONV_W - 1):
            acc = acc + vec_ref[w0 + k:w0 + k + 1, :] * cs[:, k * D_MODEL:(k + 1) * D_MODEL]
        o_state[...] = jnp.concatenate([cs[:, D_MODEL:], xin], axis=1)
        return acc

    @pl.when(i == 0)
    def _dense():
        xc = conv(p(P_XRG), rgc_ref, o_rgc, V_RGCW, V_RGCB)
        a, bt = _rg_gates(xc, vec_ref, wax_ref)
        h = a * rgh_ref[...] + bt
        o_rgh[...] = h
        mix_s[...] = _sigmoid_of_twice(p(P_GA)) * (h * _silu_of_twice(p(P_ZRG)))

        x_ml = p(P_XML)
        xc_ml = _silu(conv(x_ml, mlc_ref, o_mlc, V_MLCW, V_MLCB))
        xcml_s[...] = xc_ml
        xhb = xc_ml.astype(_BF16)
        xmb = x_ml.astype(_BF16)
        qs, ks, vs = [], [], []
        for h_ in range(N_HEADS):
            sl = slice(h_ * DH, (h_ + 1) * DH)
            qs.append(_dot(xhb[:, sl], wq_ref[h_]))
            ks.append(_dot(xhb[:, sl], wk_ref[h_]) * (DH ** -0.5))
            vs.append(_dot(xmb[:, sl], wv_ref[h_]))
        q = jnp.concatenate(qs, axis=1)
        k = jnp.concatenate(ks, axis=1)
        v = jnp.concatenate(vs, axis=1)
        qkvb = jnp.concatenate([q, k, v], axis=1).astype(_BF16)
        gpre = _dot(qkvb, wif_ref[...]) + bif_ref[...]
        ig = gpre[:, :LANES]
        lf = _log_sigmoid(gpre[:, LANES:])
        m0 = mlm_ref[...]
        m = jnp.maximum(m0 + lf, ig)
        o_mlm[...] = m
        dg = jnp.exp(ig - m)
        inter = jnp.exp(lf + m0 - m)
        floor = jnp.exp(-m)
        n0 = mln_ref[...]
        s_cols, rden_cols = [], []
        for h_ in range(N_HEADS):
            sl = slice(h_ * DH, (h_ + 1) * DH)
            s_h = jnp.sum(q[:, sl] * k[:, sl], axis=-1, keepdims=True) * dg[:, h_:h_ + 1]
            qn = jnp.sum(q[:, sl] * n0[:, sl], axis=-1, keepdims=True)
            den = inter[:, h_:h_ + 1] * qn + s_h
            s_cols.append(s_h)
            rden_cols.append(1.0 / jnp.maximum(jnp.abs(den), floor[:, h_:h_ + 1]))
        dg_b = _bcast_heads([dg[:, h_:h_ + 1] for h_ in range(N_HEADS)])
        inter_b = _bcast_heads([inter[:, h_:h_ + 1] for h_ in range(N_HEADS)])
        inter_s[...] = inter_b
        dv_s[...] = dg_b * v
        sv_s[...] = _bcast_heads(s_cols) * v
        rden_s[...] = _bcast_heads(rden_cols)
        o_mln[...] = inter_b * n0 + dg_b * k
        qT_s[...] = q.T
        kT_s[...] = k.T

        pos = jnp.full((1, HALF), float(PAST_LEN), _F32)
        cos, sin = _rope_tables(pos, inv_ref[...])
        qr = jnp.concatenate(_rope(p(P_QR), cos, sin), axis=1)
        kr = jnp.concatenate(_rope(p(P_KR), cos, sin, DH ** -0.5), axis=1)
        v_r = p(P_VR)
        s_cols = [jnp.sum(qr[:, h_ * DH:(h_ + 1) * DH] * kr[:, h_ * DH:(h_ + 1) * DH],
                          axis=-1, keepdims=True) for h_ in range(N_HEADS)]
        svr_s[...] = _bcast_heads(s_cols) * v_r
        vr_s[...] = v_r
        qrT_s[...] = qr.T
        krT_s[...] = kr.T

    lane = lax.broadcasted_iota(jnp.int32, (DH, R), 1)
    for bi in range(S_BT):
        b = i * S_BT + bi
        onehot = (lane == b).astype(_F32)

        def column(t_ref, h_):
            return jnp.sum(t_ref[h_ * DH:(h_ + 1) * DH, :] * onehot, axis=1, keepdims=True)

        for h_ in range(N_HEADS):
            sl = slice(h_ * DH, (h_ + 1) * DH)
            C0 = C_ref[bi, h_]
            qC_s[pl.ds(b, 1), sl] = jnp.sum(C0 * column(qT_s, h_), axis=0, keepdims=True)
            o_C[bi, h_] = (inter_s[pl.ds(b, 1), sl] * C0
                           + column(kT_s, h_) * dv_s[pl.ds(b, 1), sl])
            S0 = S_ref[bi, h_]
            qS_s[pl.ds(b, 1), sl] = jnp.sum(S0 * column(qrT_s, h_), axis=0, keepdims=True)
            o_S[bi, h_] = (math.exp(_RET_LOG_GAMMA[h_]) * S0
                           + column(krT_s, h_) * vr_s[pl.ds(b, 1), sl])

    @pl.when(i == n_steps - 1)
    def _epilogue():
        h_ml = (inter_s[...] * qC_s[...] + sv_s[...]) * rden_s[...]
        hn = jnp.concatenate(
            [_headnorm(h_ml[:, h_ * DH:(h_ + 1) * DH]) for h_ in range(N_HEADS)], axis=1)
        y_b = hn * vec_ref[V_MLGN:V_MLGN + 1, :] + vec_ref[V_MLSKIP:V_MLSKIP + 1, :] * xcml_s[...]
        mix = mix_s[...] + _sigmoid_of_twice(p(P_GB)) * (y_b * _silu_of_twice(p(P_ZML)))
        cross = jnp.concatenate(
            [jnp.full((1, DH), math.exp(_RET_LOG_GAMMA[h_]), _F32) for h_ in range(N_HEADS)], axis=1)
        h_ret = svr_s[...] + qS_s[...] * cross
        hn = jnp.concatenate(
            [_headnorm(h_ret[:, h_ * DH:(h_ + 1) * DH]) for h_ in range(N_HEADS)], axis=1)
        mix = mix + _sigmoid_of_twice(p(P_GC)) * (
            hn * vec_ref[V_RETGN:V_RETGN + 1, :] * _silu_of_twice(p(P_GR)))
        out = _dot_chunks(mix.astype(_BF16), wout_ref, 0, D_MODEL // W_CHUNK)
        gate = mod_ref[:, 2 * D_MODEL:3 * D_MODEL]
        y_ref[...] = x_ref[...] + gate * (_rms(out) * vec_ref[V_NPOST:V_NPOST + 1, :])


def _sample_mix(layer, prev_big, proj, x, mod, vec, bif2, inv, wax, wq, wk, wv, wif2, wout,
                rgh, rgc, mlc, mln, mlm, mlC_all, retS_all):
    R = x.shape[0]
    c2 = lambda i: (0, 0)
    st5 = lambda i: (layer, i, 0, 0, 0)

    def resident(*shape):
        return pl.BlockSpec((None,) + shape, lambda i: (layer,) + (0,) * len(shape),
                            pipeline_mode=pl.Buffered(1))

    def resident2(*shape):
        return pl.BlockSpec(shape, c2, pipeline_mode=pl.Buffered(1))

    row_blk = pl.BlockSpec((R, D_MODEL), c2)
    conv_blk = pl.BlockSpec((R, (CONV_W - 1) * D_MODEL), c2)
    lane_blk = pl.BlockSpec((R, LANES), c2)
    state_blk = pl.BlockSpec((None, S_BT, N_HEADS, DH, DH), st5)
    f32 = lambda *s: jax.ShapeDtypeStruct(s, _F32)
    n_in = 19
    outs = pl.pallas_call(
        _sample_mix_kernel,
        grid=(R // S_BT,),
        input_output_aliases={n_in: 6, n_in + 1: 7} if prev_big else {},
        in_specs=[
            resident2(R, D_IN),
            resident2(R, D_MODEL),
            resident(R, 3 * D_MODEL),
            resident(V_ROWS, D_MODEL),
            resident(1, 2 * LANES),
            resident2(1, HALF),
            resident(RG_HEADS, RG_BW, 2 * RG_BW),
            resident(N_HEADS, DH, DH),
            resident(N_HEADS, DH, DH),
            resident(N_HEADS, DH, DH),
            resident(3 * D_MODEL, 2 * LANES),
            resident(D_MODEL // W_CHUNK, D_MODEL, W_CHUNK),
            resident(R, D_MODEL),
            resident(R, (CONV_W - 1) * D_MODEL),
            resident(R, (CONV_W - 1) * D_MODEL),
            resident(R, D_MODEL),
            resident(R, LANES),
            state_blk,
            state_blk,
        ] + [pl.BlockSpec(memory_space=pl.ANY)] * len(prev_big),
        out_specs=[row_blk, row_blk, conv_blk, conv_blk, row_blk, lane_blk, state_blk, state_blk],
        out_shape=[
            f32(R, D_MODEL), f32(R, D_MODEL), f32(R, (CONV_W - 1) * D_MODEL),
            f32(R, (CONV_W - 1) * D_MODEL), f32(R, D_MODEL), f32(R, LANES),
            f32(DEPTH, R, N_HEADS, DH, DH), f32(DEPTH, R, N_HEADS, DH, DH),
        ],
        scratch_shapes=(
            [pltpu.VMEM((R, D_MODEL), _F32) for _ in range(10)]
            + [pltpu.VMEM((D_MODEL, R), _F32) for _ in range(4)]),
        compiler_params=pltpu.CompilerParams(
            dimension_semantics=("arbitrary",), vmem_limit_bytes=VMEM_LIMIT),
        name="sample_mix",
    )(proj, x, mod, vec, bif2, inv, wax, wq, wk, wv, wif2, wout,
      rgh, rgc, mlc, mln, mlm, mlC_all, retS_all, *prev_big)
    y, o_rgh, o_rgc, o_mlc, o_mln, o_mlm, o_C, o_S = outs
    return y, (o_rgh, o_rgc.reshape(R, CONV_W - 1, D_MODEL),
               o_mln.reshape(R, N_HEADS, DH), o_mlm[:, :N_HEADS],
               o_mlc.reshape(R, CONV_W - 1, D_MODEL)), (o_C, o_S)


def _pack_params(norm_pre, norm_post, rg_conv_w, rg_conv_b, rg_b_a, rg_b_x, rg_lambda,
                 ml_conv_w, ml_conv_b, ml_skip, ml_gn, ret_gn):
    row = lambda p: p[:, None, :]
    rows = [row(norm_pre), row(norm_post), rg_conv_w, row(rg_conv_b), row(0.5 * rg_b_a), row(0.5 * rg_b_x),
            row(rg_lambda), ml_conv_w, row(ml_conv_b), row(ml_skip), row(ml_gn), row(ret_gn)]
    vec = jnp.concatenate(rows, axis=1)
    return jnp.pad(vec, ((0, 0), (0, V_ROWS - vec.shape[1]), (0, 0)))


def _column_chunks(w):
    depth, K, N = w.shape
    return w.astype(_BF16).reshape(depth, K, N // W_CHUNK, W_CHUNK).transpose(0, 2, 1, 3)


def kernel(x_prompt, x_sample, c_prompt, c_sample, state_rg_h, state_rg_conv, state_ml_C, state_ml_n, state_ml_m, state_ml_conv, state_ret_S, w_ada, b_ada, norm_pre, norm_post, w_in, rg_conv_w, rg_conv_b, rg_w_a, rg_b_a, rg_w_x, rg_b_x, rg_lambda, ml_conv_w, ml_conv_b, ml_w_q, ml_w_k, ml_w_v, ml_w_if, ml_b_if, ml_skip, ml_gn, ret_gn, w_out):
    xs = x_sample[:, 0, :]
    R = xs.shape[0]
    mod_all = _ada_call(jnp.concatenate([c_sample, c_prompt], axis=0), w_ada, b_ada)
    mod_rows = mod_all.reshape(DEPTH, mod_all.shape[1], 1, 3 * D_MODEL)
    inv = (ROPE_BASE ** (-jnp.linspace(0.0, 1.0, HALF, dtype=_F32)))[None, :]
    vec = _pack_params(norm_pre, norm_post, rg_conv_w, rg_conv_b, rg_b_a, rg_b_x,
                       rg_lambda, ml_conv_w, ml_conv_b, ml_skip, ml_gn, ret_gn)
    halved = jnp.zeros((N_PROJ,), _F32).at[jnp.array([P_ZRG, P_ZML, P_GR, P_GA, P_GB, P_GC])].set(1.0)
    col_scale = jnp.repeat(1.0 - 0.5 * halved, D_MODEL)
    win = _column_chunks(w_in * col_scale)
    wout = _column_chunks(w_out)
    wax = (0.5 * jnp.concatenate([rg_w_a, rg_w_x], axis=-1)).astype(_BF16)
    wq = ml_w_q.astype(_BF16)
    wk = ml_w_k.astype(_BF16)
    wv = ml_w_v.astype(_BF16)
    lane_pad = ((0, 0), (0, 0), (0, LANES - N_HEADS))
    wif2 = jnp.concatenate([jnp.pad(ml_w_if[..., :N_HEADS], lane_pad),
                            jnp.pad(ml_w_if[..., N_HEADS:], lane_pad)], axis=-1).astype(_BF16)
    b_if = ml_b_if[:, None, :]
    bif2 = jnp.concatenate([jnp.pad(b_if[..., :N_HEADS], lane_pad),
                            jnp.pad(b_if[..., N_HEADS:], lane_pad)], axis=-1)
    s_rgc_in = state_rg_conv.reshape(DEPTH, R, -1)
    s_mlc_in = state_ml_conv.reshape(DEPTH, R, -1)
    s_mln_in = state_ml_n.reshape(DEPTH, R, -1)
    s_mlm_in = jnp.pad(state_ml_m, lane_pad)

    xp = _interleave_tiles(x_prompt)
    st_p, st_s = [], []
    big_p, big_s = (), ()
    for l in range(DEPTH):
        xp, sp, big_p = _prompt_layer(l, big_p, R, xp, mod_rows, vec, bif2, inv, win, wax, wq, wk, wv, wif2, wout)
        st_p.append(sp)
        proj = _sample_proj(l, xs, mod_all, vec, win)
        xs, ss, big_s = _sample_mix(
            l, big_s, proj, xs, mod_all, vec, bif2, inv, wax, wq, wk, wv, wif2, wout,
            state_rg_h, s_rgc_in, s_mlc_in, s_mln_in, s_mlm_in, state_ml_C, state_ret_S)
        st_s.append(ss)
    p_rgh, p_rgc, p_n, p_m, p_mlc = [jnp.stack(a) for a in zip(*st_p)]
    s_rgh, s_rgc, s_n, s_m, s_mlc = [jnp.stack(a) for a in zip(*st_s)]
    return (_deinterleave_tiles(xp), xs[:, None, :], p_rgh, p_rgc, big_p[0], p_n, p_m, p_mlc, big_p[1],
            s_rgh, s_rgc, big_s[0], s_n, s_m, s_mlc, big_s[1])
```

```python
import math

import jax
import jax.numpy as jnp
from jax import lax
from jax.experimental import pallas as pl
from jax.experimental.pallas import tpu as pltpu

D_MODEL = 1024
DEPTH = 2
CONV_W = 4
RG_HEADS = 8
RG_BW = D_MODEL // RG_HEADS
RG_C = 8.0
N_HEADS = 4
DH = D_MODEL // N_HEADS
HALF = DH // 2
PAST_LEN = 16384
ROPE_BASE = 10000.0
EPS = 1e-6
N_PROJ = 11
D_IN = N_PROJ * D_MODEL
P_XRG, P_ZRG, P_XML, P_ZML, P_QR, P_KR, P_VR, P_GR, P_GA, P_GB, P_GC = range(N_PROJ)
P_GATE_LIKE = (P_ZRG, P_ZML, P_GR, P_GA, P_GB, P_GC)

V_NPRE, V_NPOST, V_RGCW, V_RGCB, V_RGBA, V_RGBX, V_RGLAM = 0, 1, 2, 6, 7, 8, 9
V_MLCW, V_MLCB, V_MLSKIP, V_MLGN, V_RETGN, V_ROWS = 10, 14, 15, 16, 17, 24

SUBLANES = 8
LANES = 128
MXU_N = 256
W_CHUNK = MXU_N
T_TILE = 256
SEG = T_TILE // SUBLANES
N_GROUPS = T_TILE // SUBLANES
VMEM_LIMIT = 60 * 1024 * 1024
PREFETCH_START = 1
PREFETCH_B1 = (1, 1, 1, 1)
PREFETCH_GATES = 1
PREFETCH_A = 4
PREFETCH_B2 = 3
PREFETCH_C = 2

_RET_LOG_GAMMA = tuple(math.log1p(-2.0 ** (-5.0 - h)) for h in range(N_HEADS))

_NT = (((1,), (1,)), ((), ()))
_TN = (((0,), (0,)), ((), ()))
_F32 = jnp.float32
_BF16 = jnp.bfloat16


def _dot(a, b):
    return jnp.dot(a, b, preferred_element_type=_F32)


def _dot_chunks(a, w_ref, first, n):
    return jnp.concatenate([_dot(a, w_ref[first + c]) for c in range(n)], axis=1)


def _sigmoid_of_twice(hx):
    return 0.5 * jnp.tanh(hx) + 0.5


def _silu_of_twice(hx):
    return hx * jnp.tanh(hx) + hx


def _silu(x):
    return _silu_of_twice(0.5 * x)


def _softplus(x):
    return jnp.maximum(x, 0.0) + jnp.log1p(jnp.exp(-jnp.abs(x)))


def _log_sigmoid(x):
    return -_softplus(-x)


def _neg_expm1_2x(x):
    t = jnp.tanh(x)
    return (-2.0 * t) / (1.0 - t)


def _rms(x):
    return x * lax.rsqrt(jnp.mean(x * x, axis=-1, keepdims=True) + EPS)


def _headnorm(h):
    mu = jnp.mean(h, axis=-1, keepdims=True)
    c = h - mu
    var = jnp.mean(c * c, axis=-1, keepdims=True)
    return c * lax.rsqrt(var + EPS)


def _rg_gates(xc, vec_ref, wax_ref, lane0=0):
    xcb = xc.astype(_BF16)
    a_parts, b_parts = [], []
    for hh in range(xc.shape[1] // RG_BW):
        loc = slice(hh * RG_BW, (hh + 1) * RG_BW)
        sl = slice(lane0 + hh * RG_BW, lane0 + (hh + 1) * RG_BW)
        g = _dot(xcb[:, loc], wax_ref[sl.start // RG_BW])
        r = _sigmoid_of_twice(g[:, :RG_BW] + vec_ref[V_RGBA:V_RGBA + 1, sl])
        i = _sigmoid_of_twice(g[:, RG_BW:] + vec_ref[V_RGBX:V_RGBX + 1, sl])
        log_a = (-RG_C) * r * _softplus(-vec_ref[V_RGLAM:V_RGLAM + 1, sl])
        a_parts.append(jnp.exp(log_a))
        b_parts.append(jnp.sqrt(_neg_expm1_2x(log_a)) * (i * xc[:, loc]))
    return jnp.concatenate(a_parts, axis=1), jnp.concatenate(b_parts, axis=1)


def _rope_tables(pos, inv_row):
    ang = pos * inv_row
    return jnp.cos(ang), jnp.sin(ang)


def _rope(x, cos, sin, scale=None):
    out = []
    for h in range(N_HEADS):
        x1 = x[:, h * DH:h * DH + HALF]
        x2 = x[:, h * DH + HALF:(h + 1) * DH]
        r = jnp.concatenate([x1 * cos - x2 * sin, x1 * sin + x2 * cos], axis=1)
        out.append(r if scale is None else r * scale)
    return out


def _token_of_row(i):
    return (i & (SUBLANES - 1)) * SEG + (i >> 3)


def _split3_bf16(x):
    p0 = x.astype(_BF16)
    r = x - p0.astype(_F32)
    p1 = r.astype(_BF16)
    p2 = (r - p1.astype(_F32)).astype(_BF16)
    return p0, p1, p2


def _ada_kernel(c_ref, w_ref, b_ref, o_ref):
    c = c_ref[...]
    w = w_ref[...]
    c_hi = c.astype(_BF16)
    c_lo = (c - c_hi.astype(_F32)).astype(_BF16)
    w_hi = w.astype(_BF16)
    w_lo = (w - w_hi.astype(_F32)).astype(_BF16)
    o_ref[...] = _dot(c_hi, w_hi) + _dot(c_hi, w_lo) + _dot(c_lo, w_hi) + b_ref[...]


def _ada_call(c_all, w_ada, b_ada):
    rows = c_all.shape[0]
    return pl.pallas_call(
        _ada_kernel,
        grid=(DEPTH, 3),
        in_specs=[
            pl.BlockSpec((rows, D_MODEL), lambda l, n: (0, 0)),
            pl.BlockSpec((None, D_MODEL, D_MODEL), lambda l, n: (l, 0, n)),
            pl.BlockSpec((None, 1, D_MODEL), lambda l, n: (l, 0, n)),
        ],
        out_specs=pl.BlockSpec((None, rows, D_MODEL), lambda l, n: (l, 0, n)),
        out_shape=jax.ShapeDtypeStruct((DEPTH, rows, 3 * D_MODEL), _F32),
        name="adaln",
    )(c_all, w_ada, b_ada.reshape(DEPTH, 1, 3 * D_MODEL))


def _shifted_conv(xin, cs, carry_ref, vec_ref, w0, brow, o_state):
    T = T_TILE
    tail = (CONV_W - 1) * SUBLANES
    prev = carry_ref[:, cs]
    cur = xin[T - tail:T, :]
    sub = lax.broadcasted_iota(jnp.int32, (SUBLANES, xin.shape[1]), 0)
    lead = []
    for q in range(CONV_W - 1):
        sl = slice(q * SUBLANES, (q + 1) * SUBLANES)
        lead.append(jnp.where(sub == 0, pltpu.roll(prev[sl], 1, axis=0), pltpu.roll(cur[sl], 1, axis=0)))
    ext = jnp.concatenate(lead + [xin], axis=0)
    acc = vec_ref[brow:brow + 1, cs] + vec_ref[w0 + 3:w0 + 4, cs] * xin
    for d in range(1, CONV_W):
        acc = acc + vec_ref[w0 + 3 - d:w0 + 4 - d, cs] * ext[tail - SUBLANES * d:tail - SUBLANES * d + T, :]
    carry_ref[:, cs] = cur
    o_state[0, :, cs] = jnp.concatenate(
        [cur[q * SUBLANES + SUBLANES - 1:(q + 1) * SUBLANES, :] for q in range(CONV_W - 1)], axis=0)
    return acc


def _segment_scan(a, bt, h0):
    hs, cums = [], []
    h = cum = None
    for r in range(N_GROUPS):
        sl = slice(r * SUBLANES, (r + 1) * SUBLANES)
        if r == 0:
            h, cum = bt[sl], a[sl]
        else:
            h, cum = a[sl] * h + bt[sl], a[sl] * cum
        hs.append(h)
        cums.append(cum)
    carry = h0
    carries = [carry]
    for s in range(SUBLANES):
        carry = cum[s:s + 1, :] * carry + h[s:s + 1, :]
        carries.append(carry)
    carry_in = jnp.concatenate(carries[:SUBLANES], axis=0)
    out = jnp.concatenate([hs[r] + cums[r] * carry_in for r in range(N_GROUPS)], axis=0)
    return out, carries[SUBLANES]


def _prompt_kernel(x_ref, mod_ref, vec_ref, bif_ref, inv_ref,
                   win_ref, wax_ref, wq_ref, wk_ref, wv_ref, wif_ref, wout_ref, *rest):
    (y_ref, o_rgh, o_rgc, o_C, o_n, o_m, o_mlc, o_S,
     cc_rg, cc_ml, qkv_s, xcml_s, mix_s, m_s, cos_s, sin_s,
     mask_s, tri_s, dret_s, cross_s, wk_s) = rest[-21:]
    T = T_TILE
    b = pl.program_id(0)
    j = pl.program_id(1)

    @pl.when(jnp.logical_and(b == 0, j == 0))
    def _init_tables():
        tr = _token_of_row(lax.broadcasted_iota(jnp.int32, (T, T), 0))
        tc = _token_of_row(lax.broadcasted_iota(jnp.int32, (T, T), 1))
        causal = tr >= tc
        mask_s[...] = jnp.where(causal, 0.0, -jnp.inf)
        tri_s[...] = jnp.where(causal, 1.0, 0.0).astype(_BF16)
        rel = (tr - tc).astype(_F32)
        tokf = _token_of_row(lax.broadcasted_iota(jnp.int32, (T, DH), 0)).astype(_F32)
        for h in range(N_HEADS):
            lg = _RET_LOG_GAMMA[h]
            dret_s[h] = jnp.where(causal, jnp.exp(rel * lg), 0.0)
            cross_s[h] = jnp.exp((tokf + 1.0) * lg)
            wk_s[h] = jnp.exp((T - 1.0 - tokf) * lg)

    @pl.when(b == 0)
    def _init_rope():
        tok = _token_of_row(lax.broadcasted_iota(jnp.int32, (T, HALF), 0))
        c, s = _rope_tables((j * T + tok).astype(_F32), inv_ref[...])
        cos_s[pl.ds(pl.multiple_of(j * T, T), T), :] = c
        sin_s[pl.ds(pl.multiple_of(j * T, T), T), :] = s

    @pl.when(j == 0)
    def _init_state():
        o_rgh[...] = jnp.zeros_like(o_rgh)
        o_C[...] = jnp.zeros_like(o_C)
        o_n[...] = jnp.zeros_like(o_n)
        o_S[...] = jnp.zeros_like(o_S)
        m_s[...] = jnp.zeros_like(m_s)
        cc_rg[...] = jnp.zeros_like(cc_rg)
        cc_ml[...] = jnp.zeros_like(cc_ml)

    x = x_ref[0]
    shift = mod_ref[0, :, 0:D_MODEL]
    scale = mod_ref[0, :, D_MODEL:2 * D_MODEL]
    gate = mod_ref[0, :, 2 * D_MODEL:3 * D_MODEL]
    u = _rms(x) * (vec_ref[V_NPRE:V_NPRE + 1, :] * (1.0 + scale)) + shift
    ub = u.astype(_BF16)
    n_chunk = D_MODEL // MXU_N
    n_wide = D_MODEL // W_CHUNK

    pending = [(P_XML, w) for w in range(n_wide)]
    pending += [(blk, w) for w in range(n_wide) for blk in (P_XRG, P_ZRG, P_GA)]
    pending += [(blk, w) for w in range(n_wide) for blk in (P_ZML, P_GB)]
    pending += [(blk, w) for w in range(n_wide) for blk in (P_QR, P_KR, P_VR, P_GR, P_GC)]
    ready = {}

    def prefetch(n):
        for _ in range(min(n, len(pending))):
            blk, w = pending.pop(0)
            ready[(blk, w)] = _dot(ub, win_ref[blk * n_wide + w])

    def proj(blk, c):
        per_wide = W_CHUNK // MXU_N
        key = (blk, c // per_wide)
        while key not in ready:
            prefetch(1)
        lo = (c % per_wide) * MXU_N
        v = ready[key][:, lo:lo + MXU_N]
        return 0.5 * v if blk in P_GATE_LIKE else v

    prefetch(PREFETCH_START)
    for h in range(N_HEADS):
        cs = slice(h * DH, (h + 1) * DH)
        x_ml = proj(P_XML, h)
        prefetch(PREFETCH_B1[h])
        xc_ml = _silu(_shifted_conv(x_ml, cs, cc_ml, vec_ref, V_MLCW, V_MLCB, o_mlc))
        xcml_s[:, cs] = xc_ml
        xhb = xc_ml.astype(_BF16)
        prefetch(1)
        q = _dot(xhb, wq_ref[h]).astype(_BF16)
        k = (_dot(xhb, wk_ref[h]) * (DH ** -0.5)).astype(_BF16)
        v = _dot(x_ml.astype(_BF16), wv_ref[h]).astype(_BF16)
        qkv_s[:, cs] = q
        qkv_s[:, D_MODEL + h * DH:D_MODEL + (h + 1) * DH] = k
        qkv_s[:, 2 * D_MODEL + h * DH:2 * D_MODEL + (h + 1) * DH] = v
    mask = mask_s[...]

    def gates_sums():
        gpre = jnp.concatenate([_dot(qkv_s[0:T // 2, :], wif_ref[...]),
                                _dot(qkv_s[T // 2:T, :], wif_ref[...])], axis=0) + bif_ref[...]
        lf = _log_sigmoid(gpre[:, LANES:])
        prefetch(1)
        F = sum(_dot(tri_s[...], part) for part in _split3_bf16(lf))
        uu = gpre[:, :LANES] - F
        return F, uu, uu.T

    def gates_running_max(uu_rows):
        lane = lax.broadcasted_iota(jnp.int32, (T, LANES), 1)
        cmax = jnp.zeros((T, LANES), _F32)
        for h in range(N_HEADS):
            cm_h = jnp.max(uu_rows[h:h + 1, :] + mask, axis=1, keepdims=True)
            cmax = jnp.where(lane == h, cm_h, cmax)
        return cmax

    def gates_columns(F, uu, cmax):
        m0 = m_s[0:1, :]
        mrun = jnp.maximum(m0, cmax)
        m = F + mrun
        mL = m[T - 1:T, :]
        FL = F[T - 1:T, :]
        m_s[...] = jnp.broadcast_to(mL, m_s.shape)
        o_m[0] = m_s[...]
        return (-mrun,
                jnp.exp(m0 - mrun),
                jnp.exp(-m),
                jnp.exp(FL - mL + uu),
                jnp.broadcast_to(jnp.exp(FL + m0 - mL), (T, LANES)))

    prefetch(PREFETCH_GATES)
    F, uu, uu_rows = gates_sums()
    for c in range(n_chunk):
        cs = slice(c * MXU_N, (c + 1) * MXU_N)
        x_rg = proj(P_XRG, c)
        prefetch(2)
        xc = _shifted_conv(x_rg, cs, cc_rg, vec_ref, V_RGCW, V_RGCB, o_rgc)
        prefetch(1)
        a, bt = _rg_gates(xc, vec_ref, wax_ref, c * MXU_N)
        prefetch(PREFETCH_A - 3)
        h_rg, h_last = _segment_scan(a, bt, o_rgh[0, :, cs])
        o_rgh[0, :, cs] = h_last
        mix_s[:, cs] = _sigmoid_of_twice(proj(P_GA, c)) * (h_rg * _silu_of_twice(proj(P_ZRG, c)))
        if c == 0:
            cmax = gates_running_max(uu_rows)
        if c == 1:
            c_row, c_inter, c_floor, c_wl, c_decay = gates_columns(F, uu, cmax)

    for h in range(N_HEADS):
        cs = slice(h * DH, (h + 1) * DH)
        prefetch(1)
        qh = qkv_s[:, h * DH:(h + 1) * DH]
        kh = qkv_s[:, D_MODEL + h * DH:D_MODEL + (h + 1) * DH]
        vh = qkv_s[:, 2 * D_MODEL + h * DH:2 * D_MODEL + (h + 1) * DH]
        col = lambda c: c[:, h:h + 1]
        dmat = jnp.exp(col(c_row) + (uu_rows[h:h + 1, :] + mask))
        s = lax.dot_general(qh, kh, _NT, preferred_element_type=_F32) * dmat
        C0 = o_C[0, h]
        n0 = o_n[0, h:h + 1, :]
        prefetch(1)
        num = col(c_inter) * _dot(qh, C0.astype(_BF16)) + _dot(s.astype(_BF16), vh)
        den = (col(c_inter) * jnp.sum(qh.astype(_F32) * n0, axis=-1, keepdims=True)
               + jnp.sum(s, axis=-1, keepdims=True))
        h_ml = num * (1.0 / jnp.maximum(jnp.abs(den), col(c_floor)))
        kw = kh.astype(_F32) * col(c_wl)
        prefetch(PREFETCH_B2 - 2)
        o_C[0, h] = col(c_decay) * C0 + lax.dot_general(
            kw.astype(_BF16), vh, _TN, preferred_element_type=_F32)
        o_n[0, h:h + 1, :] = col(c_decay)[0:1, :] * n0 + jnp.sum(kw, axis=0, keepdims=True)
        y_b = (_headnorm(h_ml) * vec_ref[V_MLGN:V_MLGN + 1, cs]
               + vec_ref[V_MLSKIP:V_MLSKIP + 1, cs] * xcml_s[:, cs])
        mix_s[:, cs] += _sigmoid_of_twice(proj(P_GB, h)) * (y_b * _silu_of_twice(proj(P_ZML, h)))

    t0 = pl.multiple_of(j * T, T)
    cos = cos_s[pl.ds(t0, T), :]
    sin = sin_s[pl.ds(t0, T), :]

    def rope(xh):
        x1, x2 = xh[:, :HALF], xh[:, HALF:]
        return jnp.concatenate([x1 * cos - x2 * sin, x1 * sin + x2 * cos], axis=1)

    for h in range(N_HEADS):
        cs = slice(h * DH, (h + 1) * DH)
        qh = rope(proj(P_QR, h)).astype(_BF16)
        kf = rope(proj(P_KR, h)) * (DH ** -0.5)
        vh = proj(P_VR, h).astype(_BF16)
        prefetch(1)
        s = lax.dot_general(qh, kf.astype(_BF16), _NT, preferred_element_type=_F32) * dret_s[h]
        S0 = o_S[0, h]
        prefetch(PREFETCH_C - 1)
        h_ret = _dot(s.astype(_BF16), vh) + _dot(qh, S0.astype(_BF16)) * cross_s[h]
        kw = (kf * wk_s[h]).astype(_BF16)
        o_S[0, h] = math.exp(T * _RET_LOG_GAMMA[h]) * S0 + lax.dot_general(
            kw, vh, _TN, preferred_element_type=_F32)
        y_c = _headnorm(h_ret) * vec_ref[V_RETGN:V_RETGN + 1, cs] * _silu_of_twice(proj(P_GR, h))
        mix_s[:, cs] += _sigmoid_of_twice(proj(P_GC, h)) * y_c

    out = _dot_chunks(mix_s[...].astype(_BF16), wout_ref, 0, n_wide)
    y_ref[0] = x + gate * (_rms(out) * vec_ref[V_NPOST:V_NPOST + 1, :])


def _prompt_layer(layer, prev_big, mod_row0, x, mod, vec, bif2, inv, win, wax, wq, wk, wv, wif2, wout):
    B, L, _ = x.shape
    T = T_TILE
    per_b3 = lambda b, j: (b, 0, 0)
    big_blk = pl.BlockSpec((None, 1, N_HEADS, DH, DH), lambda b, j: (layer, b, 0, 0, 0))
    big_shape = jax.ShapeDtypeStruct((DEPTH, B, N_HEADS, DH, DH), _F32)

    def resident(*shape):
        return pl.BlockSpec((None,) + shape, lambda b, j: (layer,) + (0,) * len(shape),
                            pipeline_mode=pl.Buffered(1))

    n_in = 12
    extra_specs = [pl.BlockSpec(memory_space=pl.ANY)] * len(prev_big)
    aliases = {n_in: 3, n_in + 1: 7} if prev_big else {}
    tail = (CONV_W - 1) * SUBLANES
    outs = pl.pallas_call(
        _prompt_kernel,
        grid=(B, L // T),
        input_output_aliases=aliases,
        in_specs=[
            pl.BlockSpec((1, T, D_MODEL), lambda b, j: (b, j, 0)),
            pl.BlockSpec((None, 1, 1, 3 * D_MODEL), lambda b, j: (layer, mod_row0 + b, 0, 0)),
            resident(V_ROWS, D_MODEL),
            resident(1, 2 * LANES),
            pl.BlockSpec((1, HALF), lambda b, j: (0, 0), pipeline_mode=pl.Buffered(1)),
            resident(D_IN // W_CHUNK, D_MODEL, W_CHUNK),
            resident(RG_HEADS, RG_BW, 2 * RG_BW),
            resident(N_HEADS, DH, DH),
            resident(N_HEADS, DH, DH),
            resident(N_HEADS, DH, DH),
            resident(3 * D_MODEL, 2 * LANES),
            resident(D_MODEL // W_CHUNK, D_MODEL, W_CHUNK),
        ] + extra_specs,
        out_specs=[
            pl.BlockSpec((1, T, D_MODEL), lambda b, j: (b, j, 0)),
            pl.BlockSpec((1, 1, D_MODEL), per_b3),
            pl.BlockSpec((1, CONV_W - 1, D_MODEL), per_b3),
            big_blk,
            pl.BlockSpec((1, N_HEADS, DH), per_b3),
            pl.BlockSpec((1, SUBLANES, LANES), per_b3),
            pl.BlockSpec((1, CONV_W - 1, D_MODEL), per_b3),
            big_blk,
        ],
        out_shape=[
            jax.ShapeDtypeStruct((B, L, D_MODEL), _F32),
            jax.ShapeDtypeStruct((B, 1, D_MODEL), _F32),
            jax.ShapeDtypeStruct((B, CONV_W - 1, D_MODEL), _F32),
            big_shape,
            jax.ShapeDtypeStruct((B, N_HEADS, DH), _F32),
            jax.ShapeDtypeStruct((B, SUBLANES, LANES), _F32),
            jax.ShapeDtypeStruct((B, CONV_W - 1, D_MODEL), _F32),
            big_shape,
        ],
        scratch_shapes=[
            pltpu.VMEM((tail, D_MODEL), _F32),
            pltpu.VMEM((tail, D_MODEL), _F32),
            pltpu.VMEM((T, 3 * D_MODEL), _BF16),
            pltpu.VMEM((T, D_MODEL), _F32),
            pltpu.VMEM((T, D_MODEL), _F32),
            pltpu.VMEM((SUBLANES, LANES), _F32),
            pltpu.VMEM((L, HALF), _F32),
            pltpu.VMEM((L, HALF), _F32),
            pltpu.VMEM((T, T), _F32),
            pltpu.VMEM((T, T), _BF16),
            pltpu.VMEM((N_HEADS, T, T), _F32),
            pltpu.VMEM((N_HEADS, T, DH), _F32),
            pltpu.VMEM((N_HEADS, T, DH), _F32),
        ],
        compiler_params=pltpu.CompilerParams(
            dimension_semantics=("arbitrary", "arbitrary"),
            vmem_limit_bytes=VMEM_LIMIT),
        name="prompt_layer",
    )(x, mod, vec, bif2, inv, win, wax, wq, wk, wv, wif2, wout, *prev_big)
    y, rgh, rgc, mlC, mln, mlm, mlc, retS = outs
    return y, (rgh[:, 0], rgc, mln, mlm[:, 0, :N_HEADS], mlc), (mlC, retS)


def _interleave_tiles(x):
    B, L, D = x.shape
    return x.reshape(B, L // T_TILE, SUBLANES, SEG, D).transpose(0, 1, 3, 2, 4).reshape(B, L, D)


def _deinterleave_tiles(x):
    B, L, D = x.shape
    return x.reshape(B, L // T_TILE, SEG, SUBLANES, D).transpose(0, 1, 3, 2, 4).reshape(B, L, D)


def _sample_proj_kernel(x_ref, mod_ref, vec_ref, w_ref, o_ref):
    scale = mod_ref[:, D_MODEL:2 * D_MODEL]
    u = _rms(x_ref[...]) * (vec_ref[V_NPRE:V_NPRE + 1, :] * (1.0 + scale)) + mod_ref[:, 0:D_MODEL]
    o_ref[...] = _dot_chunks(u.astype(_BF16), w_ref, 0, w_ref.shape[0])


def _sample_proj(layer, x, mod, vec, win):
    rows = x.shape[0]
    n_steps = D_MODEL // W_CHUNK
    return pl.pallas_call(
        _sample_proj_kernel,
        grid=(n_steps,),
        in_specs=[
            pl.BlockSpec((rows, D_MODEL), lambda n: (0, 0)),
            pl.BlockSpec((None, rows, 3 * D_MODEL), lambda n: (layer, 0, 0)),
            pl.BlockSpec((None, V_ROWS, D_MODEL), lambda n: (layer, 0, 0)),
            pl.BlockSpec((None, N_PROJ, D_MODEL, W_CHUNK), lambda n: (layer, n, 0, 0)),
        ],
        out_specs=pl.BlockSpec((rows, N_PROJ * W_CHUNK), lambda n: (0, n)),
        out_shape=jax.ShapeDtypeStruct((rows, D_IN), _F32),
        compiler_params=pltpu.CompilerParams(vmem_limit_bytes=VMEM_LIMIT),
        name="sample_proj",
    )(x, mod, vec, win)


S_BT = 2


def _bcast_heads(cols):
    rows = cols[0].shape[0]
    return jnp.concatenate([jnp.broadcast_to(c, (rows, DH)) for c in cols], axis=1)


def _sample_mix_kernel(proj_ref, x_ref, mod_ref, vec_ref, bif_ref, inv_ref,
                       wax_ref, wq_ref, wk_ref, wv_ref, wif_ref, wout_ref,
                       rgh_ref, rgc_ref, mlc_ref, mln_ref, mlm_ref, C_ref, S_ref, *rest):
    (y_ref, o_rgh, o_rgc, o_mlc, o_mln, o_mlm, o_C, o_S,
     mix_s, xcml_s, inter_s, dv_s, sv_s, rden_s, svr_s, vr_s, qC_s, qS_s,
     qT_s, kT_s, qrT_s, krT_s) = rest[-22:]
    i = pl.program_id(0)
    n_steps = pl.num_programs(0)
    R = x_ref.shape[0]

    def p(blk):
        v = proj_ref[:, blk * D_MODEL:(blk + 1) * D_MODEL]
        return 0.5 * v if blk in P_GATE_LIKE else v

    def conv(xin, cs_ref, o_state, w0, brow):
        cs = cs_ref[...]
        acc = vec_ref[brow:brow + 1, :] + vec_ref[w0 + 3:w0 + 4, :] * xin
        for k in range(CONV_W - 1):
            acc = acc + vec_ref[w0 + k:w0 + k + 1, :] * cs[:, k * D_MODEL:(k + 1) * D_MODEL]
        o_state[...] = jnp.concatenate([cs[:, D_MODEL:], xin], axis=1)
        return acc

    @pl.when(i == 0)
    def _dense():
        xc = conv(p(P_XRG), rgc_ref, o_rgc, V_RGCW, V_RGCB)
        a, bt = _rg_gates(xc, vec_ref, wax_ref)
        h = a * rgh_ref[...] + bt
        o_rgh[...] = h
        mix_s[...] = _sigmoid_of_twice(p(P_GA)) * (h * _silu_of_twice(p(P_ZRG)))

        x_ml = p(P_XML)
        xc_ml = _silu(conv(x_ml, mlc_ref, o_mlc, V_MLCW, V_MLCB))
        xcml_s[...] = xc_ml
        xhb = xc_ml.astype(_BF16)
        xmb = x_ml.astype(_BF16)
        qs, ks, vs = [], [], []
        for h_ in range(N_HEADS):
            sl = slice(h_ * DH, (h_ + 1) * DH)
            qs.append(_dot(xhb[:, sl], wq_ref[h_]))
            ks.append(_dot(xhb[:, sl], wk_ref[h_]) * (DH ** -0.5))
            vs.append(_dot(xmb[:, sl], wv_ref[h_]))
        q = jnp.concatenate(qs, axis=1)
        k = jnp.concatenate(ks, axis=1)
        v = jnp.concatenate(vs, axis=1)
        qkvb = jnp.concatenate([q, k, v], axis=1).astype(_BF16)
        gpre = _dot(qkvb, wif_ref[...]) + bif_ref[...]
        ig = gpre[:, :LANES]
        lf = _log_sigmoid(gpre[:, LANES:])
        m0 = mlm_ref[...]
        m = jnp.maximum(m0 + lf, ig)
        o_mlm[...] = m
        dg = jnp.exp(ig - m)
        inter = jnp.exp(lf + m0 - m)
        floor = jnp.exp(-m)
        n0 = mln_ref[...]
        s_cols, rden_cols = [], []
        for h_ in range(N_HEADS):
            sl = slice(h_ * DH, (h_ + 1) * DH)
            s_h = jnp.sum(q[:, sl] * k[:, sl], axis=-1, keepdims=True) * dg[:, h_:h_ + 1]
            qn = jnp.sum(q[:, sl] * n0[:, sl], axis=-1, keepdims=True)
            den = inter[:, h_:h_ + 1] * qn + s_h
            s_cols.append(s_h)
            rden_cols.append(1.0 / jnp.maximum(jnp.abs(den), floor[:, h_:h_ + 1]))
        dg_b = _bcast_heads([dg[:, h_:h_ + 1] for h_ in range(N_HEADS)])
        inter_b = _bcast_heads([inter[:, h_:h_ + 1] for h_ in range(N_HEADS)])
        inter_s[...] = inter_b
        dv_s[...] = dg_b * v
        sv_s[...] = _bcast_heads(s_cols) * v
        rden_s[...] = _bcast_heads(rden_cols)
        o_mln[...] = inter_b * n0 + dg_b * k
        qT_s[...] = q.T
        kT_s[...] = k.T

        pos = jnp.full((1, HALF), float(PAST_LEN), _F32)
        cos, sin = _rope_tables(pos, inv_ref[...])
        qr = jnp.concatenate(_rope(p(P_QR), cos, sin), axis=1)
        kr = jnp.concatenate(_rope(p(P_KR), cos, sin, DH ** -0.5), axis=1)
        v_r = p(P_VR)
        s_cols = [jnp.sum(qr[:, h_ * DH:(h_ + 1) * DH] * kr[:, h_ * DH:(h_ + 1) * DH],
                          axis=-1, keepdims=True) for h_ in range(N_HEADS)]
        svr_s[...] = _bcast_heads(s_cols) * v_r
        vr_s[...] = v_r
        qrT_s[...] = qr.T
        krT_s[...] = kr.T

    lane = lax.broadcasted_iota(jnp.int32, (DH, R), 1)
    for bi in range(S_BT):
        b = i * S_BT + bi
        onehot = (lane == b).astype(_F32)

        def column(t_ref, h_):
            return jnp.sum(t_ref[h_ * DH:(h_ + 1) * DH, :] * onehot, axis=1, keepdims=True)

        for h_ in range(N_HEADS):
            sl = slice(h_ * DH, (h_ + 1) * DH)
            C0 = C_ref[bi, h_]
            qC_s[pl.ds(b, 1), sl] = jnp.sum(C0 * column(qT_s, h_), axis=0, keepdims=True)
            o_C[bi, h_] = (inter_s[pl.ds(b, 1), sl] * C0
                           + column(kT_s, h_) * dv_s[pl.ds(b, 1), sl])
            S0 = S_ref[bi, h_]
            qS_s[pl.ds(b, 1), sl] = jnp.sum(S0 * column(qrT_s, h_), axis=0, keepdims=True)
            o_S[bi, h_] = (math.exp(_RET_LOG_GAMMA[h_]) * S0
                           + column(krT_s, h_) * vr_s[pl.ds(b, 1), sl])

    @pl.when(i == n_steps - 1)
    def _epilogue():
        h_ml = (inter_s[...] * qC_s[...] + sv_s[...]) * rden_s[...]
        hn = jnp.concatenate(
            [_headnorm(h_ml[:, h_ * DH:(h_ + 1) * DH]) for h_ in range(N_HEADS)], axis=1)
        y_b = hn * vec_ref[V_MLGN:V_MLGN + 1, :] + vec_ref[V_MLSKIP:V_MLSKIP + 1, :] * xcml_s[...]
        mix = mix_s[...] + _sigmoid_of_twice(p(P_GB)) * (y_b * _silu_of_twice(p(P_ZML)))
        cross = jnp.concatenate(
            [jnp.full((1, DH), math.exp(_RET_LOG_GAMMA[h_]), _F32) for h_ in range(N_HEADS)], axis=1)
        h_ret = svr_s[...] + qS_s[...] * cross
        hn = jnp.concatenate(
            [_headnorm(h_ret[:, h_ * DH:(h_ + 1) * DH]) for h_ in range(N_HEADS)], axis=1)
        mix = mix + _sigmoid_of_twice(p(P_GC)) * (
            hn * vec_ref[V_RETGN:V_RETGN + 1, :] * _silu_of_twice(p(P_GR)))
        out = _dot_chunks(mix.astype(_BF16), wout_ref, 0, D_MODEL // W_CHUNK)
        gate = mod_ref[:, 2 * D_MODEL:3 * D_MODEL]
        y_ref[...] = x_ref[...] + gate * (_rms(out) * vec_ref[V_NPOST:V_NPOST + 1, :])


def _sample_mix(layer, prev_big, proj, x, mod, vec, bif2, inv, wax, wq, wk, wv, wif2, wout,
                rgh, rgc, mlc, mln, mlm, mlC_all, retS_all):
    R = x.shape[0]
    c2 = lambda i: (0, 0)
    st5 = lambda i: (layer, i, 0, 0, 0)

    def resident(*shape):
        return pl.BlockSpec((None,) + shape, lambda i: (layer,) + (0,) * len(shape),
                            pipeline_mode=pl.Buffered(1))

    def resident2(*shape):
        return pl.BlockSpec(shape, c2, pipeline_mode=pl.Buffered(1))

    row_blk = pl.BlockSpec((R, D_MODEL), c2)
    conv_blk = pl.BlockSpec((R, (CONV_W - 1) * D_MODEL), c2)
    lane_blk = pl.BlockSpec((R, LANES), c2)
    state_blk = pl.BlockSpec((None, S_BT, N_HEADS, DH, DH), st5)
    f32 = lambda *s: jax.ShapeDtypeStruct(s, _F32)
    n_in = 19
    outs = pl.pallas_call(
        _sample_mix_kernel,
        grid=(R // S_BT,),
        input_output_aliases={n_in: 6, n_in + 1: 7} if prev_big else {},
        in_specs=[
            resident2(R, D_IN),
            resident2(R, D_MODEL),
            resident(R, 3 * D_MODEL),
            resident(V_ROWS, D_MODEL),
            resident(1, 2 * LANES),
            resident2(1, HALF),
            resident(RG_HEADS, RG_BW, 2 * RG_BW),
            resident(N_HEADS, DH, DH),
            resident(N_HEADS, DH, DH),
            resident(N_HEADS, DH, DH),
            resident(3 * D_MODEL, 2 * LANES),
            resident(D_MODEL // W_CHUNK, D_MODEL, W_CHUNK),
            resident(R, D_MODEL),
            resident(R, (CONV_W - 1) * D_MODEL),
            resident(R, (CONV_W - 1) * D_MODEL),
            resident(R, D_MODEL),
            resident(R, LANES),
            state_blk,
            state_blk,
        ] + [pl.BlockSpec(memory_space=pl.ANY)] * len(prev_big),
        out_specs=[row_blk, row_blk, conv_blk, conv_blk, row_blk, lane_blk, state_blk, state_blk],
        out_shape=[
            f32(R, D_MODEL), f32(R, D_MODEL), f32(R, (CONV_W - 1) * D_MODEL),
            f32(R, (CONV_W - 1) * D_MODEL), f32(R, D_MODEL), f32(R, LANES),
            f32(DEPTH, R, N_HEADS, DH, DH), f32(DEPTH, R, N_HEADS, DH, DH),
        ],
        scratch_shapes=(
            [pltpu.VMEM((R, D_MODEL), _F32) for _ in range(10)]
            + [pltpu.VMEM((D_MODEL, R), _F32) for _ in range(4)]),
        compiler_params=pltpu.CompilerParams(
            dimension_semantics=("arbitrary",), vmem_limit_bytes=VMEM_LIMIT),
        name="sample_mix",
    )(proj, x, mod, vec, bif2, inv, wax, wq, wk, wv, wif2, wout,
      rgh, rgc, mlc, mln, mlm, mlC_all, retS_all, *prev_big)
    y, o_rgh, o_rgc, o_mlc, o_mln, o_mlm, o_C, o_S = outs
    return y, (o_rgh, o_rgc.reshape(R, CONV_W - 1, D_MODEL),
               o_mln.reshape(R, N_HEADS, DH), o_mlm[:, :N_HEADS],
               o_mlc.reshape(R, CONV_W - 1, D_MODEL)), (o_C, o_S)


def _pack_params(norm_pre, norm_post, rg_conv_w, rg_conv_b, rg_b_a, rg_b_x, rg_lambda,
                 ml_conv_w, ml_conv_b, ml_skip, ml_gn, ret_gn):
    row = lambda p: p[:, None, :]
    rows = [row(norm_pre), row(norm_post), rg_conv_w, row(rg_conv_b), row(0.5 * rg_b_a), row(0.5 * rg_b_x),
            row(rg_lambda), ml_conv_w, row(ml_conv_b), row(ml_skip), row(ml_gn), row(ret_gn)]
    vec = jnp.concatenate(rows, axis=1)
    return jnp.pad(vec, ((0, 0), (0, V_ROWS - vec.shape[1]), (0, 0)))


def _column_chunks(w):
    depth, K, N = w.shape
    return w.astype(_BF16).reshape(depth, K, N // W_CHUNK, W_CHUNK).transpose(0, 2, 1, 3)


def kernel(x_prompt, x_sample, c_prompt, c_sample, state_rg_h, state_rg_conv, state_ml_C, state_ml_n, state_ml_m, state_ml_conv, state_ret_S, w_ada, b_ada, norm_pre, norm_post, w_in, rg_conv_w, rg_conv_b, rg_w_a, rg_b_a, rg_w_x, rg_b_x, rg_lambda, ml_conv_w, ml_conv_b, ml_w_q, ml_w_k, ml_w_v, ml_w_if, ml_b_if, ml_skip, ml_gn, ret_gn, w_out):
    xs = x_sample[:, 0, :]
    R = xs.shape[0]
    mod_all = _ada_call(jnp.concatenate([c_sample, c_prompt], axis=0), w_ada, b_ada)
    mod_rows = mod_all.reshape(DEPTH, mod_all.shape[1], 1, 3 * D_MODEL)
    inv = (ROPE_BASE ** (-jnp.linspace(0.0, 1.0, HALF, dtype=_F32)))[None, :]
    vec = _pack_params(norm_pre, norm_post, rg_conv_w, rg_conv_b, rg_b_a, rg_b_x,
                       rg_lambda, ml_conv_w, ml_conv_b, ml_skip, ml_gn, ret_gn)
    win = _column_chunks(w_in)
    wout = _column_chunks(w_out)
    wax = (0.5 * jnp.concatenate([rg_w_a, rg_w_x], axis=-1)).astype(_BF16)
    wq = ml_w_q.astype(_BF16)
    wk = ml_w_k.astype(_BF16)
    wv = ml_w_v.astype(_BF16)
    lane_pad = ((0, 0), (0, 0), (0, LANES - N_HEADS))
    wif2 = jnp.concatenate([jnp.pad(ml_w_if[..., :N_HEADS], lane_pad),
                            jnp.pad(ml_w_if[..., N_HEADS:], lane_pad)], axis=-1).astype(_BF16)
    b_if = ml_b_if[:, None, :]
    bif2 = jnp.concatenate([jnp.pad(b_if[..., :N_HEADS], lane_pad),
                            jnp.pad(b_if[..., N_HEADS:], lane_pad)], axis=-1)
    s_rgc_in = state_rg_conv.reshape(DEPTH, R, -1)
    s_mlc_in = state_ml_conv.reshape(DEPTH, R, -1)
    s_mln_in = state_ml_n.reshape(DEPTH, R, -1)
    s_mlm_in = jnp.pad(state_ml_m, lane_pad)

    xp = _interleave_tiles(x_prompt)
    st_p, st_s = [], []
    big_p, big_s = (), ()
    for l in range(DEPTH):
        xp, sp, big_p = _prompt_layer(l, big_p, R, xp, mod_rows, vec, bif2, inv, win, wax, wq, wk, wv, wif2, wout)
        st_p.append(sp)
        proj = _sample_proj(l, xs, mod_all, vec, win)
        xs, ss, big_s = _sample_mix(
            l, big_s, proj, xs, mod_all, vec, bif2, inv, wax, wq, wk, wv, wif2, wout,
            state_rg_h, s_rgc_in, s_mlc_in, s_mln_in, s_mlm_in, state_ml_C, state_ret_S)
        st_s.append(ss)
    p_rgh, p_rgc, p_n, p_m, p_mlc = [jnp.stack(a) for a in zip(*st_p)]
    s_rgh, s_rgc, s_n, s_m, s_mlc = [jnp.stack(a) for a in zip(*st_s)]
    return (_deinterleave_tiles(xp), xs[:, None, :], p_rgh, p_rgc, big_p[0], p_n, p_m, p_mlc, big_p[1],
            s_rgh, s_rgc, big_s[0], s_n, s_m, s_mlc, big_s[1])
```

```python
import math

import jax
import jax.numpy as jnp
from jax import lax
from jax.experimental import pallas as pl
from jax.experimental.pallas import tpu as pltpu

D_MODEL = 1024
DEPTH = 2
CONV_W = 4
RG_HEADS = 8
RG_BW = D_MODEL // RG_HEADS
RG_C = 8.0
N_HEADS = 4
DH = D_MODEL // N_HEADS
HALF = DH // 2
PAST_LEN = 16384
ROPE_BASE = 10000.0
EPS = 1e-6
N_PROJ = 11
D_IN = N_PROJ * D_MODEL
P_XRG, P_ZRG, P_XML, P_ZML, P_QR, P_KR, P_VR, P_GR, P_GA, P_GB, P_GC = range(N_PROJ)
P_GATE_LIKE = (P_ZRG, P_ZML, P_GR, P_GA, P_GB, P_GC)

V_NPRE, V_NPOST, V_RGCW, V_RGCB, V_RGBA, V_RGBX, V_RGLAM = 0, 1, 2, 6, 7, 8, 9
V_MLCW, V_MLCB, V_MLSKIP, V_MLGN, V_RETGN, V_ROWS = 10, 14, 15, 16, 17, 24

SUBLANES = 8
LANES = 128
MXU_N = 256
W_CHUNK = MXU_N
T_TILE = 256
SEG = T_TILE // SUBLANES
N_GROUPS = T_TILE // SUBLANES
VMEM_LIMIT = 60 * 1024 * 1024
PREFETCH_START = 1
PREFETCH_B1 = (1, 1, 1, 1)
PREFETCH_GATES = 1
PREFETCH_A = 4
PREFETCH_B2 = 3
PREFETCH_C = 2

_RET_LOG_GAMMA = tuple(math.log1p(-2.0 ** (-5.0 - h)) for h in range(N_HEADS))

_NT = (((1,), (1,)), ((), ()))
_TN = (((0,), (0,)), ((), ()))
_F32 = jnp.float32
_BF16 = jnp.bfloat16


def _dot(a, b):
    return jnp.dot(a, b, preferred_element_type=_F32)


def _dot_chunks(a, w_ref, first, n):
    return jnp.concatenate([_dot(a, w_ref[first + c]) for c in range(n)], axis=1)


def _sigmoid_of_twice(hx):
    return 0.5 * jnp.tanh(hx) + 0.5


def _silu_of_twice(hx):
    return hx * jnp.tanh(hx) + hx


def _silu(x):
    return _silu_of_twice(0.5 * x)


def _softplus(x):
    return jnp.maximum(x, 0.0) + jnp.log1p(jnp.exp(-jnp.abs(x)))


def _log_sigmoid(x):
    return -_softplus(-x)


def _neg_expm1_2x(x):
    t = jnp.tanh(x)
    return (-2.0 * t) / (1.0 - t)


def _rms(x):
    return x * lax.rsqrt(jnp.mean(x * x, axis=-1, keepdims=True) + EPS)


def _headnorm(h):
    mu = jnp.mean(h, axis=-1, keepdims=True)
    c = h - mu
    var = jnp.mean(c * c, axis=-1, keepdims=True)
    return c * lax.rsqrt(var + EPS)


def _rg_gates(xc, vec_ref, wax_ref, lane0=0):
    xcb = xc.astype(_BF16)
    a_parts, b_parts = [], []
    for hh in range(xc.shape[1] // RG_BW):
        loc = slice(hh * RG_BW, (hh + 1) * RG_BW)
        sl = slice(lane0 + hh * RG_BW, lane0 + (hh + 1) * RG_BW)
        g = _dot(xcb[:, loc], wax_ref[sl.start // RG_BW])
        r = _sigmoid_of_twice(g[:, :RG_BW] + vec_ref[V_RGBA:V_RGBA + 1, sl])
        i = _sigmoid_of_twice(g[:, RG_BW:] + vec_ref[V_RGBX:V_RGBX + 1, sl])
        log_a = (-RG_C) * r * _softplus(-vec_ref[V_RGLAM:V_RGLAM + 1, sl])
        a_parts.append(jnp.exp(log_a))
        b_parts.append(jnp.sqrt(_neg_expm1_2x(log_a)) * (i * xc[:, loc]))
    return jnp.concatenate(a_parts, axis=1), jnp.concatenate(b_parts, axis=1)


def _rope_tables(pos, inv_row):
    ang = pos * inv_row
    return jnp.cos(ang), jnp.sin(ang)


def _rope(x, cos, sin, scale=None):
    out = []
    for h in range(N_HEADS):
        x1 = x[:, h * DH:h * DH + HALF]
        x2 = x[:, h * DH + HALF:(h + 1) * DH]
        r = jnp.concatenate([x1 * cos - x2 * sin, x1 * sin + x2 * cos], axis=1)
        out.append(r if scale is None else r * scale)
    return out


def _token_of_row(i):
    return (i & (SUBLANES - 1)) * SEG + (i >> 3)


def _split3_bf16(x):
    p0 = x.astype(_BF16)
    r = x - p0.astype(_F32)
    p1 = r.astype(_BF16)
    p2 = (r - p1.astype(_F32)).astype(_BF16)
    return p0, p1, p2


def _ada_kernel(c_ref, w_ref, b_ref, o_ref):
    c = c_ref[...]
    w = w_ref[...]
    c_hi = c.astype(_BF16)
    c_lo = (c - c_hi.astype(_F32)).astype(_BF16)
    w_hi = w.astype(_BF16)
    o_ref[...] = _dot(c_hi, w_hi) + _dot(c_lo, w_hi) + b_ref[...]


def _ada_call(c_all, w_ada, b_ada):
    rows = c_all.shape[0]
    return pl.pallas_call(
        _ada_kernel,
        grid=(DEPTH, 3),
        in_specs=[
            pl.BlockSpec((rows, D_MODEL), lambda l, n: (0, 0)),
            pl.BlockSpec((None, D_MODEL, D_MODEL), lambda l, n: (l, 0, n)),
            pl.BlockSpec((None, 1, D_MODEL), lambda l, n: (l, 0, n)),
        ],
        out_specs=pl.BlockSpec((None, rows, D_MODEL), lambda l, n: (l, 0, n)),
        out_shape=jax.ShapeDtypeStruct((DEPTH, rows, 3 * D_MODEL), _F32),
        name="adaln",
    )(c_all, w_ada, b_ada.reshape(DEPTH, 1, 3 * D_MODEL))


def _shifted_conv(xin, cs, carry_ref, vec_ref, w0, brow, o_state):
    T = T_TILE
    tail = (CONV_W - 1) * SUBLANES
    prev = carry_ref[:, cs]
    cur = xin[T - tail:T, :]
    sub = lax.broadcasted_iota(jnp.int32, (SUBLANES, xin.shape[1]), 0)
    lead = []
    for q in range(CONV_W - 1):
        sl = slice(q * SUBLANES, (q + 1) * SUBLANES)
        lead.append(jnp.where(sub == 0, pltpu.roll(prev[sl], 1, axis=0), pltpu.roll(cur[sl], 1, axis=0)))
    ext = jnp.concatenate(lead + [xin], axis=0)
    acc = vec_ref[brow:brow + 1, cs] + vec_ref[w0 + 3:w0 + 4, cs] * xin
    for d in range(1, CONV_W):
        acc = acc + vec_ref[w0 + 3 - d:w0 + 4 - d, cs] * ext[tail - SUBLANES * d:tail - SUBLANES * d + T, :]
    carry_ref[:, cs] = cur
    o_state[0, :, cs] = jnp.concatenate(
        [cur[q * SUBLANES + SUBLANES - 1:(q + 1) * SUBLANES, :] for q in range(CONV_W - 1)], axis=0)
    return acc


def _segment_scan(a, bt, h0):
    hs, cums = [], []
    h = cum = None
    for r in range(N_GROUPS):
        sl = slice(r * SUBLANES, (r + 1) * SUBLANES)
        if r == 0:
            h, cum = bt[sl], a[sl]
        else:
            h, cum = a[sl] * h + bt[sl], a[sl] * cum
        hs.append(h)
        cums.append(cum)
    carry = h0
    carries = [carry]
    for s in range(SUBLANES):
        carry = cum[s:s + 1, :] * carry + h[s:s + 1, :]
        carries.append(carry)
    carry_in = jnp.concatenate(carries[:SUBLANES], axis=0)
    out = jnp.concatenate([hs[r] + cums[r] * carry_in for r in range(N_GROUPS)], axis=0)
    return out, carries[SUBLANES]


def _prompt_kernel(x_ref, mod_ref, vec_ref, bif_ref, inv_ref,
                   win_ref, wax_ref, wq_ref, wk_ref, wv_ref, wif_ref, wout_ref, *rest):
    (y_ref, o_rgh, o_rgc, o_C, o_n, o_m, o_mlc, o_S,
     cc_rg, cc_ml, qkv_s, xcml_s, mix_s, m_s, cos_s, sin_s,
     mask_s, tri_s, dret_s, cross_s, wk_s) = rest[-21:]
    T = T_TILE
    b = pl.program_id(0)
    j = pl.program_id(1)

    @pl.when(jnp.logical_and(b == 0, j == 0))
    def _init_tables():
        tr = _token_of_row(lax.broadcasted_iota(jnp.int32, (T, T), 0))
        tc = _token_of_row(lax.broadcasted_iota(jnp.int32, (T, T), 1))
        causal = tr >= tc
        mask_s[...] = jnp.where(causal, 0.0, -jnp.inf)
        tri_s[...] = jnp.where(causal, 1.0, 0.0).astype(_BF16)
        rel = (tr - tc).astype(_F32)
        tokf = _token_of_row(lax.broadcasted_iota(jnp.int32, (T, DH), 0)).astype(_F32)
        for h in range(N_HEADS):
            lg = _RET_LOG_GAMMA[h]
            dret_s[h] = jnp.where(causal, jnp.exp(rel * lg), 0.0)
            cross_s[h] = jnp.exp((tokf + 1.0) * lg)
            wk_s[h] = jnp.exp((T - 1.0 - tokf) * lg)

    @pl.when(b == 0)
    def _init_rope():
        tok = _token_of_row(lax.broadcasted_iota(jnp.int32, (T, HALF), 0))
        c, s = _rope_tables((j * T + tok).astype(_F32), inv_ref[...])
        cos_s[pl.ds(pl.multiple_of(j * T, T), T), :] = c
        sin_s[pl.ds(pl.multiple_of(j * T, T), T), :] = s

    @pl.when(j == 0)
    def _init_state():
        o_rgh[...] = jnp.zeros_like(o_rgh)
        o_C[...] = jnp.zeros_like(o_C)
        o_n[...] = jnp.zeros_like(o_n)
        o_S[...] = jnp.zeros_like(o_S)
        m_s[...] = jnp.zeros_like(m_s)
        cc_rg[...] = jnp.zeros_like(cc_rg)
        cc_ml[...] = jnp.zeros_like(cc_ml)

    x = x_ref[0]
    shift = mod_ref[pl.ds(b, 1), 0:D_MODEL]
    scale = mod_ref[pl.ds(b, 1), D_MODEL:2 * D_MODEL]
    gate = mod_ref[pl.ds(b, 1), 2 * D_MODEL:3 * D_MODEL]
    u = _rms(x) * (vec_ref[V_NPRE:V_NPRE + 1, :] * (1.0 + scale)) + shift
    ub = u.astype(_BF16)
    n_chunk = D_MODEL // MXU_N
    n_wide = D_MODEL // W_CHUNK

    pending = [(P_XML, w) for w in range(n_wide)]
    pending += [(blk, w) for w in range(n_wide) for blk in (P_XRG, P_ZRG, P_GA)]
    pending += [(blk, w) for w in range(n_wide) for blk in (P_ZML, P_GB)]
    pending += [(blk, w) for w in range(n_wide) for blk in (P_QR, P_KR, P_VR, P_GR, P_GC)]
    ready = {}

    def prefetch(n):
        for _ in range(min(n, len(pending))):
            blk, w = pending.pop(0)
            ready[(blk, w)] = _dot(ub, win_ref[blk * n_wide + w])

    def proj(blk, c):
        per_wide = W_CHUNK // MXU_N
        key = (blk, c // per_wide)
        while key not in ready:
            prefetch(1)
        lo = (c % per_wide) * MXU_N
        v = ready[key][:, lo:lo + MXU_N]
        return 0.5 * v if blk in P_GATE_LIKE else v

    prefetch(PREFETCH_START)
    for h in range(N_HEADS):
        cs = slice(h * DH, (h + 1) * DH)
        x_ml = proj(P_XML, h)
        prefetch(PREFETCH_B1[h])
        xc_ml = _silu(_shifted_conv(x_ml, cs, cc_ml, vec_ref, V_MLCW, V_MLCB, o_mlc))
        xcml_s[:, cs] = xc_ml
        xhb = xc_ml.astype(_BF16)
        prefetch(1)
        q = _dot(xhb, wq_ref[h].astype(_BF16)).astype(_BF16)
        k = (_dot(xhb, wk_ref[h].astype(_BF16)) * (DH ** -0.5)).astype(_BF16)
        v = _dot(x_ml.astype(_BF16), wv_ref[h].astype(_BF16)).astype(_BF16)
        qkv_s[:, cs] = q
        qkv_s[:, D_MODEL + h * DH:D_MODEL + (h + 1) * DH] = k
        qkv_s[:, 2 * D_MODEL + h * DH:2 * D_MODEL + (h + 1) * DH] = v
    mask = mask_s[...]

    def gates_sums():
        gpre = jnp.concatenate([_dot(qkv_s[0:T // 2, :], wif_ref[...]),
                                _dot(qkv_s[T // 2:T, :], wif_ref[...])], axis=0) + bif_ref[...]
        lf = _log_sigmoid(gpre[:, LANES:])
        prefetch(1)
        F = sum(_dot(tri_s[...], part) for part in _split3_bf16(lf))
        uu = gpre[:, :LANES] - F
        return F, uu, uu.T

    def gates_running_max(uu_rows):
        lane = lax.broadcasted_iota(jnp.int32, (T, LANES), 1)
        cmax = jnp.zeros((T, LANES), _F32)
        for h in range(N_HEADS):
            cm_h = jnp.max(uu_rows[h:h + 1, :] + mask, axis=1, keepdims=True)
            cmax = jnp.where(lane == h, cm_h, cmax)
        return cmax

    def gates_columns(F, uu, cmax):
        m0 = m_s[0:1, :]
        mrun = jnp.maximum(m0, cmax)
        m = F + mrun
        mL = m[T - 1:T, :]
        FL = F[T - 1:T, :]
        m_s[...] = jnp.broadcast_to(mL, m_s.shape)
        o_m[0] = m_s[...]
        return (-mrun,
                jnp.exp(m0 - mrun),
                jnp.exp(-m),
                jnp.exp(FL - mL + uu),
                jnp.broadcast_to(jnp.exp(FL + m0 - mL), (T, LANES)))

    prefetch(PREFETCH_GATES)
    F, uu, uu_rows = gates_sums()
    for c in range(n_chunk):
        cs = slice(c * MXU_N, (c + 1) * MXU_N)
        x_rg = proj(P_XRG, c)
        prefetch(2)
        xc = _shifted_conv(x_rg, cs, cc_rg, vec_ref, V_RGCW, V_RGCB, o_rgc)
        prefetch(1)
        a, bt = _rg_gates(xc, vec_ref, wax_ref, c * MXU_N)
        prefetch(PREFETCH_A - 3)
        h_rg, h_last = _segment_scan(a, bt, o_rgh[0, :, cs])
        o_rgh[0, :, cs] = h_last
        mix_s[:, cs] = _sigmoid_of_twice(proj(P_GA, c)) * (h_rg * _silu_of_twice(proj(P_ZRG, c)))
        if c == 0:
            cmax = gates_running_max(uu_rows)
        if c == 1:
            c_row, c_inter, c_floor, c_wl, c_decay = gates_columns(F, uu, cmax)

    for h in range(N_HEADS):
        cs = slice(h * DH, (h + 1) * DH)
        prefetch(1)
        qh = qkv_s[:, h * DH:(h + 1) * DH]
        kh = qkv_s[:, D_MODEL + h * DH:D_MODEL + (h + 1) * DH]
        vh = qkv_s[:, 2 * D_MODEL + h * DH:2 * D_MODEL + (h + 1) * DH]
        col = lambda c: c[:, h:h + 1]
        dmat = jnp.exp(col(c_row) + (uu_rows[h:h + 1, :] + mask))
        s = lax.dot_general(qh, kh, _NT, preferred_element_type=_F32) * dmat
        C0 = o_C[0, h]
        n0 = o_n[0, h:h + 1, :]
        prefetch(1)
        num = col(c_inter) * _dot(qh, C0.astype(_BF16)) + _dot(s.astype(_BF16), vh)
        den = (col(c_inter) * jnp.sum(qh.astype(_F32) * n0, axis=-1, keepdims=True)
               + jnp.sum(s, axis=-1, keepdims=True))
        h_ml = num * (1.0 / jnp.maximum(jnp.abs(den), col(c_floor)))
        kw = kh.astype(_F32) * col(c_wl)
        prefetch(PREFETCH_B2 - 2)
        o_C[0, h] = col(c_decay) * C0 + lax.dot_general(
            kw.astype(_BF16), vh, _TN, preferred_element_type=_F32)
        o_n[0, h:h + 1, :] = col(c_decay)[0:1, :] * n0 + jnp.sum(kw, axis=0, keepdims=True)
        y_b = (_headnorm(h_ml) * vec_ref[V_MLGN:V_MLGN + 1, cs]
               + vec_ref[V_MLSKIP:V_MLSKIP + 1, cs] * xcml_s[:, cs])
        mix_s[:, cs] += _sigmoid_of_twice(proj(P_GB, h)) * (y_b * _silu_of_twice(proj(P_ZML, h)))

    t0 = pl.multiple_of(j * T, T)
    cos = cos_s[pl.ds(t0, T), :]
    sin = sin_s[pl.ds(t0, T), :]

    def rope(xh):
        x1, x2 = xh[:, :HALF], xh[:, HALF:]
        return jnp.concatenate([x1 * cos - x2 * sin, x1 * sin + x2 * cos], axis=1)

    for h in range(N_HEADS):
        cs = slice(h * DH, (h + 1) * DH)
        qh = rope(proj(P_QR, h)).astype(_BF16)
        kf = rope(proj(P_KR, h)) * (DH ** -0.5)
        vh = proj(P_VR, h).astype(_BF16)
        prefetch(1)
        s = lax.dot_general(qh, kf.astype(_BF16), _NT, preferred_element_type=_F32) * dret_s[h]
        S0 = o_S[0, h]
        prefetch(PREFETCH_C - 1)
        h_ret = _dot(s.astype(_BF16), vh) + _dot(qh, S0.astype(_BF16)) * cross_s[h]
        kw = (kf * wk_s[h]).astype(_BF16)
        o_S[0, h] = math.exp(T * _RET_LOG_GAMMA[h]) * S0 + lax.dot_general(
            kw, vh, _TN, preferred_element_type=_F32)
        y_c = _headnorm(h_ret) * vec_ref[V_RETGN:V_RETGN + 1, cs] * _silu_of_twice(proj(P_GR, h))
        mix_s[:, cs] += _sigmoid_of_twice(proj(P_GC, h)) * y_c

    out = _dot_chunks(mix_s[...].astype(_BF16), wout_ref, 0, n_wide)
    y_ref[0] = x + gate * (_rms(out) * vec_ref[V_NPOST:V_NPOST + 1, :])


def _prompt_layer(layer, prev_big, mod_row0, x, mod, vec, bif2, inv, win, wax, wq, wk, wv, wif2, wout):
    B, L, _ = x.shape
    T = T_TILE
    per_b3 = lambda b, j: (b, 0, 0)
    big_blk = pl.BlockSpec((None, 1, N_HEADS, DH, DH), lambda b, j: (layer, b, 0, 0, 0))
    big_shape = jax.ShapeDtypeStruct((DEPTH, B, N_HEADS, DH, DH), _F32)

    def resident(*shape):
        return pl.BlockSpec((None,) + shape, lambda b, j: (layer,) + (0,) * len(shape),
                            pipeline_mode=pl.Buffered(1))

    n_in = 12
    extra_specs = [pl.BlockSpec(memory_space=pl.ANY)] * len(prev_big)
    aliases = {n_in: 3, n_in + 1: 7} if prev_big else {}
    tail = (CONV_W - 1) * SUBLANES
    outs = pl.pallas_call(
        _prompt_kernel,
        grid=(B, L // T),
        input_output_aliases=aliases,
        in_specs=[
            pl.BlockSpec((1, T, D_MODEL), lambda b, j: (b, j, 0)),
            pl.BlockSpec((None, B, 3 * D_MODEL), lambda b, j: (layer, mod_row0 // B, 0),
                         pipeline_mode=pl.Buffered(1)),
            resident(V_ROWS, D_MODEL),
            resident(1, 2 * LANES),
            pl.BlockSpec((1, HALF), lambda b, j: (0, 0), pipeline_mode=pl.Buffered(1)),
            resident(D_IN // W_CHUNK, D_MODEL, W_CHUNK),
            resident(RG_HEADS, RG_BW, 2 * RG_BW),
            resident(N_HEADS, DH, DH),
            resident(N_HEADS, DH, DH),
            resident(N_HEADS, DH, DH),
            resident(3 * D_MODEL, 2 * LANES),
            resident(D_MODEL // W_CHUNK, D_MODEL, W_CHUNK),
        ] + extra_specs,
        out_specs=[
            pl.BlockSpec((1, T, D_MODEL), lambda b, j: (b, j, 0)),
            pl.BlockSpec((1, 1, D_MODEL), per_b3),
            pl.BlockSpec((1, CONV_W - 1, D_MODEL), per_b3),
            big_blk,
            pl.BlockSpec((1, N_HEADS, DH), per_b3),
            pl.BlockSpec((1, SUBLANES, LANES), per_b3),
            pl.BlockSpec((1, CONV_W - 1, D_MODEL), per_b3),
            big_blk,
        ],
        out_shape=[
            jax.ShapeDtypeStruct((B, L, D_MODEL), _F32),
            jax.ShapeDtypeStruct((B, 1, D_MODEL), _F32),
            jax.ShapeDtypeStruct((B, CONV_W - 1, D_MODEL), _F32),
            big_shape,
            jax.ShapeDtypeStruct((B, N_HEADS, DH), _F32),
            jax.ShapeDtypeStruct((B, SUBLANES, LANES), _F32),
            jax.ShapeDtypeStruct((B, CONV_W - 1, D_MODEL), _F32),
            big_shape,
        ],
        scratch_shapes=[
            pltpu.VMEM((tail, D_MODEL), _F32),
            pltpu.VMEM((tail, D_MODEL), _F32),
            pltpu.VMEM((T, 3 * D_MODEL), _BF16),
            pltpu.VMEM((T, D_MODEL), _F32),
            pltpu.VMEM((T, D_MODEL), _F32),
            pltpu.VMEM((SUBLANES, LANES), _F32),
            pltpu.VMEM((L, HALF), _F32),
            pltpu.VMEM((L, HALF), _F32),
            pltpu.VMEM((T, T), _F32),
            pltpu.VMEM((T, T), _BF16),
            pltpu.VMEM((N_HEADS, T, T), _F32),
            pltpu.VMEM((N_HEADS, T, DH), _F32),
            pltpu.VMEM((N_HEADS, T, DH), _F32),
        ],
        compiler_params=pltpu.CompilerParams(
            dimension_semantics=("arbitrary", "arbitrary"),
            vmem_limit_bytes=VMEM_LIMIT),
        name="prompt_layer",
    )(x, mod, vec, bif2, inv, win, wax, wq, wk, wv, wif2, wout, *prev_big)
    y, rgh, rgc, mlC, mln, mlm, mlc, retS = outs
    return y, (rgh[:, 0], rgc, mln, mlm[:, 0, :N_HEADS], mlc), (mlC, retS)


def _interleave_tiles(x):
    B, L, D = x.shape
    return x.reshape(B, L // T_TILE, SUBLANES, SEG, D).transpose(0, 1, 3, 2, 4).reshape(B, L, D)


def _deinterleave_tiles(x):
    B, L, D = x.shape
    return x.reshape(B, L // T_TILE, SEG, SUBLANES, D).transpose(0, 1, 3, 2, 4).reshape(B, L, D)


def _sample_proj_kernel(x_ref, mod_ref, vec_ref, w_ref, o_ref):
    scale = mod_ref[:, D_MODEL:2 * D_MODEL]
    u = _rms(x_ref[...]) * (vec_ref[V_NPRE:V_NPRE + 1, :] * (1.0 + scale)) + mod_ref[:, 0:D_MODEL]
    o_ref[...] = _dot_chunks(u.astype(_BF16), w_ref, 0, w_ref.shape[0])


def _sample_proj(layer, x, mod, vec, win):
    rows = x.shape[0]
    n_steps = D_MODEL // W_CHUNK
    return pl.pallas_call(
        _sample_proj_kernel,
        grid=(n_steps,),
        in_specs=[
            pl.BlockSpec((rows, D_MODEL), lambda n: (0, 0)),
            pl.BlockSpec((None, rows, 3 * D_MODEL), lambda n: (layer, 0, 0)),
            pl.BlockSpec((None, V_ROWS, D_MODEL), lambda n: (layer, 0, 0)),
            pl.BlockSpec((None, N_PROJ, D_MODEL, W_CHUNK), lambda n: (layer, n, 0, 0)),
        ],
        out_specs=pl.BlockSpec((rows, N_PROJ * W_CHUNK), lambda n: (0, n)),
        out_shape=jax.ShapeDtypeStruct((rows, D_IN), _F32),
        compiler_params=pltpu.CompilerParams(vmem_limit_bytes=VMEM_LIMIT),
        name="sample_proj",
    )(x, mod, vec, win)


S_BT = 2


def _bcast_heads(cols):
    rows = cols[0].shape[0]
    return jnp.concatenate([jnp.broadcast_to(c, (rows, DH)) for c in cols], axis=1)


def _sample_mix_kernel(proj_ref, x_ref, mod_ref, vec_ref, bif_ref, inv_ref,
                       wax_ref, wq_ref, wk_ref, wv_ref, wif_ref, wout_ref,
                       rgh_ref, rgc_ref, mlc_ref, mln_ref, mlm_ref, C_ref, S_ref, *rest):
    (y_ref, o_rgh, o_rgc, o_mlc, o_mln, o_mlm, o_C, o_S,
     mix_s, xcml_s, inter_s, dv_s, sv_s, rden_s, svr_s, vr_s, qC_s, qS_s,
     qT_s, kT_s, qrT_s, krT_s) = rest[-22:]
    i = pl.program_id(0)
    n_steps = pl.num_programs(0)
    R = x_ref.shape[0]

    def p(blk):
        v = proj_ref[:, blk * D_MODEL:(blk + 1) * D_MODEL]
        return 0.5 * v if blk in P_GATE_LIKE else v

    def conv(xin, cs_ref, o_state, w0, brow):
        cs = cs_ref[...]
        acc = vec_ref[brow:brow + 1, :] + vec_ref[w0 + 3:w0 + 4, :] * xin
        for k in range(CONV_W - 1):
            acc = acc + vec_ref[w0 + k:w0 + k + 1, :] * cs[:, k * D_MODEL:(k + 1) * D_MODEL]
        o_state[...] = jnp.concatenate([cs[:, D_MODEL:], xin], axis=1)
        return acc

    @pl.when(i == 0)
    def _dense():
        xc = conv(p(P_XRG), rgc_ref, o_rgc, V_RGCW, V_RGCB)
        a, bt = _rg_gates(xc, vec_ref, wax_ref)
        h = a * rgh_ref[...] + bt
        o_rgh[...] = h
        mix_s[...] = _sigmoid_of_twice(p(P_GA)) * (h * _silu_of_twice(p(P_ZRG)))

        x_ml = p(P_XML)
        xc_ml = _silu(conv(x_ml, mlc_ref, o_mlc, V_MLCW, V_MLCB))
        xcml_s[...] = xc_ml
        xhb = xc_ml.astype(_BF16)
        xmb = x_ml.astype(_BF16)
        qs, ks, vs = [], [], []
        for h_ in range(N_HEADS):
            sl = slice(h_ * DH, (h_ + 1) * DH)
            qs.append(_dot(xhb[:, sl], wq_ref[h_].astype(_BF16)))
            ks.append(_dot(xhb[:, sl], wk_ref[h_].astype(_BF16)) * (DH ** -0.5))
            vs.append(_dot(xmb[:, sl], wv_ref[h_].astype(_BF16)))
        q = jnp.concatenate(qs, axis=1)
        k = jnp.concatenate(ks, axis=1)
        v = jnp.concatenate(vs, axis=1)
        qkvb = jnp.concatenate([q, k, v], axis=1).astype(_BF16)
        gpre = _dot(qkvb, wif_ref[...]) + bif_ref[...]
        ig = gpre[:, :LANES]
        lf = _log_sigmoid(gpre[:, LANES:])
        m0 = mlm_ref[...]
        m = jnp.maximum(m0 + lf, ig)
        o_mlm[...] = m
        dg = jnp.exp(ig - m)
        inter = jnp.exp(lf + m0 - m)
        floor = jnp.exp(-m)
        n0 = mln_ref[...]
        s_cols, rden_cols = [], []
        for h_ in range(N_HEADS):
            sl = slice(h_ * DH, (h_ + 1) * DH)
            s_h = jnp.sum(q[:, sl] * k[:, sl], axis=-1, keepdims=True) * dg[:, h_:h_ + 1]
            qn = jnp.sum(q[:, sl] * n0[:, sl], axis=-1, keepdims=True)
            den = inter[:, h_:h_ + 1] * qn + s_h
            s_cols.append(s_h)
            rden_cols.append(1.0 / jnp.maximum(jnp.abs(den), floor[:, h_:h_ + 1]))
        dg_b = _bcast_heads([dg[:, h_:h_ + 1] for h_ in range(N_HEADS)])
        inter_b = _bcast_heads([inter[:, h_:h_ + 1] for h_ in range(N_HEADS)])
        inter_s[...] = inter_b
        dv_s[...] = dg_b * v
        sv_s[...] = _bcast_heads(s_cols) * v
        rden_s[...] = _bcast_heads(rden_cols)
        o_mln[...] = inter_b * n0 + dg_b * k
        qT_s[...] = q.T
        kT_s[...] = k.T

        pos = jnp.full((1, HALF), float(PAST_LEN), _F32)
        cos, sin = _rope_tables(pos, inv_ref[...])
        qr = jnp.concatenate(_rope(p(P_QR), cos, sin), axis=1)
        kr = jnp.concatenate(_rope(p(P_KR), cos, sin, DH ** -0.5), axis=1)
        v_r = p(P_VR)
        s_cols = [jnp.sum(qr[:, h_ * DH:(h_ + 1) * DH] * kr[:, h_ * DH:(h_ + 1) * DH],
                          axis=-1, keepdims=True) for h_ in range(N_HEADS)]
        svr_s[...] = _bcast_heads(s_cols) * v_r
        vr_s[...] = v_r
        qrT_s[...] = qr.T
        krT_s[...] = kr.T

    lane = lax.broadcasted_iota(jnp.int32, (DH, R), 1)
    for bi in range(S_BT):
        b = i * S_BT + bi
        onehot = (lane == b).astype(_F32)

        def column(t_ref, h_):
            return jnp.sum(t_ref[h_ * DH:(h_ + 1) * DH, :] * onehot, axis=1, keepdims=True)

        for h_ in range(N_HEADS):
            sl = slice(h_ * DH, (h_ + 1) * DH)
            C0 = C_ref[bi, h_]
            qC_s[pl.ds(b, 1), sl] = jnp.sum(C0 * column(qT_s, h_), axis=0, keepdims=True)
            o_C[bi, h_] = (inter_s[pl.ds(b, 1), sl] * C0
                           + column(kT_s, h_) * dv_s[pl.ds(b, 1), sl])
            S0 = S_ref[bi, h_]
            qS_s[pl.ds(b, 1), sl] = jnp.sum(S0 * column(qrT_s, h_), axis=0, keepdims=True)
            o_S[bi, h_] = (math.exp(_RET_LOG_GAMMA[h_]) * S0
                           + column(krT_s, h_) * vr_s[pl.ds(b, 1), sl])

    @pl.when(i == n_steps - 1)
    def _epilogue():
        h_ml = (inter_s[...] * qC_s[...] + sv_s[...]) * rden_s[...]
        hn = jnp.concatenate(
            [_headnorm(h_ml[:, h_ * DH:(h_ + 1) * DH]) for h_ in range(N_HEADS)], axis=1)
        y_b = hn * vec_ref[V_MLGN:V_MLGN + 1, :] + vec_ref[V_MLSKIP:V_MLSKIP + 1, :] * xcml_s[...]
        mix = mix_s[...] + _sigmoid_of_twice(p(P_GB)) * (y_b * _silu_of_twice(p(P_ZML)))
        cross = jnp.concatenate(
            [jnp.full((1, DH), math.exp(_RET_LOG_GAMMA[h_]), _F32) for h_ in range(N_HEADS)], axis=1)
        h_ret = svr_s[...] + qS_s[...] * cross
        hn = jnp.concatenate(
            [_headnorm(h_ret[:, h_ * DH:(h_ + 1) * DH]) for h_ in range(N_HEADS)], axis=1)
        mix = mix + _sigmoid_of_twice(p(P_GC)) * (
            hn * vec_ref[V_RETGN:V_RETGN + 1, :] * _silu_of_twice(p(P_GR)))
        out = _dot_chunks(mix.astype(_BF16), wout_ref, 0, D_MODEL // W_CHUNK)
        gate = mod_ref[:, 2 * D_MODEL:3 * D_MODEL]
        y_ref[...] = x_ref[...] + gate * (_rms(out) * vec_ref[V_NPOST:V_NPOST + 1, :])


def _sample_mix(layer, prev_big, proj, x, mod, vec, bif2, inv, wax, wq, wk, wv, wif2, wout,
                rgh, rgc, mlc, mln, mlm, mlC_all, retS_all):
    R = x.shape[0]
    c2 = lambda i: (0, 0)
    st5 = lambda i: (layer, i, 0, 0, 0)

    def resident(*shape):
        return pl.BlockSpec((None,) + shape, lambda i: (layer,) + (0,) * len(shape),
                            pipeline_mode=pl.Buffered(1))

    def resident2(*shape):
        return pl.BlockSpec(shape, c2, pipeline_mode=pl.Buffered(1))

    row_blk = resident2(R, D_MODEL)
    conv_blk = resident2(R, (CONV_W - 1) * D_MODEL)
    lane_blk = resident2(R, LANES)
    state_blk = pl.BlockSpec((None, S_BT, N_HEADS, DH, DH), st5)
    f32 = lambda *s: jax.ShapeDtypeStruct(s, _F32)
    n_in = 19
    outs = pl.pallas_call(
        _sample_mix_kernel,
        grid=(R // S_BT,),
        input_output_aliases={n_in: 6, n_in + 1: 7} if prev_big else {},
        in_specs=[
            resident2(R, D_IN),
            resident2(R, D_MODEL),
            resident(R, 3 * D_MODEL),
            resident(V_ROWS, D_MODEL),
            resident(1, 2 * LANES),
            resident2(1, HALF),
            resident(RG_HEADS, RG_BW, 2 * RG_BW),
            resident(N_HEADS, DH, DH),
            resident(N_HEADS, DH, DH),
            resident(N_HEADS, DH, DH),
            resident(3 * D_MODEL, 2 * LANES),
            resident(D_MODEL // W_CHUNK, D_MODEL, W_CHUNK),
            resident(R, D_MODEL),
            resident(R, (CONV_W - 1) * D_MODEL),
            resident(R, (CONV_W - 1) * D_MODEL),
            resident(R, D_MODEL),
            resident(R, LANES),
            state_blk,
            state_blk,
        ] + [pl.BlockSpec(memory_space=pl.ANY)] * len(prev_big),
        out_specs=[row_blk, row_blk, conv_blk, conv_blk, row_blk, lane_blk, state_blk, state_blk],
        out_shape=[
            f32(R, D_MODEL), f32(R, D_MODEL), f32(R, (CONV_W - 1) * D_MODEL),
            f32(R, (CONV_W - 1) * D_MODEL), f32(R, D_MODEL), f32(R, LANES),
            f32(DEPTH, R, N_HEADS, DH, DH), f32(DEPTH, R, N_HEADS, DH, DH),
        ],
        scratch_shapes=(
            [pltpu.VMEM((R, D_MODEL), _F32) for _ in range(10)]
            + [pltpu.VMEM((D_MODEL, R), _F32) for _ in range(4)]),
        compiler_params=pltpu.CompilerParams(
            dimension_semantics=("arbitrary",), vmem_limit_bytes=VMEM_LIMIT),
        name="sample_mix",
    )(proj, x, mod, vec, bif2, inv, wax, wq, wk, wv, wif2, wout,
      rgh, rgc, mlc, mln, mlm, mlC_all, retS_all, *prev_big)
    y, o_rgh, o_rgc, o_mlc, o_mln, o_mlm, o_C, o_S = outs
    return y, (o_rgh, o_rgc.reshape(R, CONV_W - 1, D_MODEL),
               o_mln.reshape(R, N_HEADS, DH), o_mlm[:, :N_HEADS],
               o_mlc.reshape(R, CONV_W - 1, D_MODEL)), (o_C, o_S)


def _pack_params(norm_pre, norm_post, rg_conv_w, rg_conv_b, rg_b_a, rg_b_x, rg_lambda,
                 ml_conv_w, ml_conv_b, ml_skip, ml_gn, ret_gn):
    row = lambda p: p[:, None, :]
    rows = [row(norm_pre), row(norm_post), rg_conv_w, row(rg_conv_b), row(0.5 * rg_b_a), row(0.5 * rg_b_x),
            row(rg_lambda), ml_conv_w, row(ml_conv_b), row(ml_skip), row(ml_gn), row(ret_gn)]
    vec = jnp.concatenate(rows, axis=1)
    return jnp.pad(vec, ((0, 0), (0, V_ROWS - vec.shape[1]), (0, 0)))


CAST_CHUNKS = 8


def _cast_chunks_kernel(w_ref, o_ref):
    for c in range(o_ref.shape[0]):
        o_ref[c] = w_ref[:, c * W_CHUNK:(c + 1) * W_CHUNK].astype(_BF16)


def _column_chunks(w):
    depth, K, N = w.shape
    n_chunks = N // W_CHUNK
    per_step = math.gcd(n_chunks, CAST_CHUNKS)
    return pl.pallas_call(
        _cast_chunks_kernel,
        grid=(depth, n_chunks // per_step),
        in_specs=[pl.BlockSpec((None, K, per_step * W_CHUNK), lambda l, s: (l, 0, s))],
        out_specs=pl.BlockSpec((None, per_step, K, W_CHUNK), lambda l, s: (l, s, 0, 0)),
        out_shape=jax.ShapeDtypeStruct((depth, n_chunks, K, W_CHUNK), _BF16),
        compiler_params=pltpu.CompilerParams(vmem_limit_bytes=VMEM_LIMIT),
        name="cast_chunks",
    )(w)


def kernel(x_prompt, x_sample, c_prompt, c_sample, state_rg_h, state_rg_conv, state_ml_C, state_ml_n, state_ml_m, state_ml_conv, state_ret_S, w_ada, b_ada, norm_pre, norm_post, w_in, rg_conv_w, rg_conv_b, rg_w_a, rg_b_a, rg_w_x, rg_b_x, rg_lambda, ml_conv_w, ml_conv_b, ml_w_q, ml_w_k, ml_w_v, ml_w_if, ml_b_if, ml_skip, ml_gn, ret_gn, w_out):
    xs = x_sample[:, 0, :]
    R = xs.shape[0]
    mod_all = _ada_call(jnp.concatenate([c_sample, c_prompt], axis=0), w_ada, b_ada)
    inv = (ROPE_BASE ** (-jnp.linspace(0.0, 1.0, HALF, dtype=_F32)))[None, :]
    vec = _pack_params(norm_pre, norm_post, rg_conv_w, rg_conv_b, rg_b_a, rg_b_x,
                       rg_lambda, ml_conv_w, ml_conv_b, ml_skip, ml_gn, ret_gn)
    win = _column_chunks(w_in)
    wout = _column_chunks(w_out)
    wax = (0.5 * jnp.concatenate([rg_w_a, rg_w_x], axis=-1)).astype(_BF16)
    wq, wk, wv = ml_w_q, ml_w_k, ml_w_v
    lane_pad = ((0, 0), (0, 0), (0, LANES - N_HEADS))
    wif2 = jnp.concatenate([jnp.pad(ml_w_if[..., :N_HEADS], lane_pad),
                            jnp.pad(ml_w_if[..., N_HEADS:], lane_pad)], axis=-1).astype(_BF16)
    b_if = ml_b_if[:, None, :]
    bif2 = jnp.concatenate([jnp.pad(b_if[..., :N_HEADS], lane_pad),
                            jnp.pad(b_if[..., N_HEADS:], lane_pad)], axis=-1)
    s_rgc_in = state_rg_conv.reshape(DEPTH, R, -1)
    s_mlc_in = state_ml_conv.reshape(DEPTH, R, -1)
    s_mln_in = state_ml_n.reshape(DEPTH, R, -1)
    s_mlm_in = jnp.pad(state_ml_m, lane_pad)

    xp = _interleave_tiles(x_prompt)
    st_p, st_s = [], []
    big_p, big_s = (), ()
    for l in range(DEPTH):
        xp, sp, big_p = _prompt_layer(l, big_p, R, xp, mod_all, vec, bif2, inv, win, wax, wq, wk, wv, wif2, wout)
        st_p.append(sp)
        proj = _sample_proj(l, xs, mod_all, vec, win)
        xs, ss, big_s = _sample_mix(
            l, big_s, proj, xs, mod_all, vec, bif2, inv, wax, wq, wk, wv, wif2, wout,
            state_rg_h, s_rgc_in, s_mlc_in, s_mln_in, s_mlm_in, state_ml_C, state_ret_S)
        st_s.append(ss)
    p_rgh, p_rgc, p_n, p_m, p_mlc = [jnp.stack(a) for a in zip(*st_p)]
    s_rgh, s_rgc, s_n, s_m, s_mlc = [jnp.stack(a) for a in zip(*st_s)]
    return (_deinterleave_tiles(xp), xs[:, None, :], p_rgh, p_rgc, big_p[0], p_n, p_m, p_mlc, big_p[1],
            s_rgh, s_rgc, big_s[0], s_n, s_m, s_mlc, big_s[1])
```

```python
import math

import jax
import jax.numpy as jnp
from jax import lax
from jax.experimental import pallas as pl
from jax.experimental.pallas import tpu as pltpu

D_MODEL = 1024
DEPTH = 2
CONV_W = 4
RG_HEADS = 8
RG_BW = D_MODEL // RG_HEADS
RG_C = 8.0
N_HEADS = 4
DH = D_MODEL // N_HEADS
HALF = DH // 2
PAST_LEN = 16384
ROPE_BASE = 10000.0
EPS = 1e-6
N_PROJ = 11
D_IN = N_PROJ * D_MODEL
P_XRG, P_ZRG, P_XML, P_ZML, P_QR, P_KR, P_VR, P_GR, P_GA, P_GB, P_GC = range(N_PROJ)
P_GATE_LIKE = (P_ZRG, P_ZML, P_GR, P_GA, P_GB, P_GC)

V_NPRE, V_NPOST, V_RGCW, V_RGCB, V_RGBA, V_RGBX, V_RGLAM = 0, 1, 2, 6, 7, 8, 9
V_MLCW, V_MLCB, V_MLSKIP, V_MLGN, V_RETGN, V_ROWS = 10, 14, 15, 16, 17, 24

SUBLANES = 8
LANES = 128
MXU_N = 256
W_CHUNK = MXU_N
T_TILE = 256
SEG = T_TILE // SUBLANES
N_GROUPS = T_TILE // SUBLANES
VMEM_LIMIT = 60 * 1024 * 1024
PREFETCH_START = 1
PREFETCH_B1 = (1, 1, 1, 1)
PREFETCH_GATES = 1
PREFETCH_A = 4
PREFETCH_B2 = 3
PREFETCH_C = 2

_RET_LOG_GAMMA = tuple(math.log1p(-2.0 ** (-5.0 - h)) for h in range(N_HEADS))

_NT = (((1,), (1,)), ((), ()))
_TN = (((0,), (0,)), ((), ()))
_F32 = jnp.float32
_BF16 = jnp.bfloat16


def _dot(a, b):
    return jnp.dot(a, b, preferred_element_type=_F32)


def _dot_chunks(a, w_ref, first, n):
    return jnp.concatenate([_dot(a, w_ref[first + c]) for c in range(n)], axis=1)


def _sigmoid_of_twice(hx):
    return 0.5 * jnp.tanh(hx) + 0.5


def _silu_of_twice(hx):
    return hx * jnp.tanh(hx) + hx


def _silu(x):
    return _silu_of_twice(0.5 * x)


def _softplus(x):
    return jnp.maximum(x, 0.0) + jnp.log1p(jnp.exp(-jnp.abs(x)))


def _log_sigmoid(x):
    return -_softplus(-x)


def _neg_expm1_2x(x):
    t = jnp.tanh(x)
    return (-2.0 * t) / (1.0 - t)


def _rms(x):
    return x * lax.rsqrt(jnp.mean(x * x, axis=-1, keepdims=True) + EPS)


def _headnorm(h):
    mu = jnp.mean(h, axis=-1, keepdims=True)
    c = h - mu
    var = jnp.mean(c * c, axis=-1, keepdims=True)
    return c * lax.rsqrt(var + EPS)


def _rg_gates(xc, vec_ref, wax_ref, lane0=0):
    xcb = xc.astype(_BF16)
    a_parts, b_parts = [], []
    for hh in range(xc.shape[1] // RG_BW):
        loc = slice(hh * RG_BW, (hh + 1) * RG_BW)
        sl = slice(lane0 + hh * RG_BW, lane0 + (hh + 1) * RG_BW)
        g = _dot(xcb[:, loc], wax_ref[sl.start // RG_BW])
        r = _sigmoid_of_twice(g[:, :RG_BW] + vec_ref[V_RGBA:V_RGBA + 1, sl])
        i = _sigmoid_of_twice(g[:, RG_BW:] + vec_ref[V_RGBX:V_RGBX + 1, sl])
        log_a = (-RG_C) * r * _softplus(-vec_ref[V_RGLAM:V_RGLAM + 1, sl])
        a_parts.append(jnp.exp(log_a))
        b_parts.append(jnp.sqrt(_neg_expm1_2x(log_a)) * (i * xc[:, loc]))
    return jnp.concatenate(a_parts, axis=1), jnp.concatenate(b_parts, axis=1)


def _rope_tables(pos, inv_row):
    ang = pos * inv_row
    return jnp.cos(ang), jnp.sin(ang)


def _rope(x, cos, sin, scale=None):
    out = []
    for h in range(N_HEADS):
        x1 = x[:, h * DH:h * DH + HALF]
        x2 = x[:, h * DH + HALF:(h + 1) * DH]
        r = jnp.concatenate([x1 * cos - x2 * sin, x1 * sin + x2 * cos], axis=1)
        out.append(r if scale is None else r * scale)
    return out


def _token_of_row(i):
    return (i & (SUBLANES - 1)) * SEG + (i >> 3)


def _split3_bf16(x):
    p0 = x.astype(_BF16)
    r = x - p0.astype(_F32)
    p1 = r.astype(_BF16)
    p2 = (r - p1.astype(_F32)).astype(_BF16)
    return p0, p1, p2


def _ada_kernel(c_ref, w_ref, b_ref, o_ref):
    c = c_ref[...]
    w = w_ref[...]
    c_hi = c.astype(_BF16)
    c_lo = (c - c_hi.astype(_F32)).astype(_BF16)
    w_hi = w.astype(_BF16)
    o_ref[...] = _dot(c_hi, w_hi) + _dot(c_lo, w_hi) + b_ref[...]


def _ada_call(c_all, w_ada, b_ada):
    rows = c_all.shape[0]
    return pl.pallas_call(
        _ada_kernel,
        grid=(DEPTH, 3),
        in_specs=[
            pl.BlockSpec((rows, D_MODEL), lambda l, n: (0, 0)),
            pl.BlockSpec((None, D_MODEL, D_MODEL), lambda l, n: (l, 0, n)),
            pl.BlockSpec((None, 1, D_MODEL), lambda l, n: (l, 0, n)),
        ],
        out_specs=pl.BlockSpec((None, rows, D_MODEL), lambda l, n: (l, 0, n)),
        out_shape=jax.ShapeDtypeStruct((DEPTH, rows, 3 * D_MODEL), _F32),
        name="adaln",
    )(c_all, w_ada, b_ada.reshape(DEPTH, 1, 3 * D_MODEL))


def _shifted_conv(xin, cs, carry_ref, vec_ref, w0, brow, o_state):
    T = T_TILE
    tail = (CONV_W - 1) * SUBLANES
    prev = carry_ref[:, cs]
    cur = xin[T - tail:T, :]
    sub = lax.broadcasted_iota(jnp.int32, (SUBLANES, xin.shape[1]), 0)
    lead = []
    for q in range(CONV_W - 1):
        sl = slice(q * SUBLANES, (q + 1) * SUBLANES)
        lead.append(jnp.where(sub == 0, pltpu.roll(prev[sl], 1, axis=0), pltpu.roll(cur[sl], 1, axis=0)))
    ext = jnp.concatenate(lead + [xin], axis=0)
    acc = vec_ref[brow:brow + 1, cs] + vec_ref[w0 + 3:w0 + 4, cs] * xin
    for d in range(1, CONV_W):
        acc = acc + vec_ref[w0 + 3 - d:w0 + 4 - d, cs] * ext[tail - SUBLANES * d:tail - SUBLANES * d + T, :]
    carry_ref[:, cs] = cur
    o_state[0, :, cs] = jnp.concatenate(
        [cur[q * SUBLANES + SUBLANES - 1:(q + 1) * SUBLANES, :] for q in range(CONV_W - 1)], axis=0)
    return acc


def _segment_scan(a, bt, h0):
    hs, cums = [], []
    h = cum = None
    for r in range(N_GROUPS):
        sl = slice(r * SUBLANES, (r + 1) * SUBLANES)
        if r == 0:
            h, cum = bt[sl], a[sl]
        else:
            h, cum = a[sl] * h + bt[sl], a[sl] * cum
        hs.append(h)
        cums.append(cum)
    carry = h0
    carries = [carry]
    for s in range(SUBLANES):
        carry = cum[s:s + 1, :] * carry + h[s:s + 1, :]
        carries.append(carry)
    carry_in = jnp.concatenate(carries[:SUBLANES], axis=0)
    out = jnp.concatenate([hs[r] + cums[r] * carry_in for r in range(N_GROUPS)], axis=0)
    return out, carries[SUBLANES]


def _prompt_kernel(x_ref, mod_ref, vec_ref, bif_ref, inv_ref,
                   win_ref, wax_ref, wq_ref, wk_ref, wv_ref, wif_ref, wout_ref, *rest):
    (y_ref, o_rgh, o_rgc, o_C, o_n, o_m, o_mlc, o_S,
     cc_rg, cc_ml, qkv_s, xcml_s, mix_s, m_s, cos_s, sin_s,
     mask_s, tri_s, dret_s, cross_s, wk_s) = rest[-21:]
    T = T_TILE
    b = pl.program_id(0)
    j = pl.program_id(1)

    @pl.when(jnp.logical_and(b == 0, j == 0))
    def _init_tables():
        tr = _token_of_row(lax.broadcasted_iota(jnp.int32, (T, T), 0))
        tc = _token_of_row(lax.broadcasted_iota(jnp.int32, (T, T), 1))
        causal = tr >= tc
        mask_s[...] = jnp.where(causal, 0.0, -jnp.inf)
        tri_s[...] = jnp.where(causal, 1.0, 0.0).astype(_BF16)
        rel = (tr - tc).astype(_F32)
        tokf = _token_of_row(lax.broadcasted_iota(jnp.int32, (T, DH), 0)).astype(_F32)
        for h in range(N_HEADS):
            lg = _RET_LOG_GAMMA[h]
            dret_s[h] = jnp.where(causal, jnp.exp(rel * lg), 0.0)
            cross_s[h] = jnp.exp((tokf + 1.0) * lg)
            wk_s[h] = jnp.exp((T - 1.0 - tokf) * lg)

    @pl.when(b == 0)
    def _init_rope():
        tok = _token_of_row(lax.broadcasted_iota(jnp.int32, (T, HALF), 0))
        c, s = _rope_tables((j * T + tok).astype(_F32), inv_ref[...])
        cos_s[pl.ds(pl.multiple_of(j * T, T), T), :] = c
        sin_s[pl.ds(pl.multiple_of(j * T, T), T), :] = s

    @pl.when(j == 0)
    def _init_state():
        o_rgh[...] = jnp.zeros_like(o_rgh)
        o_C[...] = jnp.zeros_like(o_C)
        o_n[...] = jnp.zeros_like(o_n)
        o_S[...] = jnp.zeros_like(o_S)
        m_s[...] = jnp.zeros_like(m_s)
        cc_rg[...] = jnp.zeros_like(cc_rg)
        cc_ml[...] = jnp.zeros_like(cc_ml)

    x = x_ref[0]
    shift = mod_ref[pl.ds(b, 1), 0:D_MODEL]
    scale = mod_ref[pl.ds(b, 1), D_MODEL:2 * D_MODEL]
    gate = mod_ref[pl.ds(b, 1), 2 * D_MODEL:3 * D_MODEL]
    u = _rms(x) * (vec_ref[V_NPRE:V_NPRE + 1, :] * (1.0 + scale)) + shift
    ub = u.astype(_BF16)
    n_chunk = D_MODEL // MXU_N
    n_wide = D_MODEL // W_CHUNK

    pending = [(P_XML, w) for w in range(n_wide)]
    pending += [(blk, w) for w in range(n_wide) for blk in (P_XRG, P_ZRG, P_GA)]
    pending += [(blk, w) for w in range(n_wide) for blk in (P_ZML, P_GB)]
    pending += [(blk, w) for w in range(n_wide) for blk in (P_QR, P_KR, P_VR, P_GR, P_GC)]
    ready = {}

    def prefetch(n):
        for _ in range(min(n, len(pending))):
            blk, w = pending.pop(0)
            ready[(blk, w)] = _dot(ub, win_ref[blk * n_wide + w])

    def proj(blk, c):
        per_wide = W_CHUNK // MXU_N
        key = (blk, c // per_wide)
        while key not in ready:
            prefetch(1)
        lo = (c % per_wide) * MXU_N
        v = ready[key][:, lo:lo + MXU_N]
        return 0.5 * v if blk in P_GATE_LIKE else v

    prefetch(PREFETCH_START)
    for h in range(N_HEADS):
        cs = slice(h * DH, (h + 1) * DH)
        x_ml = proj(P_XML, h)
        prefetch(PREFETCH_B1[h])
        xc_ml = _silu(_shifted_conv(x_ml, cs, cc_ml, vec_ref, V_MLCW, V_MLCB, o_mlc))
        xcml_s[:, cs] = xc_ml
        xhb = xc_ml.astype(_BF16)
        prefetch(1)
        q = _dot(xhb, wq_ref[h].astype(_BF16)).astype(_BF16)
        k = (_dot(xhb, wk_ref[h].astype(_BF16)) * (DH ** -0.5)).astype(_BF16)
        v = _dot(x_ml.astype(_BF16), wv_ref[h].astype(_BF16)).astype(_BF16)
        qkv_s[:, cs] = q
        qkv_s[:, D_MODEL + h * DH:D_MODEL + (h + 1) * DH] = k
        qkv_s[:, 2 * D_MODEL + h * DH:2 * D_MODEL + (h + 1) * DH] = v
    mask = mask_s[...]

    def gates_sums():
        gpre = jnp.concatenate([_dot(qkv_s[0:T // 2, :], wif_ref[...]),
                                _dot(qkv_s[T // 2:T, :], wif_ref[...])], axis=0) + bif_ref[...]
        lf = _log_sigmoid(gpre[:, LANES:])
        prefetch(1)
        F = sum(_dot(tri_s[...], part) for part in _split3_bf16(lf))
        uu = gpre[:, :LANES] - F
        return F, uu, uu.T

    def gates_running_max(uu_rows):
        lane = lax.broadcasted_iota(jnp.int32, (T, LANES), 1)
        cmax = jnp.zeros((T, LANES), _F32)
        for h in range(N_HEADS):
            cm_h = jnp.max(uu_rows[h:h + 1, :] + mask, axis=1, keepdims=True)
            cmax = jnp.where(lane == h, cm_h, cmax)
        return cmax

    def gates_columns(F, uu, cmax):
        m0 = m_s[0:1, :]
        mrun = jnp.maximum(m0, cmax)
        m = F + mrun
        mL = m[T - 1:T, :]
        FL = F[T - 1:T, :]
        m_s[...] = jnp.broadcast_to(mL, m_s.shape)
        o_m[0] = m_s[...]
        return (-mrun,
                jnp.exp(m0 - mrun),
                jnp.exp(-m),
                jnp.exp(FL - mL + uu),
                jnp.broadcast_to(jnp.exp(FL + m0 - mL), (T, LANES)))

    prefetch(PREFETCH_GATES)
    F, uu, uu_rows = gates_sums()
    for c in range(n_chunk):
        cs = slice(c * MXU_N, (c + 1) * MXU_N)
        x_rg = proj(P_XRG, c)
        prefetch(2)
        xc = _shifted_conv(x_rg, cs, cc_rg, vec_ref, V_RGCW, V_RGCB, o_rgc)
        prefetch(1)
        a, bt = _rg_gates(xc, vec_ref, wax_ref, c * MXU_N)
        prefetch(PREFETCH_A - 3)
        h_rg, h_last = _segment_scan(a, bt, o_rgh[0, :, cs])
        o_rgh[0, :, cs] = h_last
        mix_s[:, cs] = _sigmoid_of_twice(proj(P_GA, c)) * (h_rg * _silu_of_twice(proj(P_ZRG, c)))
        if c == 0:
            cmax = gates_running_max(uu_rows)
        if c == 1:
            c_row, c_inter, c_floor, c_wl, c_decay = gates_columns(F, uu, cmax)

    for h in range(N_HEADS):
        cs = slice(h * DH, (h + 1) * DH)
        prefetch(1)
        qh = qkv_s[:, h * DH:(h + 1) * DH]
        kh = qkv_s[:, D_MODEL + h * DH:D_MODEL + (h + 1) * DH]
        vh = qkv_s[:, 2 * D_MODEL + h * DH:2 * D_MODEL + (h + 1) * DH]
        col = lambda c: c[:, h:h + 1]
        dmat = jnp.exp(col(c_row) + (uu_rows[h:h + 1, :] + mask))
        s = lax.dot_general(qh, kh, _NT, preferred_element_type=_F32) * dmat
        C0 = o_C[0, h]
        n0 = o_n[0, h:h + 1, :]
        prefetch(1)
        num = col(c_inter) * _dot(qh, C0.astype(_BF16)) + _dot(s.astype(_BF16), vh)
        den = (col(c_inter) * jnp.sum(qh.astype(_F32) * n0, axis=-1, keepdims=True)
               + jnp.sum(s, axis=-1, keepdims=True))
        h_ml = num * (1.0 / jnp.maximum(jnp.abs(den), col(c_floor)))
        kw = kh.astype(_F32) * col(c_wl)
        prefetch(PREFETCH_B2 - 2)
        o_C[0, h] = col(c_decay) * C0 + lax.dot_general(
            kw.astype(_BF16), vh, _TN, preferred_element_type=_F32)
        o_n[0, h:h + 1, :] = col(c_decay)[0:1, :] * n0 + jnp.sum(kw, axis=0, keepdims=True)
        y_b = (_headnorm(h_ml) * vec_ref[V_MLGN:V_MLGN + 1, cs]
               + vec_ref[V_MLSKIP:V_MLSKIP + 1, cs] * xcml_s[:, cs])
        mix_s[:, cs] += _sigmoid_of_twice(proj(P_GB, h)) * (y_b * _silu_of_twice(proj(P_ZML, h)))

    t0 = pl.multiple_of(j * T, T)
    cos = cos_s[pl.ds(t0, T), :]
    sin = sin_s[pl.ds(t0, T), :]

    def rope(xh):
        x1, x2 = xh[:, :HALF], xh[:, HALF:]
        return jnp.concatenate([x1 * cos - x2 * sin, x1 * sin + x2 * cos], axis=1)

    for h in range(N_HEADS):
        cs = slice(h * DH, (h + 1) * DH)
        qh = rope(proj(P_QR, h)).astype(_BF16)
        kf = rope(proj(P_KR, h)) * (DH ** -0.5)
        vh = proj(P_VR, h).astype(_BF16)
        prefetch(1)
        s = lax.dot_general(qh, kf.astype(_BF16), _NT, preferred_element_type=_F32) * dret_s[h]
        S0 = o_S[0, h]
        prefetch(PREFETCH_C - 1)
        h_ret = _dot(s.astype(_BF16), vh) + _dot(qh, S0.astype(_BF16)) * cross_s[h]
        kw = (kf * wk_s[h]).astype(_BF16)
        o_S[0, h] = math.exp(T * _RET_LOG_GAMMA[h]) * S0 + lax.dot_general(
            kw, vh, _TN, preferred_element_type=_F32)
        y_c = _headnorm(h_ret) * vec_ref[V_RETGN:V_RETGN + 1, cs] * _silu_of_twice(proj(P_GR, h))
        mix_s[:, cs] += _sigmoid_of_twice(proj(P_GC, h)) * y_c

    out = _dot_chunks(mix_s[...].astype(_BF16), wout_ref, 0, n_wide)
    y_ref[0] = x + gate * (_rms(out) * vec_ref[V_NPOST:V_NPOST + 1, :])


def _prompt_layer(layer, prev_big, mod_row0, x, mod, vec, bif2, inv, win, wax, wq, wk, wv, wif2, wout):
    B, L, _ = x.shape
    T = T_TILE
    per_b3 = lambda b, j: (b, 0, 0)
    big_blk = pl.BlockSpec((None, 1, N_HEADS, DH, DH), lambda b, j: (layer, b, 0, 0, 0))
    big_shape = jax.ShapeDtypeStruct((DEPTH, B, N_HEADS, DH, DH), _F32)

    def resident(*shape):
        return pl.BlockSpec((None,) + shape, lambda b, j: (layer,) + (0,) * len(shape),
                            pipeline_mode=pl.Buffered(1))

    n_in = 12
    extra_specs = [pl.BlockSpec(memory_space=pl.ANY)] * len(prev_big)
    aliases = {n_in: 3, n_in + 1: 7} if prev_big else {}
    tail = (CONV_W - 1) * SUBLANES
    outs = pl.pallas_call(
        _prompt_kernel,
        grid=(B, L // T),
        input_output_aliases=aliases,
        in_specs=[
            pl.BlockSpec((1, T, D_MODEL), lambda b, j: (b, j, 0)),
            pl.BlockSpec((None, B, 3 * D_MODEL), lambda b, j: (layer, mod_row0 // B, 0),
                         pipeline_mode=pl.Buffered(1)),
            resident(V_ROWS, D_MODEL),
            resident(1, 2 * LANES),
            pl.BlockSpec((1, HALF), lambda b, j: (0, 0), pipeline_mode=pl.Buffered(1)),
            resident(D_IN // W_CHUNK, D_MODEL, W_CHUNK),
            resident(RG_HEADS, RG_BW, 2 * RG_BW),
            resident(N_HEADS, DH, DH),
            resident(N_HEADS, DH, DH),
            resident(N_HEADS, DH, DH),
            resident(3 * D_MODEL, 2 * LANES),
            resident(D_MODEL // W_CHUNK, D_MODEL, W_CHUNK),
        ] + extra_specs,
        out_specs=[
            pl.BlockSpec((1, T, D_MODEL), lambda b, j: (b, j, 0)),
            pl.BlockSpec((1, 1, D_MODEL), per_b3),
            pl.BlockSpec((1, CONV_W - 1, D_MODEL), per_b3),
            big_blk,
            pl.BlockSpec((1, N_HEADS, DH), per_b3),
            pl.BlockSpec((1, SUBLANES, LANES), per_b3),
            pl.BlockSpec((1, CONV_W - 1, D_MODEL), per_b3),
            big_blk,
        ],
        out_shape=[
            jax.ShapeDtypeStruct((B, L, D_MODEL), _F32),
            jax.ShapeDtypeStruct((B, 1, D_MODEL), _F32),
            jax.ShapeDtypeStruct((B, CONV_W - 1, D_MODEL), _F32),
            big_shape,
            jax.ShapeDtypeStruct((B, N_HEADS, DH), _F32),
            jax.ShapeDtypeStruct((B, SUBLANES, LANES), _F32),
            jax.ShapeDtypeStruct((B, CONV_W - 1, D_MODEL), _F32),
            big_shape,
        ],
        scratch_shapes=[
            pltpu.VMEM((tail, D_MODEL), _F32),
            pltpu.VMEM((tail, D_MODEL), _F32),
            pltpu.VMEM((T, 3 * D_MODEL), _BF16),
            pltpu.VMEM((T, D_MODEL), _F32),
            pltpu.VMEM((T, D_MODEL), _F32),
            pltpu.VMEM((SUBLANES, LANES), _F32),
            pltpu.VMEM((L, HALF), _F32),
            pltpu.VMEM((L, HALF), _F32),
            pltpu.VMEM((T, T), _F32),
            pltpu.VMEM((T, T), _BF16),
            pltpu.VMEM((N_HEADS, T, T), _F32),
            pltpu.VMEM((N_HEADS, T, DH), _F32),
            pltpu.VMEM((N_HEADS, T, DH), _F32),
        ],
        compiler_params=pltpu.CompilerParams(
            dimension_semantics=("arbitrary", "arbitrary"),
            vmem_limit_bytes=VMEM_LIMIT),
        name="prompt_layer",
    )(x, mod, vec, bif2, inv, win, wax, wq, wk, wv, wif2, wout, *prev_big)
    y, rgh, rgc, mlC, mln, mlm, mlc, retS = outs
    return y, (rgh[:, 0], rgc, mln, mlm[:, 0, :N_HEADS], mlc), (mlC, retS)


def _interleave_tiles(x):
    B, L, D = x.shape
    return x.reshape(B, L // T_TILE, SUBLANES, SEG, D).transpose(0, 1, 3, 2, 4).reshape(B, L, D)


def _deinterleave_tiles(x):
    B, L, D = x.shape
    return x.reshape(B, L // T_TILE, SEG, SUBLANES, D).transpose(0, 1, 3, 2, 4).reshape(B, L, D)


def _sample_proj_kernel(x_ref, mod_ref, vec_ref, w_ref, o_ref):
    scale = mod_ref[:, D_MODEL:2 * D_MODEL]
    u = _rms(x_ref[...]) * (vec_ref[V_NPRE:V_NPRE + 1, :] * (1.0 + scale)) + mod_ref[:, 0:D_MODEL]
    o_ref[...] = _dot_chunks(u.astype(_BF16), w_ref, 0, w_ref.shape[0])


def _sample_proj(layer, x, mod, vec, win):
    rows = x.shape[0]
    n_steps = D_MODEL // W_CHUNK
    return pl.pallas_call(
        _sample_proj_kernel,
        grid=(n_steps,),
        in_specs=[
            pl.BlockSpec((rows, D_MODEL), lambda n: (0, 0)),
            pl.BlockSpec((None, rows, 3 * D_MODEL), lambda n: (layer, 0, 0)),
            pl.BlockSpec((None, V_ROWS, D_MODEL), lambda n: (layer, 0, 0)),
            pl.BlockSpec((None, N_PROJ, D_MODEL, W_CHUNK), lambda n: (layer, n, 0, 0)),
        ],
        out_specs=pl.BlockSpec((rows, N_PROJ * W_CHUNK), lambda n: (0, n)),
        out_shape=jax.ShapeDtypeStruct((rows, D_IN), _F32),
        compiler_params=pltpu.CompilerParams(vmem_limit_bytes=VMEM_LIMIT),
        name="sample_proj",
    )(x, mod, vec, win)


S_BT = 2


def _bcast_heads(cols):
    rows = cols[0].shape[0]
    return jnp.concatenate([jnp.broadcast_to(c, (rows, DH)) for c in cols], axis=1)


def _sample_mix_kernel(proj_ref, x_ref, mod_ref, vec_ref, bif_ref, inv_ref,
                       wax_ref, wq_ref, wk_ref, wv_ref, wif_ref, wout_ref,
                       rgh_ref, rgc_ref, mlc_ref, mln_ref, mlm_ref, C_ref, S_ref, *rest):
    (y_ref, o_rgh, o_rgc, o_mlc, o_mln, o_mlm, o_C, o_S,
     mix_s, xcml_s, inter_s, dv_s, sv_s, rden_s, svr_s, vr_s, qC_s, qS_s,
     qT_s, kT_s, qrT_s, krT_s) = rest[-22:]
    i = pl.program_id(0)
    n_steps = pl.num_programs(0)
    R = x_ref.shape[0]

    def p(blk):
        v = proj_ref[:, blk * D_MODEL:(blk + 1) * D_MODEL]
        return 0.5 * v if blk in P_GATE_LIKE else v

    def conv(xin, cs_ref, o_state, w0, brow):
        cs = cs_ref[...]
        acc = vec_ref[brow:brow + 1, :] + vec_ref[w0 + 3:w0 + 4, :] * xin
        for k in range(CONV_W - 1):
            acc = acc + vec_ref[w0 + k:w0 + k + 1, :] * cs[:, k * D_MODEL:(k + 1) * D_MODEL]
        o_state[...] = jnp.concatenate([cs[:, D_MODEL:], xin], axis=1)
        return acc

    @pl.when(i == 0)
    def _dense():
        xc = conv(p(P_XRG), rgc_ref, o_rgc, V_RGCW, V_RGCB)
        a, bt = _rg_gates(xc, vec_ref, wax_ref)
        h = a * rgh_ref[...] + bt
        o_rgh[...] = h
        mix_s[...] = _sigmoid_of_twice(p(P_GA)) * (h * _silu_of_twice(p(P_ZRG)))

        x_ml = p(P_XML)
        xc_ml = _silu(conv(x_ml, mlc_ref, o_mlc, V_MLCW, V_MLCB))
        xcml_s[...] = xc_ml
        xhb = xc_ml.astype(_BF16)
        xmb = x_ml.astype(_BF16)
        qs, ks, vs = [], [], []
        for h_ in range(N_HEADS):
            sl = slice(h_ * DH, (h_ + 1) * DH)
            qs.append(_dot(xhb[:, sl], wq_ref[h_].astype(_BF16)))
            ks.append(_dot(xhb[:, sl], wk_ref[h_].astype(_BF16)) * (DH ** -0.5))
            vs.append(_dot(xmb[:, sl], wv_ref[h_].astype(_BF16)))
        q = jnp.concatenate(qs, axis=1)
        k = jnp.concatenate(ks, axis=1)
        v = jnp.concatenate(vs, axis=1)
        qkvb = jnp.concatenate([q, k, v], axis=1).astype(_BF16)
        gpre = _dot(qkvb, wif_ref[...]) + bif_ref[...]
        ig = gpre[:, :LANES]
        lf = _log_sigmoid(gpre[:, LANES:])
        m0 = mlm_ref[...]
        m = jnp.maximum(m0 + lf, ig)
        o_mlm[...] = m
        dg = jnp.exp(ig - m)
        inter = jnp.exp(lf + m0 - m)
        floor = jnp.exp(-m)
        n0 = mln_ref[...]
        s_cols, rden_cols = [], []
        for h_ in range(N_HEADS):
            sl = slice(h_ * DH, (h_ + 1) * DH)
            s_h = jnp.sum(q[:, sl] * k[:, sl], axis=-1, keepdims=True) * dg[:, h_:h_ + 1]
            qn = jnp.sum(q[:, sl] * n0[:, sl], axis=-1, keepdims=True)
            den = inter[:, h_:h_ + 1] * qn + s_h
            s_cols.append(s_h)
            rden_cols.append(1.0 / jnp.maximum(jnp.abs(den), floor[:, h_:h_ + 1]))
        dg_b = _bcast_heads([dg[:, h_:h_ + 1] for h_ in range(N_HEADS)])
        inter_b = _bcast_heads([inter[:, h_:h_ + 1] for h_ in range(N_HEADS)])
        inter_s[...] = inter_b
        dv_s[...] = dg_b * v
        sv_s[...] = _bcast_heads(s_cols) * v
        rden_s[...] = _bcast_heads(rden_cols)
        o_mln[...] = inter_b * n0 + dg_b * k
        qT_s[...] = q.T
        kT_s[...] = k.T

        pos = jnp.full((1, HALF), float(PAST_LEN), _F32)
        cos, sin = _rope_tables(pos, inv_ref[...])
        qr = jnp.concatenate(_rope(p(P_QR), cos, sin), axis=1)
        kr = jnp.concatenate(_rope(p(P_KR), cos, sin, DH ** -0.5), axis=1)
        v_r = p(P_VR)
        s_cols = [jnp.sum(qr[:, h_ * DH:(h_ + 1) * DH] * kr[:, h_ * DH:(h_ + 1) * DH],
                          axis=-1, keepdims=True) for h_ in range(N_HEADS)]
        svr_s[...] = _bcast_heads(s_cols) * v_r
        vr_s[...] = v_r
        qrT_s[...] = qr.T
        krT_s[...] = kr.T

    lane = lax.broadcasted_iota(jnp.int32, (DH, R), 1)
    for bi in range(S_BT):
        b = i * S_BT + bi
        onehot = (lane == b).astype(_F32)

        def column(t_ref, h_):
            return jnp.sum(t_ref[h_ * DH:(h_ + 1) * DH, :] * onehot, axis=1, keepdims=True)

        for h_ in range(N_HEADS):
            sl = slice(h_ * DH, (h_ + 1) * DH)
            C0 = C_ref[bi, h_]
            qC_s[pl.ds(b, 1), sl] = jnp.sum(C0 * column(qT_s, h_), axis=0, keepdims=True)
            o_C[bi, h_] = (inter_s[pl.ds(b, 1), sl] * C0
                           + column(kT_s, h_) * dv_s[pl.ds(b, 1), sl])
            S0 = S_ref[bi, h_]
            qS_s[pl.ds(b, 1), sl] = jnp.sum(S0 * column(qrT_s, h_), axis=0, keepdims=True)
            o_S[bi, h_] = (math.exp(_RET_LOG_GAMMA[h_]) * S0
                           + column(krT_s, h_) * vr_s[pl.ds(b, 1), sl])

    @pl.when(i == n_steps - 1)
    def _epilogue():
        h_ml = (inter_s[...] * qC_s[...] + sv_s[...]) * rden_s[...]
        hn = jnp.concatenate(
            [_headnorm(h_ml[:, h_ * DH:(h_ + 1) * DH]) for h_ in range(N_HEADS)], axis=1)
        y_b = hn * vec_ref[V_MLGN:V_MLGN + 1, :] + vec_ref[V_MLSKIP:V_MLSKIP + 1, :] * xcml_s[...]
        mix = mix_s[...] + _sigmoid_of_twice(p(P_GB)) * (y_b * _silu_of_twice(p(P_ZML)))
        cross = jnp.concatenate(
            [jnp.full((1, DH), math.exp(_RET_LOG_GAMMA[h_]), _F32) for h_ in range(N_HEADS)], axis=1)
        h_ret = svr_s[...] + qS_s[...] * cross
        hn = jnp.concatenate(
            [_headnorm(h_ret[:, h_ * DH:(h_ + 1) * DH]) for h_ in range(N_HEADS)], axis=1)
        mix = mix + _sigmoid_of_twice(p(P_GC)) * (
            hn * vec_ref[V_RETGN:V_RETGN + 1, :] * _silu_of_twice(p(P_GR)))
        out = _dot_chunks(mix.astype(_BF16), wout_ref, 0, D_MODEL // W_CHUNK)
        gate = mod_ref[:, 2 * D_MODEL:3 * D_MODEL]
        y_ref[...] = x_ref[...] + gate * (_rms(out) * vec_ref[V_NPOST:V_NPOST + 1, :])


def _sample_mix(layer, prev_big, proj, x, mod, vec, bif2, inv, wax, wq, wk, wv, wif2, wout,
                rgh, rgc, mlc, mln, mlm, mlC_all, retS_all):
    R = x.shape[0]
    c2 = lambda i: (0, 0)
    st5 = lambda i: (layer, i, 0, 0, 0)

    def resident(*shape):
        return pl.BlockSpec((None,) + shape, lambda i: (layer,) + (0,) * len(shape),
                            pipeline_mode=pl.Buffered(1))

    def resident2(*shape):
        return pl.BlockSpec(shape, c2, pipeline_mode=pl.Buffered(1))

    row_blk = resident2(R, D_MODEL)
    conv_blk = resident2(R, (CONV_W - 1) * D_MODEL)
    lane_blk = resident2(R, LANES)
    state_blk = pl.BlockSpec((None, S_BT, N_HEADS, DH, DH), st5)
    f32 = lambda *s: jax.ShapeDtypeStruct(s, _F32)
    n_in = 19
    outs = pl.pallas_call(
        _sample_mix_kernel,
        grid=(R // S_BT,),
        input_output_aliases={n_in: 6, n_in + 1: 7} if prev_big else {},
        in_specs=[
            resident2(R, D_IN),
            resident2(R, D_MODEL),
            resident(R, 3 * D_MODEL),
            resident(V_ROWS, D_MODEL),
            resident(1, 2 * LANES),
            resident2(1, HALF),
            resident(RG_HEADS, RG_BW, 2 * RG_BW),
            resident(N_HEADS, DH, DH),
            resident(N_HEADS, DH, DH),
            resident(N_HEADS, DH, DH),
            resident(3 * D_MODEL, 2 * LANES),
            resident(D_MODEL // W_CHUNK, D_MODEL, W_CHUNK),
            resident(R, D_MODEL),
            resident(R, (CONV_W - 1) * D_MODEL),
            resident(R, (CONV_W - 1) * D_MODEL),
            resident(R, D_MODEL),
            resident(R, LANES),
            state_blk,
            state_blk,
        ] + [pl.BlockSpec(memory_space=pl.ANY)] * len(prev_big),
        out_specs=[row_blk, row_blk, conv_blk, conv_blk, row_blk, lane_blk, state_blk, state_blk],
        out_shape=[
            f32(R, D_MODEL), f32(R, D_MODEL), f32(R, (CONV_W - 1) * D_MODEL),
            f32(R, (CONV_W - 1) * D_MODEL), f32(R, D_MODEL), f32(R, LANES),
            f32(DEPTH, R, N_HEADS, DH, DH), f32(DEPTH, R, N_HEADS, DH, DH),
        ],
        scratch_shapes=(
            [pltpu.VMEM((R, D_MODEL), _F32) for _ in range(10)]
            + [pltpu.VMEM((D_MODEL, R), _F32) for _ in range(4)]),
        compiler_params=pltpu.CompilerParams(
            dimension_semantics=("arbitrary",), vmem_limit_bytes=VMEM_LIMIT),
        name="sample_mix",
    )(proj, x, mod, vec, bif2, inv, wax, wq, wk, wv, wif2, wout,
      rgh, rgc, mlc, mln, mlm, mlC_all, retS_all, *prev_big)
    y, o_rgh, o_rgc, o_mlc, o_mln, o_mlm, o_C, o_S = outs
    return y, (o_rgh, o_rgc.reshape(R, CONV_W - 1, D_MODEL),
               o_mln.reshape(R, N_HEADS, DH), o_mlm[:, :N_HEADS],
               o_mlc.reshape(R, CONV_W - 1, D_MODEL)), (o_C, o_S)


def _pack_params(norm_pre, norm_post, rg_conv_w, rg_conv_b, rg_b_a, rg_b_x, rg_lambda,
                 ml_conv_w, ml_conv_b, ml_skip, ml_gn, ret_gn):
    row = lambda p: p[:, None, :]
    rows = [row(norm_pre), row(norm_post), rg_conv_w, row(rg_conv_b), row(0.5 * rg_b_a), row(0.5 * rg_b_x),
            row(rg_lambda), ml_conv_w, row(ml_conv_b), row(ml_skip), row(ml_gn), row(ret_gn)]
    vec = jnp.concatenate(rows, axis=1)
    return jnp.pad(vec, ((0, 0), (0, V_ROWS - vec.shape[1]), (0, 0)))


CAST_CHUNKS = N_PROJ


def _cast_chunks_kernel(w_ref, o_ref):
    for c in range(o_ref.shape[0]):
        o_ref[c] = w_ref[:, c * W_CHUNK:(c + 1) * W_CHUNK].astype(_BF16)


def _column_chunks(w):
    depth, K, N = w.shape
    n_chunks = N // W_CHUNK
    per_step = math.gcd(n_chunks, CAST_CHUNKS)
    return pl.pallas_call(
        _cast_chunks_kernel,
        grid=(depth, n_chunks // per_step),
        in_specs=[pl.BlockSpec((None, K, per_step * W_CHUNK), lambda l, s: (l, 0, s))],
        out_specs=pl.BlockSpec((None, per_step, K, W_CHUNK), lambda l, s: (l, s, 0, 0)),
        out_shape=jax.ShapeDtypeStruct((depth, n_chunks, K, W_CHUNK), _BF16),
        compiler_params=pltpu.CompilerParams(vmem_limit_bytes=VMEM_LIMIT),
        name="cast_chunks",
    )(w)


def kernel(x_prompt, x_sample, c_prompt, c_sample, state_rg_h, state_rg_conv, state_ml_C, state_ml_n, state_ml_m, state_ml_conv, state_ret_S, w_ada, b_ada, norm_pre, norm_post, w_in, rg_conv_w, rg_conv_b, rg_w_a, rg_b_a, rg_w_x, rg_b_x, rg_lambda, ml_conv_w, ml_conv_b, ml_w_q, ml_w_k, ml_w_v, ml_w_if, ml_b_if, ml_skip, ml_gn, ret_gn, w_out):
    xs = x_sample[:, 0, :]
    R = xs.shape[0]
    mod_all = _ada_call(jnp.concatenate([c_sample, c_prompt], axis=0), w_ada, b_ada)
    inv = (ROPE_BASE ** (-jnp.linspace(0.0, 1.0, HALF, dtype=_F32)))[None, :]
    vec = _pack_params(norm_pre, norm_post, rg_conv_w, rg_conv_b, rg_b_a, rg_b_x,
                       rg_lambda, ml_conv_w, ml_conv_b, ml_skip, ml_gn, ret_gn)
    win = _column_chunks(w_in)
    wout = _column_chunks(w_out)
    wax = (0.5 * jnp.concatenate([rg_w_a, rg_w_x], axis=-1)).astype(_BF16)
    wq, wk, wv = ml_w_q, ml_w_k, ml_w_v
    lane_pad = ((0, 0), (0, 0), (0, LANES - N_HEADS))
    wif2 = jnp.concatenate([jnp.pad(ml_w_if[..., :N_HEADS], lane_pad),
                            jnp.pad(ml_w_if[..., N_HEADS:], lane_pad)], axis=-1).astype(_BF16)
    b_if = ml_b_if[:, None, :]
    bif2 = jnp.concatenate([jnp.pad(b_if[..., :N_HEADS], lane_pad),
                            jnp.pad(b_if[..., N_HEADS:], lane_pad)], axis=-1)
    s_rgc_in = state_rg_conv.reshape(DEPTH, R, -1)
    s_mlc_in = state_ml_conv.reshape(DEPTH, R, -1)
    s_mln_in = state_ml_n.reshape(DEPTH, R, -1)
    s_mlm_in = jnp.pad(state_ml_m, lane_pad)

    xp = _interleave_tiles(x_prompt)
    st_p, st_s = [], []
    big_p, big_s = (), ()
    for l in range(DEPTH):
        xp, sp, big_p = _prompt_layer(l, big_p, R, xp, mod_all, vec, bif2, inv, win, wax, wq, wk, wv, wif2, wout)
        st_p.append(sp)
        proj = _sample_proj(l, xs, mod_all, vec, win)
        xs, ss, big_s = _sample_mix(
            l, big_s, proj, xs, mod_all, vec, bif2, inv, wax, wq, wk, wv, wif2, wout,
            state_rg_h, s_rgc_in, s_mlc_in, s_mln_in, s_mlm_in, state_ml_C, state_ret_S)
        st_s.append(ss)
    p_rgh, p_rgc, p_n, p_m, p_mlc = [jnp.stack(a) for a in zip(*st_p)]
    s_rgh, s_rgc, s_n, s_m, s_mlc = [jnp.stack(a) for a in zip(*st_s)]
    return (_deinterleave_tiles(xp), xs[:, None, :], p_rgh, p_rgc, big_p[0], p_n, p_m, p_mlc, big_p[1],
            s_rgh, s_rgc, big_s[0], s_n, s_m, s_mlc, big_s[1])
```

```python
import math

import jax
import jax.numpy as jnp
from jax import lax
from jax.experimental import pallas as pl
from jax.experimental.pallas import tpu as pltpu

D_MODEL = 1024
DEPTH = 2
CONV_W = 4
RG_HEADS = 8
RG_BW = D_MODEL // RG_HEADS
RG_C = 8.0
N_HEADS = 4
DH = D_MODEL // N_HEADS
HALF = DH // 2
PAST_LEN = 16384
ROPE_BASE = 10000.0
EPS = 1e-6
N_PROJ = 11
D_IN = N_PROJ * D_MODEL
P_XRG, P_ZRG, P_XML, P_ZML, P_QR, P_KR, P_VR, P_GR, P_GA, P_GB, P_GC = range(N_PROJ)
P_GATE_LIKE = (P_ZRG, P_ZML, P_GR, P_GA, P_GB, P_GC)

V_NPRE, V_NPOST, V_RGCW, V_RGCB, V_RGBA, V_RGBX, V_RGLAM = 0, 1, 2, 6, 7, 8, 9
V_MLCW, V_MLCB, V_MLSKIP, V_MLGN, V_RETGN, V_ROWS = 10, 14, 15, 16, 17, 24

SUBLANES = 8
LANES = 128
MXU_N = 256
W_CHUNK = MXU_N
T_TILE = 256
SEG = T_TILE // SUBLANES
N_GROUPS = T_TILE // SUBLANES
VMEM_LIMIT = 60 * 1024 * 1024
PREFETCH_START = 1
PREFETCH_B1 = (1, 1, 1, 1)
PREFETCH_GATES = 1
PREFETCH_A = 4
PREFETCH_B2 = 3
PREFETCH_C = 2

_RET_LOG_GAMMA = tuple(math.log1p(-2.0 ** (-5.0 - h)) for h in range(N_HEADS))

_NT = (((1,), (1,)), ((), ()))
_TN = (((0,), (0,)), ((), ()))
_F32 = jnp.float32
_BF16 = jnp.bfloat16


def _dot(a, b):
    return jnp.dot(a, b, preferred_element_type=_F32)


def _dot_chunks(a, w_ref, first, n):
    return jnp.concatenate([_dot(a, w_ref[first + c]) for c in range(n)], axis=1)


def _sigmoid_of_twice(hx):
    return 0.5 * jnp.tanh(hx) + 0.5


def _silu_of_twice(hx):
    return hx * jnp.tanh(hx) + hx


def _silu(x):
    return _silu_of_twice(0.5 * x)


def _softplus(x):
    return jnp.maximum(x, 0.0) + jnp.log1p(jnp.exp(-jnp.abs(x)))


def _log_sigmoid(x):
    return -_softplus(-x)


def _neg_expm1_2x(x):
    t = jnp.tanh(x)
    return (-2.0 * t) / (1.0 - t)


def _rms(x):
    return x * lax.rsqrt(jnp.mean(x * x, axis=-1, keepdims=True) + EPS)


def _headnorm(h):
    mu = jnp.mean(h, axis=-1, keepdims=True)
    c = h - mu
    var = jnp.mean(c * c, axis=-1, keepdims=True)
    return c * lax.rsqrt(var + EPS)


def _rg_gates(xc, vec_ref, wax_ref, lane0=0):
    xcb = xc.astype(_BF16)
    a_parts, b_parts = [], []
    for hh in range(xc.shape[1] // RG_BW):
        loc = slice(hh * RG_BW, (hh + 1) * RG_BW)
        sl = slice(lane0 + hh * RG_BW, lane0 + (hh + 1) * RG_BW)
        g = _dot(xcb[:, loc], wax_ref[sl.start // RG_BW])
        r = _sigmoid_of_twice(g[:, :RG_BW] + vec_ref[V_RGBA:V_RGBA + 1, sl])
        i = _sigmoid_of_twice(g[:, RG_BW:] + vec_ref[V_RGBX:V_RGBX + 1, sl])
        log_a = (-RG_C) * r * _softplus(-vec_ref[V_RGLAM:V_RGLAM + 1, sl])
        a_parts.append(jnp.exp(log_a))
        b_parts.append(jnp.sqrt(_neg_expm1_2x(log_a)) * (i * xc[:, loc]))
    return jnp.concatenate(a_parts, axis=1), jnp.concatenate(b_parts, axis=1)


def _rope_tables(pos, inv_row):
    ang = pos * inv_row
    return jnp.cos(ang), jnp.sin(ang)


def _rope(x, cos, sin, scale=None):
    out = []
    for h in range(N_HEADS):
        x1 = x[:, h * DH:h * DH + HALF]
        x2 = x[:, h * DH + HALF:(h + 1) * DH]
        r = jnp.concatenate([x1 * cos - x2 * sin, x1 * sin + x2 * cos], axis=1)
        out.append(r if scale is None else r * scale)
    return out


def _token_of_row(i):
    return (i & (SUBLANES - 1)) * SEG + (i >> 3)


def _split3_bf16(x):
    p0 = x.astype(_BF16)
    r = x - p0.astype(_F32)
    p1 = r.astype(_BF16)
    p2 = (r - p1.astype(_F32)).astype(_BF16)
    return p0, p1, p2


def _ada_kernel(c_ref, w_ref, b_ref, o_ref):
    c = c_ref[...]
    w = w_ref[...]
    c_hi = c.astype(_BF16)
    c_lo = (c - c_hi.astype(_F32)).astype(_BF16)
    w_hi = w.astype(_BF16)
    o_ref[...] = _dot(c_hi, w_hi) + _dot(c_lo, w_hi) + b_ref[...]


def _ada_call(c_all, w_ada, b_ada):
    rows = c_all.shape[0]
    return pl.pallas_call(
        _ada_kernel,
        grid=(DEPTH, 3),
        in_specs=[
            pl.BlockSpec((rows, D_MODEL), lambda l, n: (0, 0)),
            pl.BlockSpec((None, D_MODEL, D_MODEL), lambda l, n: (l, 0, n)),
            pl.BlockSpec((None, 1, D_MODEL), lambda l, n: (l, 0, n)),
        ],
        out_specs=pl.BlockSpec((None, rows, D_MODEL), lambda l, n: (l, 0, n)),
        out_shape=jax.ShapeDtypeStruct((DEPTH, rows, 3 * D_MODEL), _F32),
        name="adaln",
    )(c_all, w_ada, b_ada.reshape(DEPTH, 1, 3 * D_MODEL))


def _shifted_conv(xin, cs, carry_ref, vec_ref, w0, brow, o_state):
    T = T_TILE
    tail = (CONV_W - 1) * SUBLANES
    prev = carry_ref[:, cs]
    cur = xin[T - tail:T, :]
    sub = lax.broadcasted_iota(jnp.int32, (SUBLANES, xin.shape[1]), 0)
    lead = []
    for q in range(CONV_W - 1):
        sl = slice(q * SUBLANES, (q + 1) * SUBLANES)
        lead.append(jnp.where(sub == 0, pltpu.roll(prev[sl], 1, axis=0), pltpu.roll(cur[sl], 1, axis=0)))
    ext = jnp.concatenate(lead + [xin], axis=0)
    acc = vec_ref[brow:brow + 1, cs] + vec_ref[w0 + 3:w0 + 4, cs] * xin
    for d in range(1, CONV_W):
        acc = acc + vec_ref[w0 + 3 - d:w0 + 4 - d, cs] * ext[tail - SUBLANES * d:tail - SUBLANES * d + T, :]
    carry_ref[:, cs] = cur
    o_state[0, :, cs] = jnp.concatenate(
        [cur[q * SUBLANES + SUBLANES - 1:(q + 1) * SUBLANES, :] for q in range(CONV_W - 1)], axis=0)
    return acc


def _segment_scan(a, bt, h0):
    hs, cums = [], []
    h = cum = None
    for r in range(N_GROUPS):
        sl = slice(r * SUBLANES, (r + 1) * SUBLANES)
        if r == 0:
            h, cum = bt[sl], a[sl]
        else:
            h, cum = a[sl] * h + bt[sl], a[sl] * cum
        hs.append(h)
        cums.append(cum)
    carry = h0
    carries = [carry]
    for s in range(SUBLANES):
        carry = cum[s:s + 1, :] * carry + h[s:s + 1, :]
        carries.append(carry)
    carry_in = jnp.concatenate(carries[:SUBLANES], axis=0)
    out = jnp.concatenate([hs[r] + cums[r] * carry_in for r in range(N_GROUPS)], axis=0)
    return out, carries[SUBLANES]


def _prompt_kernel(x_ref, mod_ref, vec_ref, bif_ref, inv_ref,
                   win_ref, wax_ref, wq_ref, wk_ref, wv_ref, wif_ref, wout_ref, *rest):
    (y_ref, o_rgh, o_rgc, o_C, o_n, o_m, o_mlc, o_S,
     cc_rg, cc_ml, qkv_s, xcml_s, mix_s, m_s, cos_s, sin_s,
     mask_s, tri_s, dret_s, cross_s, wk_s) = rest[-21:]
    T = T_TILE
    b = pl.program_id(0)
    j = pl.program_id(1)

    @pl.when(jnp.logical_and(b == 0, j == 0))
    def _init_tables():
        tr = _token_of_row(lax.broadcasted_iota(jnp.int32, (T, T), 0))
        tc = _token_of_row(lax.broadcasted_iota(jnp.int32, (T, T), 1))
        causal = tr >= tc
        mask_s[...] = jnp.where(causal, 0.0, -jnp.inf)
        tri_s[...] = jnp.where(causal, 1.0, 0.0).astype(_BF16)
        rel = (tr - tc).astype(_F32)
        tokf = _token_of_row(lax.broadcasted_iota(jnp.int32, (T, DH), 0)).astype(_F32)
        for h in range(N_HEADS):
            lg = _RET_LOG_GAMMA[h]
            dret_s[h] = jnp.where(causal, jnp.exp(rel * lg), 0.0)
            cross_s[h] = jnp.exp((tokf + 1.0) * lg)
            wk_s[h] = jnp.exp((T - 1.0 - tokf) * lg)

    @pl.when(b == 0)
    def _init_rope():
        tok = _token_of_row(lax.broadcasted_iota(jnp.int32, (T, HALF), 0))
        c, s = _rope_tables((j * T + tok).astype(_F32), inv_ref[...])
        cos_s[pl.ds(pl.multiple_of(j * T, T), T), :] = c
        sin_s[pl.ds(pl.multiple_of(j * T, T), T), :] = s

    @pl.when(j == 0)
    def _init_state():
        o_rgh[...] = jnp.zeros_like(o_rgh)
        o_C[...] = jnp.zeros_like(o_C)
        o_n[...] = jnp.zeros_like(o_n)
        o_S[...] = jnp.zeros_like(o_S)
        m_s[...] = jnp.zeros_like(m_s)
        cc_rg[...] = jnp.zeros_like(cc_rg)
        cc_ml[...] = jnp.zeros_like(cc_ml)

    x = x_ref[0]
    shift = mod_ref[pl.ds(b, 1), 0:D_MODEL]
    scale = mod_ref[pl.ds(b, 1), D_MODEL:2 * D_MODEL]
    gate = mod_ref[pl.ds(b, 1), 2 * D_MODEL:3 * D_MODEL]
    u = _rms(x) * (vec_ref[V_NPRE:V_NPRE + 1, :] * (1.0 + scale)) + shift
    ub = u.astype(_BF16)
    n_chunk = D_MODEL // MXU_N
    n_wide = D_MODEL // W_CHUNK

    pending = [(P_XML, w) for w in range(n_wide)]
    pending += [(blk, w) for w in range(n_wide) for blk in (P_XRG, P_ZRG, P_GA)]
    pending += [(blk, w) for w in range(n_wide) for blk in (P_ZML, P_GB)]
    pending += [(blk, w) for w in range(n_wide) for blk in (P_QR, P_KR, P_VR, P_GR, P_GC)]
    ready = {}

    def prefetch(n):
        for _ in range(min(n, len(pending))):
            blk, w = pending.pop(0)
            ready[(blk, w)] = _dot(ub, win_ref[blk * n_wide + w])

    def proj(blk, c):
        per_wide = W_CHUNK // MXU_N
        key = (blk, c // per_wide)
        while key not in ready:
            prefetch(1)
        lo = (c % per_wide) * MXU_N
        v = ready[key][:, lo:lo + MXU_N]
        return 0.5 * v if blk in P_GATE_LIKE else v

    prefetch(PREFETCH_START)
    for h in range(N_HEADS):
        cs = slice(h * DH, (h + 1) * DH)
        x_ml = proj(P_XML, h)
        prefetch(PREFETCH_B1[h])
        xc_ml = _silu(_shifted_conv(x_ml, cs, cc_ml, vec_ref, V_MLCW, V_MLCB, o_mlc))
        xcml_s[:, cs] = xc_ml
        xhb = xc_ml.astype(_BF16)
        prefetch(1)
        q = _dot(xhb, wq_ref[h].astype(_BF16)).astype(_BF16)
        k = (_dot(xhb, wk_ref[h].astype(_BF16)) * (DH ** -0.5)).astype(_BF16)
        v = _dot(x_ml.astype(_BF16), wv_ref[h].astype(_BF16)).astype(_BF16)
        qkv_s[:, cs] = q
        qkv_s[:, D_MODEL + h * DH:D_MODEL + (h + 1) * DH] = k
        qkv_s[:, 2 * D_MODEL + h * DH:2 * D_MODEL + (h + 1) * DH] = v
    mask = mask_s[...]

    def gates_sums():
        gpre = jnp.concatenate([_dot(qkv_s[0:T // 2, :], wif_ref[...]),
                                _dot(qkv_s[T // 2:T, :], wif_ref[...])], axis=0) + bif_ref[...]
        lf = _log_sigmoid(gpre[:, LANES:])
        prefetch(1)
        F = sum(_dot(tri_s[...], part) for part in _split3_bf16(lf))
        uu = gpre[:, :LANES] - F
        return F, uu, uu.T

    def gates_running_max(uu_rows):
        lane = lax.broadcasted_iota(jnp.int32, (T, LANES), 1)
        cmax = jnp.zeros((T, LANES), _F32)
        for h in range(N_HEADS):
            cm_h = jnp.max(uu_rows[h:h + 1, :] + mask, axis=1, keepdims=True)
            cmax = jnp.where(lane == h, cm_h, cmax)
        return cmax

    def gates_columns(F, uu, cmax):
        m0 = m_s[0:1, :]
        mrun = jnp.maximum(m0, cmax)
        m = F + mrun
        mL = m[T - 1:T, :]
        FL = F[T - 1:T, :]
        m_s[...] = jnp.broadcast_to(mL, m_s.shape)
        o_m[0] = m_s[...]
        return (-mrun,
                jnp.exp(m0 - mrun),
                jnp.exp(-m),
                jnp.exp(FL - mL + uu),
                jnp.broadcast_to(jnp.exp(FL + m0 - mL), (T, LANES)))

    prefetch(PREFETCH_GATES)
    F, uu, uu_rows = gates_sums()
    for c in range(n_chunk):
        cs = slice(c * MXU_N, (c + 1) * MXU_N)
        x_rg = proj(P_XRG, c)
        prefetch(2)
        xc = _shifted_conv(x_rg, cs, cc_rg, vec_ref, V_RGCW, V_RGCB, o_rgc)
        prefetch(PREFETCH_A - 2)
        a, bt = _rg_gates(xc, vec_ref, wax_ref, c * MXU_N)
        h_rg, h_last = _segment_scan(a, bt, o_rgh[0, :, cs])
        o_rgh[0, :, cs] = h_last
        mix_s[:, cs] = _sigmoid_of_twice(proj(P_GA, c)) * (h_rg * _silu_of_twice(proj(P_ZRG, c)))
        if c == 0:
            cmax = gates_running_max(uu_rows)
        if c == 1:
            c_row, c_inter, c_floor, c_wl, c_decay = gates_columns(F, uu, cmax)

    for h in range(N_HEADS):
        cs = slice(h * DH, (h + 1) * DH)
        prefetch(1)
        qh = qkv_s[:, h * DH:(h + 1) * DH]
        kh = qkv_s[:, D_MODEL + h * DH:D_MODEL + (h + 1) * DH]
        vh = qkv_s[:, 2 * D_MODEL + h * DH:2 * D_MODEL + (h + 1) * DH]
        col = lambda c: c[:, h:h + 1]
        dmat = jnp.exp(col(c_row) + (uu_rows[h:h + 1, :] + mask))
        s = lax.dot_general(qh, kh, _NT, preferred_element_type=_F32) * dmat
        C0 = o_C[0, h]
        n0 = o_n[0, h:h + 1, :]
        prefetch(1)
        num = col(c_inter) * _dot(qh, C0.astype(_BF16)) + _dot(s.astype(_BF16), vh)
        den = (col(c_inter) * jnp.sum(qh.astype(_F32) * n0, axis=-1, keepdims=True)
               + jnp.sum(s, axis=-1, keepdims=True))
        h_ml = num * (1.0 / jnp.maximum(jnp.abs(den), col(c_floor)))
        kw = kh.astype(_F32) * col(c_wl)
        prefetch(PREFETCH_B2 - 2)
        o_C[0, h] = col(c_decay) * C0 + lax.dot_general(
            kw.astype(_BF16), vh, _TN, preferred_element_type=_F32)
        o_n[0, h:h + 1, :] = col(c_decay)[0:1, :] * n0 + jnp.sum(kw, axis=0, keepdims=True)
        y_b = (_headnorm(h_ml) * vec_ref[V_MLGN:V_MLGN + 1, cs]
               + vec_ref[V_MLSKIP:V_MLSKIP + 1, cs] * xcml_s[:, cs])
        mix_s[:, cs] += _sigmoid_of_twice(proj(P_GB, h)) * (y_b * _silu_of_twice(proj(P_ZML, h)))

    t0 = pl.multiple_of(j * T, T)
    cos = cos_s[pl.ds(t0, T), :]
    sin = sin_s[pl.ds(t0, T), :]

    def rope(xh):
        x1, x2 = xh[:, :HALF], xh[:, HALF:]
        return jnp.concatenate([x1 * cos - x2 * sin, x1 * sin + x2 * cos], axis=1)

    for h in range(N_HEADS):
        cs = slice(h * DH, (h + 1) * DH)
        qh = rope(proj(P_QR, h)).astype(_BF16)
        kf = rope(proj(P_KR, h)) * (DH ** -0.5)
        vh = proj(P_VR, h).astype(_BF16)
        prefetch(1)
        s = lax.dot_general(qh, kf.astype(_BF16), _NT, preferred_element_type=_F32) * dret_s[h]
        S0 = o_S[0, h]
        prefetch(PREFETCH_C - 1)
        h_ret = _dot(s.astype(_BF16), vh) + _dot(qh, S0.astype(_BF16)) * cross_s[h]
        kw = (kf * wk_s[h]).astype(_BF16)
        o_S[0, h] = math.exp(T * _RET_LOG_GAMMA[h]) * S0 + lax.dot_general(
            kw, vh, _TN, preferred_element_type=_F32)
        y_c = _headnorm(h_ret) * vec_ref[V_RETGN:V_RETGN + 1, cs] * _silu_of_twice(proj(P_GR, h))
        mix_s[:, cs] += _sigmoid_of_twice(proj(P_GC, h)) * y_c

    out = _dot_chunks(mix_s[...].astype(_BF16), wout_ref, 0, n_wide)
    y_ref[0] = x + gate * (_rms(out) * vec_ref[V_NPOST:V_NPOST + 1, :])


def _prompt_layer(layer, prev_big, mod_row0, x, mod, vec, bif2, inv, win, wax, wq, wk, wv, wif2, wout):
    B, L, _ = x.shape
    T = T_TILE
    per_b3 = lambda b, j: (b, 0, 0)
    big_blk = pl.BlockSpec((None, 1, N_HEADS, DH, DH), lambda b, j: (layer, b, 0, 0, 0))
    big_shape = jax.ShapeDtypeStruct((DEPTH, B, N_HEADS, DH, DH), _F32)

    def resident(*shape):
        return pl.BlockSpec((None,) + shape, lambda b, j: (layer,) + (0,) * len(shape),
                            pipeline_mode=pl.Buffered(1))

    n_in = 12
    extra_specs = [pl.BlockSpec(memory_space=pl.ANY)] * len(prev_big)
    aliases = {n_in: 3, n_in + 1: 7} if prev_big else {}
    tail = (CONV_W - 1) * SUBLANES
    outs = pl.pallas_call(
        _prompt_kernel,
        grid=(B, L // T),
        input_output_aliases=aliases,
        in_specs=[
            pl.BlockSpec((1, T, D_MODEL), lambda b, j: (b, j, 0)),
            pl.BlockSpec((None, B, 3 * D_MODEL), lambda b, j: (layer, mod_row0 // B, 0),
                         pipeline_mode=pl.Buffered(1)),
            resident(V_ROWS, D_MODEL),
            resident(1, 2 * LANES),
            pl.BlockSpec((1, HALF), lambda b, j: (0, 0), pipeline_mode=pl.Buffered(1)),
            resident(D_IN // W_CHUNK, D_MODEL, W_CHUNK),
            resident(RG_HEADS, RG_BW, 2 * RG_BW),
            resident(N_HEADS, DH, DH),
            resident(N_HEADS, DH, DH),
            resident(N_HEADS, DH, DH),
            resident(3 * D_MODEL, 2 * LANES),
            resident(D_MODEL // W_CHUNK, D_MODEL, W_CHUNK),
        ] + extra_specs,
        out_specs=[
            pl.BlockSpec((1, T, D_MODEL), lambda b, j: (b, j, 0)),
            pl.BlockSpec((1, 1, D_MODEL), per_b3),
            pl.BlockSpec((1, CONV_W - 1, D_MODEL), per_b3),
            big_blk,
            pl.BlockSpec((1, N_HEADS, DH), per_b3),
            pl.BlockSpec((1, SUBLANES, LANES), per_b3),
            pl.BlockSpec((1, CONV_W - 1, D_MODEL), per_b3),
            big_blk,
        ],
        out_shape=[
            jax.ShapeDtypeStruct((B, L, D_MODEL), _F32),
            jax.ShapeDtypeStruct((B, 1, D_MODEL), _F32),
            jax.ShapeDtypeStruct((B, CONV_W - 1, D_MODEL), _F32),
            big_shape,
            jax.ShapeDtypeStruct((B, N_HEADS, DH), _F32),
            jax.ShapeDtypeStruct((B, SUBLANES, LANES), _F32),
            jax.ShapeDtypeStruct((B, CONV_W - 1, D_MODEL), _F32),
            big_shape,
        ],
        scratch_shapes=[
            pltpu.VMEM((tail, D_MODEL), _F32),
            pltpu.VMEM((tail, D_MODEL), _F32),
            pltpu.VMEM((T, 3 * D_MODEL), _BF16),
            pltpu.VMEM((T, D_MODEL), _F32),
            pltpu.VMEM((T, D_MODEL), _F32),
            pltpu.VMEM((SUBLANES, LANES), _F32),
            pltpu.VMEM((L, HALF), _F32),
            pltpu.VMEM((L, HALF), _F32),
            pltpu.VMEM((T, T), _F32),
            pltpu.VMEM((T, T), _BF16),
            pltpu.VMEM((N_HEADS, T, T), _F32),
            pltpu.VMEM((N_HEADS, T, DH), _F32),
            pltpu.VMEM((N_HEADS, T, DH), _F32),
        ],
        compiler_params=pltpu.CompilerParams(
            dimension_semantics=("arbitrary", "arbitrary"),
            vmem_limit_bytes=VMEM_LIMIT),
        name="prompt_layer",
    )(x, mod, vec, bif2, inv, win, wax, wq, wk, wv, wif2, wout, *prev_big)
    y, rgh, rgc, mlC, mln, mlm, mlc, retS = outs
    return y, (rgh[:, 0], rgc, mln, mlm[:, 0, :N_HEADS], mlc), (mlC, retS)


def _interleave_tiles(x):
    B, L, D = x.shape
    return x.reshape(B, L // T_TILE, SUBLANES, SEG, D).transpose(0, 1, 3, 2, 4).reshape(B, L, D)


def _deinterleave_tiles(x):
    B, L, D = x.shape
    return x.reshape(B, L // T_TILE, SEG, SUBLANES, D).transpose(0, 1, 3, 2, 4).reshape(B, L, D)


def _sample_proj_kernel(x_ref, mod_ref, vec_ref, w_ref, o_ref):
    scale = mod_ref[:, D_MODEL:2 * D_MODEL]
    u = _rms(x_ref[...]) * (vec_ref[V_NPRE:V_NPRE + 1, :] * (1.0 + scale)) + mod_ref[:, 0:D_MODEL]
    o_ref[...] = _dot_chunks(u.astype(_BF16), w_ref, 0, w_ref.shape[0])


def _sample_proj(layer, x, mod, vec, win):
    rows = x.shape[0]
    n_steps = D_MODEL // W_CHUNK
    return pl.pallas_call(
        _sample_proj_kernel,
        grid=(n_steps,),
        in_specs=[
            pl.BlockSpec((rows, D_MODEL), lambda n: (0, 0)),
            pl.BlockSpec((None, rows, 3 * D_MODEL), lambda n: (layer, 0, 0)),
            pl.BlockSpec((None, V_ROWS, D_MODEL), lambda n: (layer, 0, 0)),
            pl.BlockSpec((None, N_PROJ, D_MODEL, W_CHUNK), lambda n: (layer, n, 0, 0)),
        ],
        out_specs=pl.BlockSpec((rows, N_PROJ * W_CHUNK), lambda n: (0, n)),
        out_shape=jax.ShapeDtypeStruct((rows, D_IN), _F32),
        compiler_params=pltpu.CompilerParams(vmem_limit_bytes=VMEM_LIMIT),
        name="sample_proj",
    )(x, mod, vec, win)


S_BT = 2


def _bcast_heads(cols):
    rows = cols[0].shape[0]
    return jnp.concatenate([jnp.broadcast_to(c, (rows, DH)) for c in cols], axis=1)


def _sample_mix_kernel(proj_ref, x_ref, mod_ref, vec_ref, bif_ref, inv_ref,
                       wax_ref, wq_ref, wk_ref, wv_ref, wif_ref, wout_ref,
                       rgh_ref, rgc_ref, mlc_ref, mln_ref, mlm_ref, C_ref, S_ref, *rest):
    (y_ref, o_rgh, o_rgc, o_mlc, o_mln, o_mlm, o_C, o_S,
     mix_s, xcml_s, inter_s, dv_s, sv_s, rden_s, svr_s, vr_s, qC_s, qS_s,
     qT_s, kT_s, qrT_s, krT_s) = rest[-22:]
    i = pl.program_id(0)
    n_steps = pl.num_programs(0)
    R = x_ref.shape[0]

    def p(blk):
        v = proj_ref[:, blk * D_MODEL:(blk + 1) * D_MODEL]
        return 0.5 * v if blk in P_GATE_LIKE else v

    def conv(xin, cs_ref, o_state, w0, brow):
        cs = cs_ref[...]
        acc = vec_ref[brow:brow + 1, :] + vec_ref[w0 + 3:w0 + 4, :] * xin
        for k in range(CONV_W - 1):
            acc = acc + vec_ref[w0 + k:w0 + k + 1, :] * cs[:, k * D_MODEL:(k + 1) * D_MODEL]
        o_state[...] = jnp.concatenate([cs[:, D_MODEL:], xin], axis=1)
        return acc

    @pl.when(i == 0)
    def _dense():
        xc = conv(p(P_XRG), rgc_ref, o_rgc, V_RGCW, V_RGCB)
        a, bt = _rg_gates(xc, vec_ref, wax_ref)
        h = a * rgh_ref[...] + bt
        o_rgh[...] = h
        mix_s[...] = _sigmoid_of_twice(p(P_GA)) * (h * _silu_of_twice(p(P_ZRG)))

        x_ml = p(P_XML)
        xc_ml = _silu(conv(x_ml, mlc_ref, o_mlc, V_MLCW, V_MLCB))
        xcml_s[...] = xc_ml
        xhb = xc_ml.astype(_BF16)
        xmb = x_ml.astype(_BF16)
        qs, ks, vs = [], [], []
        for h_ in range(N_HEADS):
            sl = slice(h_ * DH, (h_ + 1) * DH)
            qs.append(_dot(xhb[:, sl], wq_ref[h_].astype(_BF16)))
            ks.append(_dot(xhb[:, sl], wk_ref[h_].astype(_BF16)) * (DH ** -0.5))
            vs.append(_dot(xmb[:, sl], wv_ref[h_].astype(_BF16)))
        q = jnp.concatenate(qs, axis=1)
        k = jnp.concatenate(ks, axis=1)
        v = jnp.concatenate(vs, axis=1)
        qkvb = jnp.concatenate([q, k, v], axis=1).astype(_BF16)
        gpre = _dot(qkvb, wif_ref[...]) + bif_ref[...]
        ig = gpre[:, :LANES]
        lf = _log_sigmoid(gpre[:, LANES:])
        m0 = mlm_ref[...]
        m = jnp.maximum(m0 + lf, ig)
        o_mlm[...] = m
        dg = jnp.exp(ig - m)
        inter = jnp.exp(lf + m0 - m)
        floor = jnp.exp(-m)
        n0 = mln_ref[...]
        s_cols, rden_cols = [], []
        for h_ in range(N_HEADS):
            sl = slice(h_ * DH, (h_ + 1) * DH)
            s_h = jnp.sum(q[:, sl] * k[:, sl], axis=-1, keepdims=True) * dg[:, h_:h_ + 1]
            qn = jnp.sum(q[:, sl] * n0[:, sl], axis=-1, keepdims=True)
            den = inter[:, h_:h_ + 1] * qn + s_h
            s_cols.append(s_h)
            rden_cols.append(1.0 / jnp.maximum(jnp.abs(den), floor[:, h_:h_ + 1]))
        dg_b = _bcast_heads([dg[:, h_:h_ + 1] for h_ in range(N_HEADS)])
        inter_b = _bcast_heads([inter[:, h_:h_ + 1] for h_ in range(N_HEADS)])
        inter_s[...] = inter_b
        dv_s[...] = dg_b * v
        sv_s[...] = _bcast_heads(s_cols) * v
        rden_s[...] = _bcast_heads(rden_cols)
        o_mln[...] = inter_b * n0 + dg_b * k
        qT_s[...] = q.T
        kT_s[...] = k.T

        pos = jnp.full((1, HALF), float(PAST_LEN), _F32)
        cos, sin = _rope_tables(pos, inv_ref[...])
        qr = jnp.concatenate(_rope(p(P_QR), cos, sin), axis=1)
        kr = jnp.concatenate(_rope(p(P_KR), cos, sin, DH ** -0.5), axis=1)
        v_r = p(P_VR)
        s_cols = [jnp.sum(qr[:, h_ * DH:(h_ + 1) * DH] * kr[:, h_ * DH:(h_ + 1) * DH],
                          axis=-1, keepdims=True) for h_ in range(N_HEADS)]
        svr_s[...] = _bcast_heads(s_cols) * v_r
        vr_s[...] = v_r
        qrT_s[...] = qr.T
        krT_s[...] = kr.T

    lane = lax.broadcasted_iota(jnp.int32, (DH, R), 1)
    for bi in range(S_BT):
        b = i * S_BT + bi
        onehot = (lane == b).astype(_F32)

        def column(t_ref, h_):
            return jnp.sum(t_ref[h_ * DH:(h_ + 1) * DH, :] * onehot, axis=1, keepdims=True)

        for h_ in range(N_HEADS):
            sl = slice(h_ * DH, (h_ + 1) * DH)
            C0 = C_ref[bi, h_]
            qC_s[pl.ds(b, 1), sl] = jnp.sum(C0 * column(qT_s, h_), axis=0, keepdims=True)
            o_C[bi, h_] = (inter_s[pl.ds(b, 1), sl] * C0
                           + column(kT_s, h_) * dv_s[pl.ds(b, 1), sl])
            S0 = S_ref[bi, h_]
            qS_s[pl.ds(b, 1), sl] = jnp.sum(S0 * column(qrT_s, h_), axis=0, keepdims=True)
            o_S[bi, h_] = (math.exp(_RET_LOG_GAMMA[h_]) * S0
                           + column(krT_s, h_) * vr_s[pl.ds(b, 1), sl])

    @pl.when(i == n_steps - 1)
    def _epilogue():
        h_ml = (inter_s[...] * qC_s[...] + sv_s[...]) * rden_s[...]
        hn = jnp.concatenate(
            [_headnorm(h_ml[:, h_ * DH:(h_ + 1) * DH]) for h_ in range(N_HEADS)], axis=1)
        y_b = hn * vec_ref[V_MLGN:V_MLGN + 1, :] + vec_ref[V_MLSKIP:V_MLSKIP + 1, :] * xcml_s[...]
        mix = mix_s[...] + _sigmoid_of_twice(p(P_GB)) * (y_b * _silu_of_twice(p(P_ZML)))
        cross = jnp.concatenate(
            [jnp.full((1, DH), math.exp(_RET_LOG_GAMMA[h_]), _F32) for h_ in range(N_HEADS)], axis=1)
        h_ret = svr_s[...] + qS_s[...] * cross
        hn = jnp.concatenate(
            [_headnorm(h_ret[:, h_ * DH:(h_ + 1) * DH]) for h_ in range(N_HEADS)], axis=1)
        mix = mix + _sigmoid_of_twice(p(P_GC)) * (
            hn * vec_ref[V_RETGN:V_RETGN + 1, :] * _silu_of_twice(p(P_GR)))
        out = _dot_chunks(mix.astype(_BF16), wout_ref, 0, D_MODEL // W_CHUNK)
        gate = mod_ref[:, 2 * D_MODEL:3 * D_MODEL]
        y_ref[...] = x_ref[...] + gate * (_rms(out) * vec_ref[V_NPOST:V_NPOST + 1, :])


def _sample_mix(layer, prev_big, proj, x, mod, vec, bif2, inv, wax, wq, wk, wv, wif2, wout,
                rgh, rgc, mlc, mln, mlm, mlC_all, retS_all):
    R = x.shape[0]
    c2 = lambda i: (0, 0)
    st5 = lambda i: (layer, i, 0, 0, 0)

    def resident(*shape):
        return pl.BlockSpec((None,) + shape, lambda i: (layer,) + (0,) * len(shape),
                            pipeline_mode=pl.Buffered(1))

    def resident2(*shape):
        return pl.BlockSpec(shape, c2, pipeline_mode=pl.Buffered(1))

    row_blk = resident2(R, D_MODEL)
    conv_blk = resident2(R, (CONV_W - 1) * D_MODEL)
    lane_blk = resident2(R, LANES)
    state_blk = pl.BlockSpec((None, S_BT, N_HEADS, DH, DH), st5)
    f32 = lambda *s: jax.ShapeDtypeStruct(s, _F32)
    n_in = 19
    outs = pl.pallas_call(
        _sample_mix_kernel,
        grid=(R // S_BT,),
        input_output_aliases={n_in: 6, n_in + 1: 7} if prev_big else {},
        in_specs=[
            resident2(R, D_IN),
            resident2(R, D_MODEL),
            resident(R, 3 * D_MODEL),
            resident(V_ROWS, D_MODEL),
            resident(1, 2 * LANES),
            resident2(1, HALF),
            resident(RG_HEADS, RG_BW, 2 * RG_BW),
            resident(N_HEADS, DH, DH),
            resident(N_HEADS, DH, DH),
            resident(N_HEADS, DH, DH),
            resident(3 * D_MODEL, 2 * LANES),
            resident(D_MODEL // W_CHUNK, D_MODEL, W_CHUNK),
            resident(R, D_MODEL),
            resident(R, (CONV_W - 1) * D_MODEL),
            resident(R, (CONV_W - 1) * D_MODEL),
            resident(R, D_MODEL),
            resident(R, LANES),
            state_blk,
            state_blk,
        ] + [pl.BlockSpec(memory_space=pl.ANY)] * len(prev_big),
        out_specs=[row_blk, row_blk, conv_blk, conv_blk, row_blk, lane_blk, state_blk, state_blk],
        out_shape=[
            f32(R, D_MODEL), f32(R, D_MODEL), f32(R, (CONV_W - 1) * D_MODEL),
            f32(R, (CONV_W - 1) * D_MODEL), f32(R, D_MODEL), f32(R, LANES),
            f32(DEPTH, R, N_HEADS, DH, DH), f32(DEPTH, R, N_HEADS, DH, DH),
        ],
        scratch_shapes=(
            [pltpu.VMEM((R, D_MODEL), _F32) for _ in range(10)]
            + [pltpu.VMEM((D_MODEL, R), _F32) for _ in range(4)]),
        compiler_params=pltpu.CompilerParams(
            dimension_semantics=("arbitrary",), vmem_limit_bytes=VMEM_LIMIT),
        name="sample_mix",
    )(proj, x, mod, vec, bif2, inv, wax, wq, wk, wv, wif2, wout,
      rgh, rgc, mlc, mln, mlm, mlC_all, retS_all, *prev_big)
    y, o_rgh, o_rgc, o_mlc, o_mln, o_mlm, o_C, o_S = outs
    return y, (o_rgh, o_rgc.reshape(R, CONV_W - 1, D_MODEL),
               o_mln.reshape(R, N_HEADS, DH), o_mlm[:, :N_HEADS],
               o_mlc.reshape(R, CONV_W - 1, D_MODEL)), (o_C, o_S)


def _pack_params(norm_pre, norm_post, rg_conv_w, rg_conv_b, rg_b_a, rg_b_x, rg_lambda,
                 ml_conv_w, ml_conv_b, ml_skip, ml_gn, ret_gn):
    row = lambda p: p[:, None, :]
    rows = [row(norm_pre), row(norm_post), rg_conv_w, row(rg_conv_b), row(0.5 * rg_b_a), row(0.5 * rg_b_x),
            row(rg_lambda), ml_conv_w, row(ml_conv_b), row(ml_skip), row(ml_gn), row(ret_gn)]
    vec = jnp.concatenate(rows, axis=1)
    return jnp.pad(vec, ((0, 0), (0, V_ROWS - vec.shape[1]), (0, 0)))


def _column_chunks(w):
    depth, K, N = w.shape
    return w.astype(_BF16).reshape(depth, K, N // W_CHUNK, W_CHUNK).transpose(0, 2, 1, 3)


def kernel(x_prompt, x_sample, c_prompt, c_sample, state_rg_h, state_rg_conv, state_ml_C, state_ml_n, state_ml_m, state_ml_conv, state_ret_S, w_ada, b_ada, norm_pre, norm_post, w_in, rg_conv_w, rg_conv_b, rg_w_a, rg_b_a, rg_w_x, rg_b_x, rg_lambda, ml_conv_w, ml_conv_b, ml_w_q, ml_w_k, ml_w_v, ml_w_if, ml_b_if, ml_skip, ml_gn, ret_gn, w_out):
    xs = x_sample[:, 0, :]
    R = xs.shape[0]
    mod_all = _ada_call(jnp.concatenate([c_sample, c_prompt], axis=0), w_ada, b_ada)
    inv = (ROPE_BASE ** (-jnp.linspace(0.0, 1.0, HALF, dtype=_F32)))[None, :]
    vec = _pack_params(norm_pre, norm_post, rg_conv_w, rg_conv_b, rg_b_a, rg_b_x,
                       rg_lambda, ml_conv_w, ml_conv_b, ml_skip, ml_gn, ret_gn)
    win = _column_chunks(w_in)
    wout = _column_chunks(w_out)
    wax = (0.5 * jnp.concatenate([rg_w_a, rg_w_x], axis=-1)).astype(_BF16)
    wq, wk, wv = ml_w_q, ml_w_k, ml_w_v
    lane_pad = ((0, 0), (0, 0), (0, LANES - N_HEADS))
    wif2 = jnp.concatenate([jnp.pad(ml_w_if[..., :N_HEADS], lane_pad),
                            jnp.pad(ml_w_if[..., N_HEADS:], lane_pad)], axis=-1).astype(_BF16)
    b_if = ml_b_if[:, None, :]
    bif2 = jnp.concatenate([jnp.pad(b_if[..., :N_HEADS], lane_pad),
                            jnp.pad(b_if[..., N_HEADS:], lane_pad)], axis=-1)
    s_rgc_in = state_rg_conv.reshape(DEPTH, R, -1)
    s_mlc_in = state_ml_conv.reshape(DEPTH, R, -1)
    s_mln_in = state_ml_n.reshape(DEPTH, R, -1)
    s_mlm_in = jnp.pad(state_ml_m, lane_pad)

    xp = _interleave_tiles(x_prompt)
    st_p, st_s = [], []
    big_p, big_s = (), ()
    for l in range(DEPTH):
        xp, sp, big_p = _prompt_layer(l, big_p, R, xp, mod_all, vec, bif2, inv, win, wax, wq, wk, wv, wif2, wout)
        st_p.append(sp)
        proj = _sample_proj(l, xs, mod_all, vec, win)
        xs, ss, big_s = _sample_mix(
            l, big_s, proj, xs, mod_all, vec, bif2, inv, wax, wq, wk, wv, wif2, wout,
            state_rg_h, s_rgc_in, s_mlc_in, s_mln_in, s_mlm_in, state_ml_C, state_ret_S)
        st_s.append(ss)
    p_rgh, p_rgc, p_n, p_m, p_mlc = [jnp.stack(a) for a in zip(*st_p)]
    s_rgh, s_rgc, s_n, s_m, s_mlc = [jnp.stack(a) for a in zip(*st_s)]
    return (_deinterleave_tiles(xp), xs[:, None, :], p_rgh, p_rgc, big_p[0], p_n, p_m, p_mlc, big_p[1],
            s_rgh, s_rgc, big_s[0], s_n, s_m, s_mlc, big_s[1])
```

```python
import math

import jax
import jax.numpy as jnp
from jax import lax
from jax.experimental import pallas as pl
from jax.experimental.pallas import tpu as pltpu

D_MODEL = 1024
DEPTH = 2
CONV_W = 4
RG_HEADS = 8
RG_BW = D_MODEL // RG_HEADS
RG_C = 8.0
N_HEADS = 4
DH = D_MODEL // N_HEADS
HALF = DH // 2
PAST_LEN = 16384
ROPE_BASE = 10000.0
EPS = 1e-6
N_PROJ = 11
D_IN = N_PROJ * D_MODEL
P_XRG, P_ZRG, P_XML, P_ZML, P_QR, P_KR, P_VR, P_GR, P_GA, P_GB, P_GC = range(N_PROJ)
P_GATE_LIKE = (P_ZRG, P_ZML, P_GR, P_GA, P_GB, P_GC)

V_NPRE, V_NPOST, V_RGCW, V_RGCB, V_RGBA, V_RGBX, V_RGLAM = 0, 1, 2, 6, 7, 8, 9
V_MLCW, V_MLCB, V_MLSKIP, V_MLGN, V_RETGN, V_ROWS = 10, 14, 15, 16, 17, 24

SUBLANES = 8
LANES = 128
MXU_N = 256
W_CHUNK = MXU_N
T_TILE = 256
SEG = T_TILE // SUBLANES
N_GROUPS = T_TILE // SUBLANES
VMEM_LIMIT = 60 * 1024 * 1024
PREFETCH_START = 1
PREFETCH_B1 = (1, 1, 1, 1)
PREFETCH_GATES = 1
PREFETCH_A = 4
PREFETCH_B2 = 3
PREFETCH_C = 2

_RET_LOG_GAMMA = tuple(math.log1p(-2.0 ** (-5.0 - h)) for h in range(N_HEADS))

_NT = (((1,), (1,)), ((), ()))
_TN = (((0,), (0,)), ((), ()))
_F32 = jnp.float32
_BF16 = jnp.bfloat16


def _dot(a, b):
    return jnp.dot(a, b, preferred_element_type=_F32)


def _dot_chunks(a, w_ref, first, n):
    return jnp.concatenate([_dot(a, w_ref[first + c]) for c in range(n)], axis=1)


def _sigmoid_of_twice(hx):
    return 0.5 * jnp.tanh(hx) + 0.5


def _silu_of_twice(hx):
    return hx * jnp.tanh(hx) + hx


def _silu(x):
    return _silu_of_twice(0.5 * x)


def _softplus(x):
    return jnp.maximum(x, 0.0) + jnp.log1p(jnp.exp(-jnp.abs(x)))


def _log_sigmoid(x):
    return -_softplus(-x)


def _neg_expm1_2x(x):
    t = jnp.tanh(x)
    return (-2.0 * t) / (1.0 - t)


def _rms(x):
    return x * lax.rsqrt(jnp.mean(x * x, axis=-1, keepdims=True) + EPS)


def _headnorm(h):
    mu = jnp.mean(h, axis=-1, keepdims=True)
    c = h - mu
    var = jnp.mean(c * c, axis=-1, keepdims=True)
    return c * lax.rsqrt(var + EPS)


def _rg_gates(xc, vec_ref, wax_ref, lane0=0):
    xcb = xc.astype(_BF16)
    a_parts, b_parts = [], []
    for hh in range(xc.shape[1] // RG_BW):
        loc = slice(hh * RG_BW, (hh + 1) * RG_BW)
        sl = slice(lane0 + hh * RG_BW, lane0 + (hh + 1) * RG_BW)
        g = _dot(xcb[:, loc], wax_ref[sl.start // RG_BW])
        r = _sigmoid_of_twice(g[:, :RG_BW] + vec_ref[V_RGBA:V_RGBA + 1, sl])
        i = _sigmoid_of_twice(g[:, RG_BW:] + vec_ref[V_RGBX:V_RGBX + 1, sl])
        log_a = (-RG_C) * r * _softplus(-vec_ref[V_RGLAM:V_RGLAM + 1, sl])
        a_parts.append(jnp.exp(log_a))
        b_parts.append(jnp.sqrt(_neg_expm1_2x(log_a)) * (i * xc[:, loc]))
    return jnp.concatenate(a_parts, axis=1), jnp.concatenate(b_parts, axis=1)


def _rope_tables(pos, inv_row):
    ang = pos * inv_row
    return jnp.cos(ang), jnp.sin(ang)


def _rope(x, cos, sin, scale=None):
    out = []
    for h in range(N_HEADS):
        x1 = x[:, h * DH:h * DH + HALF]
        x2 = x[:, h * DH + HALF:(h + 1) * DH]
        r = jnp.concatenate([x1 * cos - x2 * sin, x1 * sin + x2 * cos], axis=1)
        out.append(r if scale is None else r * scale)
    return out


def _token_of_row(i):
    return (i & (SUBLANES - 1)) * SEG + (i >> 3)


def _split3_bf16(x):
    p0 = x.astype(_BF16)
    r = x - p0.astype(_F32)
    p1 = r.astype(_BF16)
    p2 = (r - p1.astype(_F32)).astype(_BF16)
    return p0, p1, p2


def _ada_kernel(c_ref, w_ref, b_ref, o_ref):
    c = c_ref[...]
    w = w_ref[...]
    c_hi = c.astype(_BF16)
    c_lo = (c - c_hi.astype(_F32)).astype(_BF16)
    w_hi = w.astype(_BF16)
    o_ref[...] = _dot(c_hi, w_hi) + _dot(c_lo, w_hi) + b_ref[...]


def _ada_call(c_all, w_ada, b_ada):
    rows = c_all.shape[0]
    return pl.pallas_call(
        _ada_kernel,
        grid=(DEPTH, 3),
        in_specs=[
            pl.BlockSpec((rows, D_MODEL), lambda l, n: (0, 0)),
            pl.BlockSpec((None, D_MODEL, D_MODEL), lambda l, n: (l, 0, n)),
            pl.BlockSpec((None, 1, D_MODEL), lambda l, n: (l, 0, n)),
        ],
        out_specs=pl.BlockSpec((None, rows, D_MODEL), lambda l, n: (l, 0, n)),
        out_shape=jax.ShapeDtypeStruct((DEPTH, rows, 3 * D_MODEL), _F32),
        name="adaln",
    )(c_all, w_ada, b_ada.reshape(DEPTH, 1, 3 * D_MODEL))


def _shifted_conv(xin, cs, carry_ref, vec_ref, w0, brow, o_state):
    T = T_TILE
    tail = (CONV_W - 1) * SUBLANES
    prev = carry_ref[:, cs]
    cur = xin[T - tail:T, :]
    sub = lax.broadcasted_iota(jnp.int32, (SUBLANES, xin.shape[1]), 0)
    lead = []
    for q in range(CONV_W - 1):
        sl = slice(q * SUBLANES, (q + 1) * SUBLANES)
        lead.append(jnp.where(sub == 0, pltpu.roll(prev[sl], 1, axis=0), pltpu.roll(cur[sl], 1, axis=0)))
    ext = jnp.concatenate(lead + [xin], axis=0)
    acc = vec_ref[brow:brow + 1, cs] + vec_ref[w0 + 3:w0 + 4, cs] * xin
    for d in range(1, CONV_W):
        acc = acc + vec_ref[w0 + 3 - d:w0 + 4 - d, cs] * ext[tail - SUBLANES * d:tail - SUBLANES * d + T, :]
    carry_ref[:, cs] = cur
    o_state[0, :, cs] = jnp.concatenate(
        [cur[q * SUBLANES + SUBLANES - 1:(q + 1) * SUBLANES, :] for q in range(CONV_W - 1)], axis=0)
    return acc


def _segment_scan(a, bt, h0):
    hs, cums = [], []
    h = cum = None
    for r in range(N_GROUPS):
        sl = slice(r * SUBLANES, (r + 1) * SUBLANES)
        if r == 0:
            h, cum = bt[sl], a[sl]
        else:
            h, cum = a[sl] * h + bt[sl], a[sl] * cum
        hs.append(h)
        cums.append(cum)
    carry = h0
    carries = [carry]
    for s in range(SUBLANES):
        carry = cum[s:s + 1, :] * carry + h[s:s + 1, :]
        carries.append(carry)
    carry_in = jnp.concatenate(carries[:SUBLANES], axis=0)
    out = jnp.concatenate([hs[r] + cums[r] * carry_in for r in range(N_GROUPS)], axis=0)
    return out, carries[SUBLANES]


def _prompt_kernel(x_ref, mod_ref, vec_ref, bif_ref, inv_ref,
                   win_ref, wax_ref, wq_ref, wk_ref, wv_ref, wif_ref, wout_ref, *rest):
    (y_ref, o_rgh, o_rgc, o_C, o_n, o_m, o_mlc, o_S,
     cc_rg, cc_ml, qkv_s, xcml_s, mix_s, m_s, cos_s, sin_s,
     mask_s, tri_s, dret_s, cross_s, wk_s) = rest[-21:]
    T = T_TILE
    b = pl.program_id(0)
    j = pl.program_id(1)

    @pl.when(jnp.logical_and(b == 0, j == 0))
    def _init_tables():
        tr = _token_of_row(lax.broadcasted_iota(jnp.int32, (T, T), 0))
        tc = _token_of_row(lax.broadcasted_iota(jnp.int32, (T, T), 1))
        causal = tr >= tc
        mask_s[...] = jnp.where(causal, 0.0, -jnp.inf)
        tri_s[...] = jnp.where(causal, 1.0, 0.0).astype(_BF16)
        rel = (tr - tc).astype(_F32)
        tokf = _token_of_row(lax.broadcasted_iota(jnp.int32, (T, DH), 0)).astype(_F32)
        for h in range(N_HEADS):
            lg = _RET_LOG_GAMMA[h]
            dret_s[h] = jnp.where(causal, jnp.exp(rel * lg), 0.0)
            cross_s[h] = jnp.exp((tokf + 1.0) * lg)
            wk_s[h] = jnp.exp((T - 1.0 - tokf) * lg)

    @pl.when(b == 0)
    def _init_rope():
        tok = _token_of_row(lax.broadcasted_iota(jnp.int32, (T, HALF), 0))
        c, s = _rope_tables((j * T + tok).astype(_F32), inv_ref[...])
        cos_s[pl.ds(pl.multiple_of(j * T, T), T), :] = c
        sin_s[pl.ds(pl.multiple_of(j * T, T), T), :] = s

    @pl.when(j == 0)
    def _init_state():
        o_rgh[...] = jnp.zeros_like(o_rgh)
        o_C[...] = jnp.zeros_like(o_C)
        o_n[...] = jnp.zeros_like(o_n)
        o_S[...] = jnp.zeros_like(o_S)
        m_s[...] = jnp.zeros_like(m_s)
        cc_rg[...] = jnp.zeros_like(cc_rg)
        cc_ml[...] = jnp.zeros_like(cc_ml)

    x = x_ref[0]
    shift = mod_ref[pl.ds(b, 1), 0:D_MODEL]
    scale = mod_ref[pl.ds(b, 1), D_MODEL:2 * D_MODEL]
    gate = mod_ref[pl.ds(b, 1), 2 * D_MODEL:3 * D_MODEL]
    u = _rms(x) * (vec_ref[V_NPRE:V_NPRE + 1, :] * (1.0 + scale)) + shift
    ub = u.astype(_BF16)
    n_chunk = D_MODEL // MXU_N
    n_wide = D_MODEL // W_CHUNK

    pending = [(P_XML, w) for w in range(n_wide)]
    pending += [(blk, w) for w in range(n_wide) for blk in (P_XRG, P_ZRG, P_GA)]
    pending += [(blk, w) for w in range(n_wide) for blk in (P_ZML, P_GB)]
    pending += [(blk, w) for w in range(n_wide) for blk in (P_QR, P_KR, P_VR, P_GR, P_GC)]
    ready = {}

    def prefetch(n):
        for _ in range(min(n, len(pending))):
            blk, w = pending.pop(0)
            ready[(blk, w)] = _dot(ub, win_ref[blk * n_wide + w])

    def proj(blk, c):
        per_wide = W_CHUNK // MXU_N
        key = (blk, c // per_wide)
        while key not in ready:
            prefetch(1)
        lo = (c % per_wide) * MXU_N
        v = ready[key][:, lo:lo + MXU_N]
        return 0.5 * v if blk in P_GATE_LIKE else v

    prefetch(PREFETCH_START)
    for h in range(N_HEADS):
        cs = slice(h * DH, (h + 1) * DH)
        x_ml = proj(P_XML, h)
        prefetch(PREFETCH_B1[h])
        xc_ml = _silu(_shifted_conv(x_ml, cs, cc_ml, vec_ref, V_MLCW, V_MLCB, o_mlc))
        xcml_s[:, cs] = xc_ml
        xhb = xc_ml.astype(_BF16)
        prefetch(1)
        q = _dot(xhb, wq_ref[h].astype(_BF16)).astype(_BF16)
        k = (_dot(xhb, wk_ref[h].astype(_BF16)) * (DH ** -0.5)).astype(_BF16)
        v = _dot(x_ml.astype(_BF16), wv_ref[h].astype(_BF16)).astype(_BF16)
        qkv_s[:, cs] = q
        qkv_s[:, D_MODEL + h * DH:D_MODEL + (h + 1) * DH] = k
        qkv_s[:, 2 * D_MODEL + h * DH:2 * D_MODEL + (h + 1) * DH] = v
    mask = mask_s[...]

    def gates_sums():
        gpre = jnp.concatenate([_dot(qkv_s[0:T // 2, :], wif_ref[...]),
                                _dot(qkv_s[T // 2:T, :], wif_ref[...])], axis=0) + bif_ref[...]
        lf = _log_sigmoid(gpre[:, LANES:])
        prefetch(1)
        F = sum(_dot(tri_s[...], part) for part in _split3_bf16(lf))
        uu = gpre[:, :LANES] - F
        return F, uu, uu.T

    def gates_running_max(uu_rows):
        lane = lax.broadcasted_iota(jnp.int32, (T, LANES), 1)
        cmax = jnp.zeros((T, LANES), _F32)
        for h in range(N_HEADS):
            cm_h = jnp.max(uu_rows[h:h + 1, :] + mask, axis=1, keepdims=True)
            cmax = jnp.where(lane == h, cm_h, cmax)
        return cmax

    def gates_columns(F, uu, cmax):
        m0 = m_s[0:1, :]
        mrun = jnp.maximum(m0, cmax)
        m = F + mrun
        mL = m[T - 1:T, :]
        FL = F[T - 1:T, :]
        m_s[...] = jnp.broadcast_to(mL, m_s.shape)
        o_m[0] = m_s[...]
        return (-mrun,
                jnp.exp(m0 - mrun),
                jnp.exp(-m),
                jnp.exp(FL - mL + uu),
                jnp.broadcast_to(jnp.exp(FL + m0 - mL), (T, LANES)))

    prefetch(PREFETCH_GATES)
    F, uu, uu_rows = gates_sums()
    for c in range(n_chunk):
        cs = slice(c * MXU_N, (c + 1) * MXU_N)
        x_rg = proj(P_XRG, c)
        prefetch(2)
        xc = _shifted_conv(x_rg, cs, cc_rg, vec_ref, V_RGCW, V_RGCB, o_rgc)
        prefetch(PREFETCH_A - 2)
        a, bt = _rg_gates(xc, vec_ref, wax_ref, c * MXU_N)
        h_rg, h_last = _segment_scan(a, bt, o_rgh[0, :, cs])
        o_rgh[0, :, cs] = h_last
        mix_s[:, cs] = _sigmoid_of_twice(proj(P_GA, c)) * (h_rg * _silu_of_twice(proj(P_ZRG, c)))
        if c == 0:
            cmax = gates_running_max(uu_rows)
        if c == 1:
            c_row, c_inter, c_floor, c_wl, c_decay = gates_columns(F, uu, cmax)

    for h in range(N_HEADS):
        cs = slice(h * DH, (h + 1) * DH)
        prefetch(1)
        qh = qkv_s[:, h * DH:(h + 1) * DH]
        kh = qkv_s[:, D_MODEL + h * DH:D_MODEL + (h + 1) * DH]
        vh = qkv_s[:, 2 * D_MODEL + h * DH:2 * D_MODEL + (h + 1) * DH]
        col = lambda c: c[:, h:h + 1]
        dmat = jnp.exp(col(c_row) + (uu_rows[h:h + 1, :] + mask))
        s = lax.dot_general(qh, kh, _NT, preferred_element_type=_F32) * dmat
        C0 = o_C[0, h]
        n0 = o_n[0, h:h + 1, :]
        prefetch(1)
        num = col(c_inter) * _dot(qh, C0.astype(_BF16)) + _dot(s.astype(_BF16), vh)
        den = (col(c_inter) * jnp.sum(qh.astype(_F32) * n0, axis=-1, keepdims=True)
               + jnp.sum(s, axis=-1, keepdims=True))
        h_ml = num * (1.0 / jnp.maximum(jnp.abs(den), col(c_floor)))
        kw = kh.astype(_F32) * col(c_wl)
        prefetch(PREFETCH_B2 - 2)
        o_C[0, h] = col(c_decay) * C0 + lax.dot_general(
            kw.astype(_BF16), vh, _TN, preferred_element_type=_F32)
        o_n[0, h:h + 1, :] = col(c_decay)[0:1, :] * n0 + jnp.sum(kw, axis=0, keepdims=True)
        y_b = (_headnorm(h_ml) * vec_ref[V_MLGN:V_MLGN + 1, cs]
               + vec_ref[V_MLSKIP:V_MLSKIP + 1, cs] * xcml_s[:, cs])
        mix_s[:, cs] += _sigmoid_of_twice(proj(P_GB, h)) * (y_b * _silu_of_twice(proj(P_ZML, h)))

    t0 = pl.multiple_of(j * T, T)
    cos = cos_s[pl.ds(t0, T), :]
    sin = sin_s[pl.ds(t0, T), :]

    def rope(xh):
        x1, x2 = xh[:, :HALF], xh[:, HALF:]
        return jnp.concatenate([x1 * cos - x2 * sin, x1 * sin + x2 * cos], axis=1)

    for h in range(N_HEADS):
        cs = slice(h * DH, (h + 1) * DH)
        qh = rope(proj(P_QR, h)).astype(_BF16)
        kf = rope(proj(P_KR, h)) * (DH ** -0.5)
        vh = proj(P_VR, h).astype(_BF16)
        prefetch(1)
        s = lax.dot_general(qh, kf.astype(_BF16), _NT, preferred_element_type=_F32) * dret_s[h]
        S0 = o_S[0, h]
        prefetch(PREFETCH_C - 1)
        h_ret = _dot(s.astype(_BF16), vh) + _dot(qh, S0.astype(_BF16)) * cross_s[h]
        kw = (kf * wk_s[h]).astype(_BF16)
        o_S[0, h] = math.exp(T * _RET_LOG_GAMMA[h]) * S0 + lax.dot_general(
            kw, vh, _TN, preferred_element_type=_F32)
        y_c = _headnorm(h_ret) * vec_ref[V_RETGN:V_RETGN + 1, cs] * _silu_of_twice(proj(P_GR, h))
        mix_s[:, cs] += _sigmoid_of_twice(proj(P_GC, h)) * y_c

    out = _dot_chunks(mix_s[...].astype(_BF16), wout_ref, 0, n_wide)
    y_ref[0] = x + gate * (_rms(out) * vec_ref[V_NPOST:V_NPOST + 1, :])


def _prompt_layer(layer, prev_big, mod_row0, x, mod, vec, bif2, inv, win, wax, wq, wk, wv, wif2, wout):
    B, L, _ = x.shape
    T = T_TILE
    per_b3 = lambda b, j: (b, 0, 0)
    big_blk = pl.BlockSpec((None, 1, N_HEADS, DH, DH), lambda b, j: (layer, b, 0, 0, 0))
    big_shape = jax.ShapeDtypeStruct((DEPTH, B, N_HEADS, DH, DH), _F32)

    def resident(*shape):
        return pl.BlockSpec((None,) + shape, lambda b, j: (layer,) + (0,) * len(shape),
                            pipeline_mode=pl.Buffered(1))

    n_in = 12
    extra_specs = [pl.BlockSpec(memory_space=pl.ANY)] * len(prev_big)
    aliases = {n_in: 3, n_in + 1: 7} if prev_big else {}
    tail = (CONV_W - 1) * SUBLANES
    outs = pl.pallas_call(
        _prompt_kernel,
        grid=(B, L // T),
        input_output_aliases=aliases,
        in_specs=[
            pl.BlockSpec((1, T, D_MODEL), lambda b, j: (b, j, 0)),
            pl.BlockSpec((None, B, 3 * D_MODEL), lambda b, j: (layer, mod_row0 // B, 0),
                         pipeline_mode=pl.Buffered(1)),
            resident(V_ROWS, D_MODEL),
            resident(1, 2 * LANES),
            pl.BlockSpec((1, HALF), lambda b, j: (0, 0), pipeline_mode=pl.Buffered(1)),
            resident(D_IN // W_CHUNK, D_MODEL, W_CHUNK),
            resident(RG_HEADS, RG_BW, 2 * RG_BW),
            resident(N_HEADS, DH, DH),
            resident(N_HEADS, DH, DH),
            resident(N_HEADS, DH, DH),
            resident(3 * D_MODEL, 2 * LANES),
            resident(D_MODEL // W_CHUNK, D_MODEL, W_CHUNK),
        ] + extra_specs,
        out_specs=[
            pl.BlockSpec((1, T, D_MODEL), lambda b, j: (b, j, 0)),
            pl.BlockSpec((1, 1, D_MODEL), per_b3),
            pl.BlockSpec((1, CONV_W - 1, D_MODEL), per_b3),
            big_blk,
            pl.BlockSpec((1, N_HEADS, DH), per_b3),
            pl.BlockSpec((1, SUBLANES, LANES), per_b3),
            pl.BlockSpec((1, CONV_W - 1, D_MODEL), per_b3),
            big_blk,
        ],
        out_shape=[
            jax.ShapeDtypeStruct((B, L, D_MODEL), _F32),
            jax.ShapeDtypeStruct((B, 1, D_MODEL), _F32),
            jax.ShapeDtypeStruct((B, CONV_W - 1, D_MODEL), _F32),
            big_shape,
            jax.ShapeDtypeStruct((B, N_HEADS, DH), _F32),
            jax.ShapeDtypeStruct((B, SUBLANES, LANES), _F32),
            jax.ShapeDtypeStruct((B, CONV_W - 1, D_MODEL), _F32),
            big_shape,
        ],
        scratch_shapes=[
            pltpu.VMEM((tail, D_MODEL), _F32),
            pltpu.VMEM((tail, D_MODEL), _F32),
            pltpu.VMEM((T, 3 * D_MODEL), _BF16),
            pltpu.VMEM((T, D_MODEL), _F32),
            pltpu.VMEM((T, D_MODEL), _F32),
            pltpu.VMEM((SUBLANES, LANES), _F32),
            pltpu.VMEM((L, HALF), _F32),
            pltpu.VMEM((L, HALF), _F32),
            pltpu.VMEM((T, T), _F32),
            pltpu.VMEM((T, T), _BF16),
            pltpu.VMEM((N_HEADS, T, T), _F32),
            pltpu.VMEM((N_HEADS, T, DH), _F32),
            pltpu.VMEM((N_HEADS, T, DH), _F32),
        ],
        compiler_params=pltpu.CompilerParams(
            dimension_semantics=("arbitrary", "arbitrary"),
            vmem_limit_bytes=VMEM_LIMIT),
        name="prompt_layer",
    )(x, mod, vec, bif2, inv, win, wax, wq, wk, wv, wif2, wout, *prev_big)
    y, rgh, rgc, mlC, mln, mlm, mlc, retS = outs
    return y, (rgh[:, 0], rgc, mln, mlm[:, 0, :N_HEADS], mlc), (mlC, retS)


def _interleave_tiles(x):
    B, L, D = x.shape
    return x.reshape(B, L // T_TILE, SUBLANES, SEG, D).transpose(0, 1, 3, 2, 4).reshape(B, L, D)


def _deinterleave_tiles(x):
    B, L, D = x.shape
    return x.reshape(B, L // T_TILE, SEG, SUBLANES, D).transpose(0, 1, 3, 2, 4).reshape(B, L, D)


def _sample_proj_kernel(x_ref, mod_ref, vec_ref, w_ref, o_ref):
    scale = mod_ref[:, D_MODEL:2 * D_MODEL]
    u = _rms(x_ref[...]) * (vec_ref[V_NPRE:V_NPRE + 1, :] * (1.0 + scale)) + mod_ref[:, 0:D_MODEL]
    o_ref[...] = _dot_chunks(u.astype(_BF16), w_ref, 0, w_ref.shape[0])


def _sample_proj(layer, x, mod, vec, win):
    rows = x.shape[0]
    n_steps = D_MODEL // W_CHUNK
    return pl.pallas_call(
        _sample_proj_kernel,
        grid=(n_steps,),
        in_specs=[
            pl.BlockSpec((rows, D_MODEL), lambda n: (0, 0)),
            pl.BlockSpec((None, rows, 3 * D_MODEL), lambda n: (layer, 0, 0)),
            pl.BlockSpec((None, V_ROWS, D_MODEL), lambda n: (layer, 0, 0)),
            pl.BlockSpec((None, N_PROJ, D_MODEL, W_CHUNK), lambda n: (layer, n, 0, 0)),
        ],
        out_specs=pl.BlockSpec((rows, N_PROJ * W_CHUNK), lambda n: (0, n)),
        out_shape=jax.ShapeDtypeStruct((rows, D_IN), _F32),
        compiler_params=pltpu.CompilerParams(vmem_limit_bytes=VMEM_LIMIT),
        name="sample_proj",
    )(x, mod, vec, win)


S_BT = 2


def _bcast_heads(cols):
    rows = cols[0].shape[0]
    return jnp.concatenate([jnp.broadcast_to(c, (rows, DH)) for c in cols], axis=1)


def _sample_mix_kernel(proj_ref, x_ref, mod_ref, vec_ref, bif_ref, inv_ref,
                       wax_ref, wq_ref, wk_ref, wv_ref, wif_ref, wout_ref,
                       rgh_ref, rgc_ref, mlc_ref, mln_ref, mlm_ref, C_ref, S_ref, *rest):
    (y_ref, o_rgh, o_rgc, o_mlc, o_mln, o_mlm, o_C, o_S,
     mix_s, xcml_s, inter_s, dv_s, sv_s, rden_s, svr_s, vr_s, qC_s, qS_s,
     qT_s, kT_s, qrT_s, krT_s) = rest[-22:]
    i = pl.program_id(0)
    n_steps = pl.num_programs(0)
    R = x_ref.shape[0]

    def p(blk):
        v = proj_ref[:, blk * D_MODEL:(blk + 1) * D_MODEL]
        return 0.5 * v if blk in P_GATE_LIKE else v

    def conv(xin, cs_ref, o_state, w0, brow):
        cs = cs_ref[...]
        acc = vec_ref[brow:brow + 1, :] + vec_ref[w0 + 3:w0 + 4, :] * xin
        for k in range(CONV_W - 1):
            acc = acc + vec_ref[w0 + k:w0 + k + 1, :] * cs[:, k * D_MODEL:(k + 1) * D_MODEL]
        o_state[...] = jnp.concatenate([cs[:, D_MODEL:], xin], axis=1)
        return acc

    @pl.when(i == 0)
    def _dense():
        xc = conv(p(P_XRG), rgc_ref, o_rgc, V_RGCW, V_RGCB)
        a, bt = _rg_gates(xc, vec_ref, wax_ref)
        h = a * rgh_ref[...] + bt
        o_rgh[...] = h
        mix_s[...] = _sigmoid_of_twice(p(P_GA)) * (h * _silu_of_twice(p(P_ZRG)))

        x_ml = p(P_XML)
        xc_ml = _silu(conv(x_ml, mlc_ref, o_mlc, V_MLCW, V_MLCB))
        xcml_s[...] = xc_ml
        xhb = xc_ml.astype(_BF16)
        xmb = x_ml.astype(_BF16)
        qs, ks, vs = [], [], []
        for h_ in range(N_HEADS):
            sl = slice(h_ * DH, (h_ + 1) * DH)
            qs.append(_dot(xhb[:, sl], wq_ref[h_].astype(_BF16)))
            ks.append(_dot(xhb[:, sl], wk_ref[h_].astype(_BF16)) * (DH ** -0.5))
            vs.append(_dot(xmb[:, sl], wv_ref[h_].astype(_BF16)))
        q = jnp.concatenate(qs, axis=1)
        k = jnp.concatenate(ks, axis=1)
        v = jnp.concatenate(vs, axis=1)
        qkvb = jnp.concatenate([q, k, v], axis=1).astype(_BF16)
        gpre = _dot(qkvb, wif_ref[...]) + bif_ref[...]
        ig = gpre[:, :LANES]
        lf = _log_sigmoid(gpre[:, LANES:])
        m0 = mlm_ref[...]
        m = jnp.maximum(m0 + lf, ig)
        o_mlm[...] = m
        dg = jnp.exp(ig - m)
        inter = jnp.exp(lf + m0 - m)
        floor = jnp.exp(-m)
        n0 = mln_ref[...]
        s_cols, rden_cols = [], []
        for h_ in range(N_HEADS):
            sl = slice(h_ * DH, (h_ + 1) * DH)
            s_h = jnp.sum(q[:, sl] * k[:, sl], axis=-1, keepdims=True) * dg[:, h_:h_ + 1]
            qn = jnp.sum(q[:, sl] * n0[:, sl], axis=-1, keepdims=True)
            den = inter[:, h_:h_ + 1] * qn + s_h
            s_cols.append(s_h)
            rden_cols.append(1.0 / jnp.maximum(jnp.abs(den), floor[:, h_:h_ + 1]))
        dg_b = _bcast_heads([dg[:, h_:h_ + 1] for h_ in range(N_HEADS)])
        inter_b = _bcast_heads([inter[:, h_:h_ + 1] for h_ in range(N_HEADS)])
        inter_s[...] = inter_b
        dv_s[...] = dg_b * v
        sv_s[...] = _bcast_heads(s_cols) * v
        rden_s[...] = _bcast_heads(rden_cols)
        o_mln[...] = inter_b * n0 + dg_b * k
        qT_s[...] = q.T
        kT_s[...] = k.T

        pos = jnp.full((1, HALF), float(PAST_LEN), _F32)
        cos, sin = _rope_tables(pos, inv_ref[...])
        qr = jnp.concatenate(_rope(p(P_QR), cos, sin), axis=1)
        kr = jnp.concatenate(_rope(p(P_KR), cos, sin, DH ** -0.5), axis=1)
        v_r = p(P_VR)
        s_cols = [jnp.sum(qr[:, h_ * DH:(h_ + 1) * DH] * kr[:, h_ * DH:(h_ + 1) * DH],
                          axis=-1, keepdims=True) for h_ in range(N_HEADS)]
        svr_s[...] = _bcast_heads(s_cols) * v_r
        vr_s[...] = v_r
        qrT_s[...] = qr.T
        krT_s[...] = kr.T

    lane = lax.broadcasted_iota(jnp.int32, (DH, R), 1)
    for bi in range(S_BT):
        b = i * S_BT + bi
        onehot = (lane == b).astype(_F32)

        def column(t_ref, h_):
            return jnp.sum(t_ref[h_ * DH:(h_ + 1) * DH, :] * onehot, axis=1, keepdims=True)

        for h_ in range(N_HEADS):
            sl = slice(h_ * DH, (h_ + 1) * DH)
            C0 = C_ref[bi, h_]
            qC_s[pl.ds(b, 1), sl] = jnp.sum(C0 * column(qT_s, h_), axis=0, keepdims=True)
            o_C[bi, h_] = (inter_s[pl.ds(b, 1), sl] * C0
                           + column(kT_s, h_) * dv_s[pl.ds(b, 1), sl])
            S0 = S_ref[bi, h_]
            qS_s[pl.ds(b, 1), sl] = jnp.sum(S0 * column(qrT_s, h_), axis=0, keepdims=True)
            o_S[bi, h_] = (math.exp(_RET_LOG_GAMMA[h_]) * S0
                           + column(krT_s, h_) * vr_s[pl.ds(b, 1), sl])

    @pl.when(i == n_steps - 1)
    def _epilogue():
        h_ml = (inter_s[...] * qC_s[...] + sv_s[...]) * rden_s[...]
        hn = jnp.concatenate(
            [_headnorm(h_ml[:, h_ * DH:(h_ + 1) * DH]) for h_ in range(N_HEADS)], axis=1)
        y_b = hn * vec_ref[V_MLGN:V_MLGN + 1, :] + vec_ref[V_MLSKIP:V_MLSKIP + 1, :] * xcml_s[...]
        mix = mix_s[...] + _sigmoid_of_twice(p(P_GB)) * (y_b * _silu_of_twice(p(P_ZML)))
        cross = jnp.concatenate(
            [jnp.full((1, DH), math.exp(_RET_LOG_GAMMA[h_]), _F32) for h_ in range(N_HEADS)], axis=1)
        h_ret = svr_s[...] + qS_s[...] * cross
        hn = jnp.concatenate(
            [_headnorm(h_ret[:, h_ * DH:(h_ + 1) * DH]) for h_ in range(N_HEADS)], axis=1)
        mix = mix + _sigmoid_of_twice(p(P_GC)) * (
            hn * vec_ref[V_RETGN:V_RETGN + 1, :] * _silu_of_twice(p(P_GR)))
        out = _dot_chunks(mix.astype(_BF16), wout_ref, 0, D_MODEL // W_CHUNK)
        gate = mod_ref[:, 2 * D_MODEL:3 * D_MODEL]
        y_ref[...] = x_ref[...] + gate * (_rms(out) * vec_ref[V_NPOST:V_NPOST + 1, :])


def _sample_mix(layer, prev_big, proj, x, mod, vec, bif2, inv, wax, wq, wk, wv, wif2, wout,
                rgh, rgc, mlc, mln, mlm, mlC_all, retS_all):
    R = x.shape[0]
    c2 = lambda i: (0, 0)
    st5 = lambda i: (layer, i, 0, 0, 0)

    def resident(*shape):
        return pl.BlockSpec((None,) + shape, lambda i: (layer,) + (0,) * len(shape),
                            pipeline_mode=pl.Buffered(1))

    def resident2(*shape):
        return pl.BlockSpec(shape, c2, pipeline_mode=pl.Buffered(1))

    row_blk = resident2(R, D_MODEL)
    conv_blk = resident2(R, (CONV_W - 1) * D_MODEL)
    lane_blk = resident2(R, LANES)
    state_blk = pl.BlockSpec((None, S_BT, N_HEADS, DH, DH), st5)
    f32 = lambda *s: jax.ShapeDtypeStruct(s, _F32)
    n_in = 19
    outs = pl.pallas_call(
        _sample_mix_kernel,
        grid=(R // S_BT,),
        input_output_aliases={n_in: 6, n_in + 1: 7} if prev_big else {},
        in_specs=[
            resident2(R, D_IN),
            resident2(R, D_MODEL),
            resident(R, 3 * D_MODEL),
            resident(V_ROWS, D_MODEL),
            resident(1, 2 * LANES),
            resident2(1, HALF),
            resident(RG_HEADS, RG_BW, 2 * RG_BW),
            resident(N_HEADS, DH, DH),
            resident(N_HEADS, DH, DH),
            resident(N_HEADS, DH, DH),
            resident(3 * D_MODEL, 2 * LANES),
            resident(D_MODEL // W_CHUNK, D_MODEL, W_CHUNK),
            resident(R, D_MODEL),
            resident(R, (CONV_W - 1) * D_MODEL),
            resident(R, (CONV_W - 1) * D_MODEL),
            resident(R, D_MODEL),
            resident(R, LANES),
            state_blk,
            state_blk,
        ] + [pl.BlockSpec(memory_space=pl.ANY)] * len(prev_big),
        out_specs=[row_blk, row_blk, conv_blk, conv_blk, row_blk, lane_blk, state_blk, state_blk],
        out_shape=[
            f32(R, D_MODEL), f32(R, D_MODEL), f32(R, (CONV_W - 1) * D_MODEL),
            f32(R, (CONV_W - 1) * D_MODEL), f32(R, D_MODEL), f32(R, LANES),
            f32(DEPTH, R, N_HEADS, DH, DH), f32(DEPTH, R, N_HEADS, DH, DH),
        ],
        scratch_shapes=(
            [pltpu.VMEM((R, D_MODEL), _F32) for _ in range(10)]
            + [pltpu.VMEM((D_MODEL, R), _F32) for _ in range(4)]),
        compiler_params=pltpu.CompilerParams(
            dimension_semantics=("arbitrary",), vmem_limit_bytes=VMEM_LIMIT),
        name="sample_mix",
    )(proj, x, mod, vec, bif2, inv, wax, wq, wk, wv, wif2, wout,
      rgh, rgc, mlc, mln, mlm, mlC_all, retS_all, *prev_big)
    y, o_rgh, o_rgc, o_mlc, o_mln, o_mlm, o_C, o_S = outs
    return y, (o_rgh, o_rgc.reshape(R, CONV_W - 1, D_MODEL),
               o_mln.reshape(R, N_HEADS, DH), o_mlm[:, :N_HEADS],
               o_mlc.reshape(R, CONV_W - 1, D_MODEL)), (o_C, o_S)


def _pack_params(norm_pre, norm_post, rg_conv_w, rg_conv_b, rg_b_a, rg_b_x, rg_lambda,
                 ml_conv_w, ml_conv_b, ml_skip, ml_gn, ret_gn):
    row = lambda p: p[:, None, :]
    rows = [row(norm_pre), row(norm_post), rg_conv_w, row(rg_conv_b), row(0.5 * rg_b_a), row(0.5 * rg_b_x),
            row(rg_lambda), ml_conv_w, row(ml_conv_b), row(ml_skip), row(ml_gn), row(ret_gn)]
    vec = jnp.concatenate(rows, axis=1)
    return jnp.pad(vec, ((0, 0), (0, V_ROWS - vec.shape[1]), (0, 0)))


CAST_CHUNKS = 8


def _cast_chunks_kernel(w_ref, o_ref):
    for c in range(o_ref.shape[0]):
        o_ref[c] = w_ref[:, c * W_CHUNK:(c + 1) * W_CHUNK].astype(_BF16)


def _column_chunks(w):
    depth, K, N = w.shape
    n_chunks = N // W_CHUNK
    per_step = math.gcd(n_chunks, CAST_CHUNKS)
    return pl.pallas_call(
        _cast_chunks_kernel,
        grid=(depth, n_chunks // per_step),
        in_specs=[pl.BlockSpec((None, K, per_step * W_CHUNK), lambda l, s: (l, 0, s))],
        out_specs=pl.BlockSpec((None, per_step, K, W_CHUNK), lambda l, s: (l, s, 0, 0)),
        out_shape=jax.ShapeDtypeStruct((depth, n_chunks, K, W_CHUNK), _BF16),
        compiler_params=pltpu.CompilerParams(vmem_limit_bytes=VMEM_LIMIT),
        name="cast_chunks",
    )(w)


def kernel(x_prompt, x_sample, c_prompt, c_sample, state_rg_h, state_rg_conv, state_ml_C, state_ml_n, state_ml_m, state_ml_conv, state_ret_S, w_ada, b_ada, norm_pre, norm_post, w_in, rg_conv_w, rg_conv_b, rg_w_a, rg_b_a, rg_w_x, rg_b_x, rg_lambda, ml_conv_w, ml_conv_b, ml_w_q, ml_w_k, ml_w_v, ml_w_if, ml_b_if, ml_skip, ml_gn, ret_gn, w_out):
    xs = x_sample[:, 0, :]
    R = xs.shape[0]
    mod_all = _ada_call(jnp.concatenate([c_sample, c_prompt], axis=0), w_ada, b_ada)
    inv = (ROPE_BASE ** (-jnp.linspace(0.0, 1.0, HALF, dtype=_F32)))[None, :]
    vec = _pack_params(norm_pre, norm_post, rg_conv_w, rg_conv_b, rg_b_a, rg_b_x,
                       rg_lambda, ml_conv_w, ml_conv_b, ml_skip, ml_gn, ret_gn)
    win = _column_chunks(w_in)
    wout = _column_chunks(w_out)
    wax = (0.5 * jnp.concatenate([rg_w_a, rg_w_x], axis=-1)).astype(_BF16)
    wq, wk, wv = ml_w_q, ml_w_k, ml_w_v
    lane_pad = ((0, 0), (0, 0), (0, LANES - N_HEADS))
    wif2 = jnp.concatenate([jnp.pad(ml_w_if[..., :N_HEADS], lane_pad),
                            jnp.pad(ml_w_if[..., N_HEADS:], lane_pad)], axis=-1).astype(_BF16)
    b_if = ml_b_if[:, None, :]
    bif2 = jnp.concatenate([jnp.pad(b_if[..., :N_HEADS], lane_pad),
                            jnp.pad(b_if[..., N_HEADS:], lane_pad)], axis=-1)
    s_rgc_in = state_rg_conv.reshape(DEPTH, R, -1)
    s_mlc_in = state_ml_conv.reshape(DEPTH, R, -1)
    s_mln_in = state_ml_n.reshape(DEPTH, R, -1)
    s_mlm_in = jnp.pad(state_ml_m, lane_pad)

    xp = _interleave_tiles(x_prompt)
    st_p, st_s = [], []
    big_p, big_s = (), ()
    for l in range(DEPTH):
        xp, sp, big_p = _prompt_layer(l, big_p, R, xp, mod_all, vec, bif2, inv, win, wax, wq, wk, wv, wif2, wout)
        st_p.append(sp)
        proj = _sample_proj(l, xs, mod_all, vec, win)
        xs, ss, big_s = _sample_mix(
            l, big_s, proj, xs, mod_all, vec, bif2, inv, wax, wq, wk, wv, wif2, wout,
            state_rg_h, s_rgc_in, s_mlc_in, s_mln_in, s_mlm_in, state_ml_C, state_ret_S)
        st_s.append(ss)
    p_rgh, p_rgc, p_n, p_m, p_mlc = [jnp.stack(a) for a in zip(*st_p)]
    s_rgh, s_rgc, s_n, s_m, s_mlc = [jnp.stack(a) for a in zip(*st_s)]
    return (_deinterleave_tiles(xp), xs[:, None, :], p_rgh, p_rgc, big_p[0], p_n, p_m, p_mlc, big_p[1],
            s_rgh, s_rgc, big_s[0], s_n, s_m, s_mlc, big_s[1])
```
